```python
import math, functools
import jax, jax.numpy as jnp
from jax import lax
import numpy as np

D_MODEL = 2048
BATCH = 4
SEQ = 2048
DEPTH = 1
DEC_BATCH = 128
DEC_SEQ = 4
PAST_LEN = 16384
PAGE_SIZE = 128

MLA_HEADS = D_MODEL // 256
V_HEAD = 128
MLA_WIDTH = MLA_HEADS * V_HEAD
LRU_WIDTH = D_MODEL - MLA_WIDTH
QK_NOPE = 128
QK_ROPE = 64
Q_LORA = 512
KV_LORA = 256
ROPE_THETA = 10000.0
SM_SCALE = (QK_NOPE + QK_ROPE) ** -0.5
Q_BLOCK = 128
NEG_INF = -1e30
LRU_BLOCKS = 8
LRU_BLOCK = LRU_WIDTH // LRU_BLOCKS
CONV_W = 4
LRU_C = 8.0
N_GROUPS = 4
EXPERTS_PER_GROUP = 8
N_EXPERTS = N_GROUPS * EXPERTS_PER_GROUP
TOP_K = 2
D_EXPERT = 512
PLE_DIM = 256
EPS = 1e-6
IN_COLS = Q_LORA + KV_LORA + QK_ROPE + 2 * LRU_WIDTH
SPLITS = (Q_LORA, Q_LORA + KV_LORA, Q_LORA + KV_LORA + QK_ROPE, Q_LORA + KV_LORA + QK_ROPE + LRU_WIDTH)

kernel_name = 'hymba_mla_rglru_hmoe_step'


def rmsnorm(x, g):
    xf = x.astype(jnp.float32)
    y = xf * lax.rsqrt(jnp.mean(jnp.square(xf), axis=-1, keepdims=True) + EPS)
    return (y * g.astype(jnp.float32)).astype(x.dtype)


def apply_rope(x, pos):
    half = x.shape[-1] // 2
    inv = ROPE_THETA ** (-jnp.arange(half, dtype=jnp.float32) / half)
    ang = pos.astype(jnp.float32)[:, None] * inv[None, :]
    shape = (1, pos.shape[0]) + (1,) * (x.ndim - 3) + (half,)
    cos = jnp.cos(ang).reshape(shape).astype(x.dtype)
    sin = jnp.sin(ang).reshape(shape).astype(x.dtype)
    x1, x2 = x[..., :half], x[..., half:]
    return jnp.concatenate([x1 * cos - x2 * sin, x1 * sin + x2 * cos], axis=-1)


def mixer_projections(u, pos, w_in, g_q, w_uq, g_kv, w_ukv):
    B, T, _ = u.shape
    z = u @ w_in
    c_q, c_kv, k_r, x_br, y_br = jnp.split(z, SPLITS, axis=-1)
    q = (rmsnorm(c_q, g_q) @ w_uq).reshape(B, T, MLA_HEADS, QK_NOPE + QK_ROPE)
    q_rope = apply_rope(q[..., QK_NOPE:], pos)
    q_lat = jnp.einsum('bthn,chn->bthc', q[..., :QK_NOPE], w_ukv[..., :QK_NOPE])
    lat = rmsnorm(c_kv, g_kv)
    k_rope = apply_rope(k_r, pos)
    return q_lat, q_rope, lat, k_rope, x_br, y_br


def _scores(q_lat, q_rope, lat, k_rope):
    s = jnp.einsum('bthc,bsc->bhts', q_lat, lat) + jnp.einsum('bthr,bsr->bhts', q_rope, k_rope)
    return s.astype(jnp.float32) * SM_SCALE


def mla_prompt_attention(q_lat, q_rope, lat, k_rope):
    B, S = q_lat.shape[:2]
    nb = S // Q_BLOCK
    key_pos = jnp.arange(S)

    def block(args):
        ql, qr, qpos = args
        s = _scores(ql, qr, lat, k_rope)
        s = jnp.where(key_pos[None, :] <= qpos[:, None], s, NEG_INF)
        p = jax.nn.softmax(s, axis=-1).astype(lat.dtype)
        return jnp.einsum('bhts,bsc->bthc', p, lat)

    qlb = q_lat.reshape(B, nb, Q_BLOCK, MLA_HEADS, KV_LORA).swapaxes(0, 1)
    qrb = q_rope.reshape(B, nb, Q_BLOCK, MLA_HEADS, QK_ROPE).swapaxes(0, 1)
    qpos = jnp.arange(S).reshape(nb, Q_BLOCK)
    o = lax.map(block, (qlb, qrb, qpos))
    return o.swapaxes(0, 1).reshape(B, S, MLA_HEADS, KV_LORA)


def mla_sample_attention(q_lat, q_rope, lat, k_rope, lat_past, rope_past):
    T = q_lat.shape[1]
    P = lat_past.shape[1]
    s_past = _scores(q_lat, q_rope, lat_past, rope_past)
    s_new = _scores(q_lat, q_rope, lat, k_rope)
    s_new = jnp.where(jnp.tril(jnp.ones((T, T), dtype=bool)), s_new, NEG_INF)
    p = jax.nn.softmax(jnp.concatenate([s_past, s_new], axis=-1), axis=-1).astype(lat.dtype)
    return (jnp.einsum('bhts,bsc->bthc', p[..., :P], lat_past)
            + jnp.einsum('bhts,bsc->bthc', p[..., P:], lat))


def causal_conv(x_br, conv_state, w_conv, b_conv):
    T = x_br.shape[1]
    xx = jnp.concatenate([conv_state.astype(x_br.dtype), x_br], axis=1)
    out = b_conv
    for k in range(CONV_W):
        out = out + xx[:, k:k + T] * w_conv[k]
    return out, xx[:, -(CONV_W - 1):]


def rglru(xc, h0, w_rg, b_rg, w_ig, b_ig, lru_lambda):
    B, T, W = xc.shape
    xb = xc.reshape(B, T, LRU_BLOCKS, LRU_BLOCK)
    r = jax.nn.sigmoid((jnp.einsum('btnk,nkj->btnj', xb, w_rg).reshape(B, T, W) + b_rg).astype(jnp.float32))
    i = jax.nn.sigmoid((jnp.einsum('btnk,nkj->btnj', xb, w_ig).reshape(B, T, W) + b_ig).astype(jnp.float32))
    log_a = -LRU_C * r * jax.nn.softplus(-lru_lambda.astype(jnp.float32))
    a = jnp.exp(log_a)
    u = jnp.sqrt(-jnp.expm1(2.0 * log_a)) * i * xc.astype(jnp.float32)

    def step(h, au):
        a_t, u_t = au
        h = a_t * h + u_t
        return h, h

    h_last, hs = lax.scan(step, h0.astype(jnp.float32), (a.swapaxes(0, 1), u.swapaxes(0, 1)))
    return hs.swapaxes(0, 1).astype(xc.dtype), h_last


def hier_moe(u, w_group, b_group, w_router, b_router, w_gate, w_up, w_down):
    shp = u.shape
    t = u.reshape(-1, D_MODEL)
    g_prob = jax.nn.softmax((t @ w_group).astype(jnp.float32) + b_group.astype(jnp.float32), axis=-1)
    g_idx = jnp.argmax(g_prob, axis=-1)
    g_w = jnp.take_along_axis(g_prob, g_idx[:, None], axis=-1)
    e_logits = ((t @ w_router).astype(jnp.float32) + b_router.astype(jnp.float32)).reshape(-1, N_GROUPS, EXPERTS_PER_GROUP)
    e_logits = jnp.take_along_axis(e_logits, g_idx[:, None, None], axis=1)[:, 0]
    top_v, top_i = lax.top_k(jax.nn.softmax(e_logits, axis=-1), TOP_K)
    w_k = g_w * top_v / jnp.sum(top_v, axis=-1, keepdims=True)
    eid = g_idx[:, None] * EXPERTS_PER_GROUP + top_i
    combine = jnp.sum(jax.nn.one_hot(eid, N_EXPERTS, dtype=jnp.float32) * w_k[..., None], axis=1)
    hg = jnp.einsum('nd,edf->nef', t, w_gate)
    hu = jnp.einsum('nd,edf->nef', t, w_up)
    act = jax.nn.silu(hg) * hu * combine[..., None].astype(t.dtype)
    return jnp.einsum('nef,efd->nd', act, w_down).reshape(shp)


def trunk_layer(h, p_l, pos, conv_state, lru_state, attend, w):
    B, T, _ = h.shape
    u = rmsnorm(h, w['g_mix'])
    q_lat, q_rope, lat, k_rope, x_br, y_br = mixer_projections(
        u, pos, w['w_in'], w['g_q'], w['w_uq'], w['g_kv'], w['w_ukv'])
    o_lat = attend(q_lat, q_rope, lat, k_rope)
    o_mla = jnp.einsum('bthc,chv->bthv', o_lat, w['w_ukv'][..., QK_NOPE:]).reshape(B, T, MLA_WIDTH)
    xc, new_conv = causal_conv(x_br, conv_state, w['w_conv'], w['b_conv'])
    o_rec, new_lru = rglru(xc, lru_state, w['w_rg'], w['b_rg'], w['w_ig'], w['b_ig'], w['lru_lambda'])
    o_lru = jax.nn.gelu(y_br) * o_rec
    mixed = jnp.concatenate([rmsnorm(o_mla, w['g_out_mla']), rmsnorm(o_lru, w['g_out_lru'])], axis=-1)
    h = h + mixed @ w['w_o']
    h = h + hier_moe(rmsnorm(h, w['g_ffn']), w['w_group'], w['b_group'], w['w_router'], w['b_router'],
                     w['w_gate'], w['w_up'], w['w_down'])
    gate = jax.nn.sigmoid(rmsnorm(h, w['g_ple']) @ w['w_ple_gate'] + w['b_ple_gate'])
    h = h + gate * (p_l @ w['w_ple_proj'])
    return h, (lat, k_rope, new_lru, new_conv)


def setup_inputs(seed: int = 0) -> dict:
    key = jax.random.key(seed)
    ks = jax.random.split(key, 64)
    cnt = [0]

    def nk():
        cnt[0] += 1
        return ks[cnt[0] - 1]

    def nrm(shape, scale=1.0):
        return jax.random.normal(nk(), shape, jnp.float32) * scale

    def gain(shape):
        return 1.0 + 0.05 * jax.random.normal(nk(), shape, jnp.float32)

    n_pages = PAST_LEN // PAGE_SIZE
    n_pool = (5 * DEC_BATCH * n_pages) // 4
    perm = jax.random.permutation(nk(), n_pool)[: DEC_BATCH * n_pages]
    page_table = perm.reshape(DEC_BATCH, n_pages).astype(jnp.int32)
    a_c = jax.random.uniform(nk(), (DEPTH, LRU_WIDTH), jnp.float32, 0.9, 0.999)
    a_base = a_c ** (1.0 / LRU_C)
    lru_lambda = jnp.log(a_base) - jnp.log1p(-a_base)
    return {
        'x_prompt': nrm((BATCH, SEQ, D_MODEL)),
        'x_sample': nrm((DEC_BATCH, DEC_SEQ, D_MODEL)),
        'p_prompt': nrm((DEPTH, BATCH, SEQ, PLE_DIM)),
        'p_sample': nrm((DEPTH, DEC_BATCH, DEC_SEQ, PLE_DIM)),
        'cache_latent': nrm((DEPTH, n_pool, PAGE_SIZE, KV_LORA)),
        'cache_krope': nrm((DEPTH, n_pool, PAGE_SIZE, QK_ROPE)),
        'state_lru': nrm((DEPTH, DEC_BATCH, LRU_WIDTH), 0.5),
        'state_conv': nrm((DEPTH, DEC_BATCH, CONV_W - 1, LRU_WIDTH)),
        'page_table': page_table,
        'g_mix': gain((DEPTH, D_MODEL)),
        'w_in': nrm((DEPTH, D_MODEL, IN_COLS), D_MODEL ** -0.5),
        'g_q': gain((DEPTH, Q_LORA)),
        'w_uq': nrm((DEPTH, Q_LORA, MLA_HEADS * (QK_NOPE + QK_ROPE)), Q_LORA ** -0.5),
        'g_kv': gain((DEPTH, KV_LORA)),
        'w_ukv': nrm((DEPTH, KV_LORA, MLA_HEADS, QK_NOPE + V_HEAD), KV_LORA ** -0.5),
        'w_conv': nrm((DEPTH, CONV_W, LRU_WIDTH), CONV_W ** -0.5),
        'b_conv': nrm((DEPTH, LRU_WIDTH), 0.02),
        'w_rg': nrm((DEPTH, LRU_BLOCKS, LRU_BLOCK, LRU_BLOCK), LRU_BLOCK ** -0.5),
        'b_rg': nrm((DEPTH, LRU_WIDTH), 0.1),
        'w_ig': nrm((DEPTH, LRU_BLOCKS, LRU_BLOCK, LRU_BLOCK), LRU_BLOCK ** -0.5),
        'b_ig': nrm((DEPTH, LRU_WIDTH), 0.1),
        'lru_lambda': lru_lambda,
        'g_out_mla': gain((DEPTH, MLA_WIDTH)),
        'g_out_lru': gain((DEPTH, LRU_WIDTH)),
        'w_o': nrm((DEPTH, D_MODEL, D_MODEL), D_MODEL ** -0.5),
        'g_ffn': gain((DEPTH, D_MODEL)),
        'w_group': nrm((DEPTH, D_MODEL, N_GROUPS), D_MODEL ** -0.5),
        'b_group': nrm((DEPTH, N_GROUPS), 0.01),
        'w_router': nrm((DEPTH, D_MODEL, N_EXPERTS), D_MODEL ** -0.5),
        'b_router': nrm((DEPTH, N_EXPERTS), 0.01),
        'w_gate': nrm((DEPTH, N_EXPERTS, D_MODEL, D_EXPERT), D_MODEL ** -0.5),
        'w_up': nrm((DEPTH, N_EXPERTS, D_MODEL, D_EXPERT), D_MODEL ** -0.5),
        'w_down': nrm((DEPTH, N_EXPERTS, D_EXPERT, D_MODEL), D_EXPERT ** -0.5),
        'g_ple': gain((DEPTH, D_MODEL)),
        'w_ple_gate': nrm((DEPTH, D_MODEL, D_MODEL), D_MODEL ** -0.5),
        'b_ple_gate': nrm((DEPTH, D_MODEL), 0.01),
        'w_ple_proj': nrm((DEPTH, PLE_DIM, D_MODEL), PLE_DIM ** -0.5),
        'g_final': gain((D_MODEL,)),
    }


def reference(x_prompt, x_sample, p_prompt, p_sample, cache_latent, cache_krope, state_lru, state_conv,
              page_table, g_mix, w_in, g_q, w_uq, g_kv, w_ukv, w_conv, b_conv, w_rg, b_rg, w_ig, b_ig,
              lru_lambda, g_out_mla, g_out_lru, w_o, g_ffn, w_group, b_group, w_router, b_router,
              w_gate, w_up, w_down, g_ple, w_ple_gate, b_ple_gate, w_ple_proj, g_final):
    B, S, _ = x_prompt.shape
    DB, T, _ = x_sample.shape
    past_len = page_table.shape[1] * cache_latent.shape[2]
    pos_p = jnp.arange(S)
    pos_s = past_len + jnp.arange(T)
    h_p, h_s = x_prompt, x_sample
    st_p, st_s = [], []
    for l in range(DEPTH):
        lw = dict(g_mix=g_mix[l], w_in=w_in[l], g_q=g_q[l], w_uq=w_uq[l], g_kv=g_kv[l], w_ukv=w_ukv[l],
                  w_conv=w_conv[l], b_conv=b_conv[l], w_rg=w_rg[l], b_rg=b_rg[l], w_ig=w_ig[l], b_ig=b_ig[l],
                  lru_lambda=lru_lambda[l], g_out_mla=g_out_mla[l], g_out_lru=g_out_lru[l], w_o=w_o[l],
                  g_ffn=g_ffn[l], w_group=w_group[l], b_group=b_group[l], w_router=w_router[l],
                  b_router=b_router[l], w_gate=w_gate[l], w_up=w_up[l], w_down=w_down[l], g_ple=g_ple[l],
                  w_ple_gate=w_ple_gate[l], b_ple_gate=b_ple_gate[l], w_ple_proj=w_ple_proj[l])
        zero_conv = jnp.zeros((B, CONV_W - 1, LRU_WIDTH), x_prompt.dtype)
        zero_h = jnp.zeros((B, LRU_WIDTH), jnp.float32)
        h_p, sp = trunk_layer(h_p, p_prompt[l], pos_p, zero_conv, zero_h, mla_prompt_attention, lw)
        lat_past = cache_latent[l][page_table].reshape(DB, past_len, KV_LORA)
        rope_past = cache_krope[l][page_table].reshape(DB, past_len, QK_ROPE)
        attend_s = functools.partial(mla_sample_attention, lat_past=lat_past, rope_past=rope_past)
        h_s, ss = trunk_layer(h_s, p_sample[l], pos_s, state_conv[l], state_lru[l], attend_s, lw)
        st_p.append(sp)
        st_s.append(ss)
    y_prompt = rmsnorm(h_p, g_final)
    y_sample = rmsnorm(h_s, g_final)
    new_latent_prompt = jnp.stack([s[0] for s in st_p])
    new_krope_prompt = jnp.stack([s[1] for s in st_p])
    new_lru_prompt = jnp.stack([s[2] for s in st_p])
    new_conv_prompt = jnp.stack([s[3] for s in st_p])
    new_latent_sample = jnp.stack([s[0] for s in st_s])
    new_krope_sample = jnp.stack([s[1] for s in st_s])
    new_lru_sample = jnp.stack([s[2] for s in st_s])
    new_conv_sample = jnp.stack([s[3] for s in st_s])
    return (y_prompt, y_sample, new_latent_prompt, new_krope_prompt, new_lru_prompt, new_conv_prompt,
            new_latent_sample, new_krope_sample, new_lru_sample, new_conv_sample)
```

```python
import functools

import jax
import jax.numpy as jnp
from jax import lax
from jax.experimental import pallas as pl
from jax.experimental.pallas import tpu as pltpu

F32 = jnp.float32
BF16 = jnp.bfloat16

D_MODEL = 2048
MLA_HEADS = 8
V_HEAD = 128
MLA_WIDTH = MLA_HEADS * V_HEAD
LRU_WIDTH = D_MODEL - MLA_WIDTH
QK_NOPE = 128
QK_ROPE = 64
Q_LORA = 512
KV_LORA = 256
ROPE_THETA = 10000.0
SM_SCALE = (QK_NOPE + QK_ROPE) ** -0.5
NEG_INF = -1e30
LRU_BLOCKS = 8
LRU_BLOCK = LRU_WIDTH // LRU_BLOCKS
CONV_W = 4
LRU_C = 8.0
N_GROUPS = 4
EXPERTS_PER_GROUP = 8
N_EXPERTS = N_GROUPS * EXPERTS_PER_GROUP
D_EXPERT = 512
PLE_DIM = 256
EPS = 1e-6

LANES = 128
ROPE_PAD = LANES
QK_CAT = KV_LORA + ROPE_PAD
C_Q0, C_KV0, C_KRA0, C_KRB0, C_X0, C_Y0, IN_EXT = 0, 512, 768, 896, 1024, 2048, 3072
ROUTE_LANES = LANES
VMEM_LIMIT = 56 * 1024 * 1024


def _cparams(**kw):
    return pltpu.CompilerParams(vmem_limit_bytes=VMEM_LIMIT, **kw)


def _rms(x, g):
    return x * lax.rsqrt(jnp.mean(x * x, axis=-1, keepdims=True) + EPS) * g


def _dot(a, b):
    return jnp.dot(a, b, preferred_element_type=F32)


def _dot_nt(a, b):
    return lax.dot_general(a, b, (((1,), (1,)), ((), ())), preferred_element_type=F32)


def _const_spec(shape):
    nd = len(shape)
    return pl.BlockSpec(shape, lambda *_: (0,) * nd, pipeline_mode=pl.Buffered(1))


def _proj_kernel(x_ref, cos_ref, sin_ref, gmix_ref, win_ref, gq_ref, wuq_ref, gkv_ref, wuk_ref,
                 qcat_ref, kcat_ref, lat_ref, krope_ref, xbr_ref, ybr_ref):
    u = _rms(x_ref[...], gmix_ref[...]).astype(BF16)
    z = _dot(u, win_ref[...])
    xbr_ref[...] = z[:, C_X0:C_Y0]
    ybr_ref[...] = z[:, C_Y0:IN_EXT]
    cos = cos_ref[...]
    sin = sin_ref[...]
    lat = _rms(z[:, C_KV0:C_KRA0], gkv_ref[...])
    kr = z[:, C_KRA0:C_KRB0] * cos + z[:, C_KRB0:C_X0] * sin
    lat_ref[...] = lat
    krope_ref[...] = kr[:, :QK_ROPE]
    kcat_ref[:, 0:KV_LORA] = lat.astype(BF16)
    kcat_ref[:, KV_LORA:QK_CAT] = kr.astype(BF16)
    qn = _rms(z[:, C_Q0:C_KV0], gq_ref[...]).astype(BF16)
    q = _dot(qn, wuq_ref[...])
    ra0 = MLA_HEADS * QK_NOPE
    rb0 = ra0 + MLA_HEADS * ROPE_PAD
    for h in range(MLA_HEADS):
        ql = _dot(q[:, h * QK_NOPE:(h + 1) * QK_NOPE].astype(BF16), wuk_ref[h])
        qr = (q[:, ra0 + h * ROPE_PAD:ra0 + (h + 1) * ROPE_PAD] * cos
              + q[:, rb0 + h * ROPE_PAD:rb0 + (h + 1) * ROPE_PAD] * sin)
        qcat_ref[:, h * QK_CAT:h * QK_CAT + KV_LORA] = ql.astype(BF16)
        qcat_ref[:, h * QK_CAT + KV_LORA:(h + 1) * QK_CAT] = qr.astype(BF16)


def _proj(x2d, cos, sin, pos_blocks, w, tm):
    n = x2d.shape[0]
    row = lambda width: pl.BlockSpec((tm, width), lambda i: (i, 0))
    pos = pl.BlockSpec((tm, ROPE_PAD), lambda i: (i % pos_blocks, 0))
    out_shape = (
        jax.ShapeDtypeStruct((n, MLA_HEADS * QK_CAT), BF16),
        jax.ShapeDtypeStruct((n, QK_CAT), BF16),
        jax.ShapeDtypeStruct((n, KV_LORA), F32),
        jax.ShapeDtypeStruct((n, QK_ROPE), F32),
        jax.ShapeDtypeStruct((n, LRU_WIDTH), F32),
        jax.ShapeDtypeStruct((n, LRU_WIDTH), F32),
    )
    return pl.pallas_call(
        _proj_kernel,
        out_shape=out_shape,
        grid=(n // tm,),
        in_specs=[row(D_MODEL), pos, pos, _const_spec((1, D_MODEL)), _const_spec((D_MODEL, IN_EXT)),
                  _const_spec((1, Q_LORA)), _const_spec((Q_LORA, 3 * MLA_HEADS * LANES)),
                  _const_spec((1, KV_LORA)), _const_spec((MLA_HEADS, QK_NOPE, KV_LORA))],
        out_specs=(row(MLA_HEADS * QK_CAT), row(QK_CAT), row(KV_LORA), row(QK_ROPE), row(LRU_WIDTH), row(LRU_WIDTH)),
        compiler_params=_cparams(dimension_semantics=("arbitrary",)),
        name="proj",
    )(x2d, cos, sin, w["g_mix"], w["w_in"], w["g_q"], w["w_uq"], w["g_kv"], w["w_uk"])


def _prompt_attn_kernel(q_ref, k_ref, wuv_ref, o_ref, m_ref, l_ref, acc_ref, *, tq):
    qi = pl.program_id(1)
    q = jnp.concatenate([q_ref[:, h * QK_CAT:(h + 1) * QK_CAT] for h in range(MLA_HEADS)], axis=0)
    m_ref[...] = jnp.full(m_ref.shape, NEG_INF, F32)
    l_ref[...] = jnp.zeros(l_ref.shape, F32)
    acc_ref[...] = jnp.zeros(acc_ref.shape, F32)

    def step(kb, masked):
        k = k_ref[pl.ds(pl.multiple_of(kb * tq, tq), tq), :]
        s = _dot_nt(q, k) * SM_SCALE
        if masked:
            row = lax.broadcasted_iota(jnp.int32, s.shape, 0) & (tq - 1)
            col = lax.broadcasted_iota(jnp.int32, s.shape, 1)
            s = jnp.where(col <= row, s, NEG_INF)
        m_prev = m_ref[...]
        m_new = jnp.maximum(m_prev, jnp.max(s, axis=1, keepdims=True))
        alpha = jnp.exp(m_prev - m_new)
        p = jnp.exp(s - m_new)
        l_ref[...] = alpha * l_ref[...] + jnp.sum(p, axis=1, keepdims=True)
        acc_ref[...] = alpha * acc_ref[...] + _dot(p.astype(BF16), k[:, :KV_LORA])
        m_ref[...] = m_new

    def body(kb, carry):
        step(kb, False)
        return carry

    lax.fori_loop(0, qi, body, 0)
    step(qi, True)
    o = acc_ref[...] / l_ref[...]
    for h in range(MLA_HEADS):
        o_ref[:, h * V_HEAD:(h + 1) * V_HEAD] = _dot(o[h * tq:(h + 1) * tq].astype(BF16), wuv_ref[h])


def _prompt_attn(qcat, kcat, wuv, batch, seq, tq):
    nq = seq // tq
    m = MLA_HEADS * tq
    return pl.pallas_call(
        functools.partial(_prompt_attn_kernel, tq=tq),
        out_shape=jax.ShapeDtypeStruct((batch * seq, MLA_WIDTH), F32),
        grid=(batch, nq),
        in_specs=[pl.BlockSpec((tq, MLA_HEADS * QK_CAT), lambda b, i: (b * nq + i, 0)),
                  pl.BlockSpec((seq, QK_CAT), lambda b, i: (b, 0)),
                  _const_spec((MLA_HEADS, KV_LORA, V_HEAD))],
        out_specs=pl.BlockSpec((tq, MLA_WIDTH), lambda b, i: (b * nq + i, 0)),
        scratch_shapes=[pltpu.VMEM((m, 1), F32), pltpu.VMEM((m, 1), F32), pltpu.VMEM((m, KV_LORA), F32)],
        compiler_params=_cparams(dimension_semantics=("arbitrary", "arbitrary")),
        name="prompt_attn",
    )(qcat, kcat, wuv)


def _sample_attn_kernel(pt_ref, q_ref, knew_ref, lat_hbm, rope_hbm, wuv_ref, o_ref,
                        latbuf, ropebuf, sem, m_ref, l_ref, acc_ref, *, pages, page, n_chunks, t_new):
    c = pl.program_id(1)
    step = pl.program_id(0) * n_chunks + c
    n_steps = pl.num_programs(0) * n_chunks
    slot = step % 2

    def start_chunk(step_, slot_):
        for g in range(pages):
            pid = pt_ref[step_ * pages + g]
            pltpu.make_async_copy(lat_hbm.at[pid], latbuf.at[slot_, g], sem.at[0, slot_]).start()
            pltpu.make_async_copy(rope_hbm.at[pid], ropebuf.at[slot_, g], sem.at[1, slot_]).start()

    @pl.when(step == 0)
    def _():
        start_chunk(step, slot)

    @pl.when(step + 1 < n_steps)
    def _():
        start_chunk(step + 1, 1 - slot)

    @pl.when(c == 0)
    def _():
        m_ref[...] = jnp.full(m_ref.shape, NEG_INF, F32)
        l_ref[...] = jnp.zeros(l_ref.shape, F32)
        acc_ref[...] = jnp.zeros(acc_ref.shape, F32)

    pltpu.make_async_copy(lat_hbm.at[pl.ds(0, pages)], latbuf.at[slot], sem.at[0, slot]).wait()
    pltpu.make_async_copy(rope_hbm.at[pl.ds(0, pages)], ropebuf.at[slot], sem.at[1, slot]).wait()

    q = q_ref[0]
    lat = latbuf[slot].reshape(pages * page, KV_LORA).astype(BF16)
    rope = ropebuf[slot].reshape(pages * page, QK_ROPE).astype(BF16)
    s = (_dot_nt(q[:, :KV_LORA], lat) + _dot_nt(q[:, KV_LORA:KV_LORA + QK_ROPE], rope)) * SM_SCALE
    m_prev = m_ref[...]
    m_new = jnp.maximum(m_prev, jnp.max(s, axis=1, keepdims=True))
    alpha = jnp.exp(m_prev - m_new)
    p = jnp.exp(s - m_new)
    l_ref[...] = alpha * l_ref[...] + jnp.sum(p, axis=1, keepdims=True)
    acc_ref[...] = alpha * acc_ref[...] + _dot(p.astype(BF16), lat)
    m_ref[...] = m_new

    @pl.when(c == n_chunks - 1)
    def _():
        qf = q.astype(F32)
        kn = knew_ref[0].astype(F32)
        tok = lax.broadcasted_iota(jnp.int32, (q.shape[0], 1), 0) & (t_new - 1)
        cols = []
        for j in range(t_new):
            sj = jnp.sum(qf * kn[j:j + 1, :], axis=1, keepdims=True) * SM_SCALE
            cols.append(jnp.where(tok >= j, sj, NEG_INF))
        m_prev = m_ref[...]
        m_new = m_prev
        for sj in cols:
            m_new = jnp.maximum(m_new, sj)
        alpha = jnp.exp(m_prev - m_new)
        l = alpha * l_ref[...]
        acc = alpha * acc_ref[...]
        for j, sj in enumerate(cols):
            pj = jnp.exp(sj - m_new)
            l = l + pj
            acc = acc + pj * kn[j:j + 1, :KV_LORA]
        o = (acc / l).astype(BF16)
        for h in range(MLA_HEADS):
            oh = _dot(o, wuv_ref[h])
            o_ref[0, :, h * V_HEAD:(h + 1) * V_HEAD] = oh[h * t_new:(h + 1) * t_new]


def _sample_attn(page_table, q_s, knew, cache_lat, cache_rope, wuv, pages):
    db, n_pages = page_table.shape
    page = cache_lat.shape[1]
    n_chunks = n_pages // pages
    rows = q_s.shape[1]
    t_new = rows // MLA_HEADS
    in_specs = [pl.BlockSpec((1, rows, QK_CAT), lambda b, c, pt: (b, 0, 0)),
                pl.BlockSpec((1, 8, QK_CAT), lambda b, c, pt: (b, 0, 0)),
                pl.BlockSpec(memory_space=pl.ANY), pl.BlockSpec(memory_space=pl.ANY),
                pl.BlockSpec((MLA_HEADS, KV_LORA, V_HEAD), lambda b, c, pt: (0, 0, 0))]
    grid_spec = pltpu.PrefetchScalarGridSpec(
        num_scalar_prefetch=1, grid=(db, n_chunks), in_specs=in_specs,
        out_specs=pl.BlockSpec((1, t_new, MLA_WIDTH), lambda b, c, pt: (b, 0, 0)),
        scratch_shapes=[pltpu.VMEM((2, pages, page, KV_LORA), F32), pltpu.VMEM((2, pages, page, QK_ROPE), F32),
                        pltpu.SemaphoreType.DMA((2, 2)), pltpu.VMEM((rows, 1), F32),
                        pltpu.VMEM((rows, 1), F32), pltpu.VMEM((rows, KV_LORA), F32)])
    return pl.pallas_call(
        functools.partial(_sample_attn_kernel, pages=pages, page=page, n_chunks=n_chunks, t_new=t_new),
        out_shape=jax.ShapeDtypeStruct((db, t_new, MLA_WIDTH), F32),
        grid_spec=grid_spec,
        compiler_params=_cparams(dimension_semantics=("arbitrary", "arbitrary")),
        name="sample_attn",
    )(page_table.reshape(-1), q_s, knew, cache_lat, cache_rope, wuv)


def _gates(xc, wrg_ref, brg_ref, wig_ref, big_ref, lam_ref):
    xb = xc.astype(BF16)
    r = jnp.concatenate([_dot(xb[:, n * LRU_BLOCK:(n + 1) * LRU_BLOCK], wrg_ref[n]) for n in range(LRU_BLOCKS)], axis=1)
    i = jnp.concatenate([_dot(xb[:, n * LRU_BLOCK:(n + 1) * LRU_BLOCK], wig_ref[n]) for n in range(LRU_BLOCKS)], axis=1)
    r = jax.nn.sigmoid(r + brg_ref[...])
    i = jax.nn.sigmoid(i + big_ref[...])
    neg_lam = -lam_ref[...]
    softplus = jnp.maximum(neg_lam, 0.0) + jnp.log(1.0 + jnp.exp(-jnp.abs(neg_lam)))
    log_a = -LRU_C * r * softplus
    a = jnp.exp(log_a)
    u = jnp.sqrt(1.0 - jnp.exp(2.0 * log_a)) * i * xc
    return a, u


def _gelu(y):
    return 0.5 * y * (1.0 + jnp.tanh(0.7978845608028654 * (y + 0.044715 * (y * y * y))))


def _route(logits):
    lane = lax.broadcasted_iota(jnp.int32, logits.shape, 1)
    far = jnp.int32(4 * ROUTE_LANES)
    gmask = lane < N_GROUPS
    gl = jnp.where(gmask, logits, NEG_INF)
    gmax = jnp.max(gl, axis=1, keepdims=True)
    gidx = jnp.min(jnp.where(gl == gmax, lane, far), axis=1, keepdims=True)
    g_w = 1.0 / jnp.sum(jnp.where(gmask, jnp.exp(gl - gmax), 0.0), axis=1, keepdims=True)
    lo = N_GROUPS + gidx * EXPERTS_PER_GROUP
    emask = jnp.logical_and(lane >= lo, lane < lo + EXPERTS_PER_GROUP)
    el = jnp.where(emask, logits, NEG_INF)
    emax = jnp.max(el, axis=1, keepdims=True)
    ex = jnp.where(emask, jnp.exp(el - emax), 0.0)
    prob = jnp.where(emask, ex / jnp.sum(ex, axis=1, keepdims=True), -1.0)
    p1 = jnp.max(prob, axis=1, keepdims=True)
    i1 = jnp.min(jnp.where(prob == p1, lane, far), axis=1, keepdims=True)
    rest = jnp.where(lane == i1, -1.0, prob)
    p2 = jnp.max(rest, axis=1, keepdims=True)
    i2 = jnp.min(jnp.where(rest == p2, lane, far), axis=1, keepdims=True)
    den = p1 + p2
    w1 = g_w * p1 / den
    w2 = g_w * p2 / den
    e1 = (i1 - N_GROUPS).astype(F32)
    e2 = (i2 - N_GROUPS).astype(F32)
    out = jnp.where(lane == 0, e1, jnp.where(lane == 1, e2, jnp.where(lane == 2, w1, jnp.where(lane == 3, w2, 0.0))))
    return out


def _mix_tail(o_mla, o_lru, x, gmla_ref, glru_ref, wo_ref, gffn_ref, wr_ref, br_ref):
    mixed = jnp.concatenate([_rms(o_mla, gmla_ref[...]), _rms(o_lru, glru_ref[...])], axis=1).astype(BF16)
    h1 = x + _dot(mixed, wo_ref[...])
    t = _rms(h1, gffn_ref[...])
    logits = _dot(t.astype(BF16), wr_ref[...]) + br_ref[...]
    return h1, t, _route(logits)


def _mix_prompt_kernel(xbr_ref, ybr_ref, omla_ref, x_ref, wconv_ref, bconv_ref, wrg_ref, brg_ref, wig_ref, big_ref,
                       lam_ref, gmla_ref, glru_ref, wo_ref, gffn_ref, wr_ref, br_ref,
                       h1_ref, t_ref, route_ref, lru_ref, hc_ref, tail_ref, *, tc):
    @pl.when(pl.program_id(1) == 0)
    def _():
        hc_ref[...] = jnp.zeros(hc_ref.shape, F32)
        tail_ref[...] = jnp.zeros(tail_ref.shape, F32)

    xb = xbr_ref[...]
    tail = tail_ref[...]
    row8 = lax.broadcasted_iota(jnp.int32, tail.shape, 0)
    xc = bconv_ref[...] + xb * wconv_ref[CONV_W - 1:CONV_W, :]
    for j in range(1, CONV_W):
        sh = pltpu.roll(xb, j, axis=0)
        top = jnp.where(row8 < j, pltpu.roll(tail, j, axis=0), sh[0:8])
        sh = jnp.concatenate([top, sh[8:]], axis=0)
        xc = xc + sh * wconv_ref[CONV_W - 1 - j:CONV_W - j, :]
    tail_ref[...] = xb[tc - 8:tc]

    a, u = _gates(xc, wrg_ref, brg_ref, wig_ref, big_ref, lam_ref)
    row = lax.broadcasted_iota(jnp.int32, a.shape, 0)
    d = 1
    while d < tc:
        keep = row >= d
        u = jnp.where(keep, a * pltpu.roll(u, d, axis=0) + u, u)
        a = jnp.where(keep, a * pltpu.roll(a, d, axis=0), a)
        d *= 2
    hs = a * hc_ref[...] + u
    hc_ref[...] = hs[tc - 1:tc]
    lru_ref[...] = hs[tc - 1:tc]

    o_lru = _gelu(ybr_ref[...]) * hs
    h1, t, route = _mix_tail(omla_ref[...], o_lru, x_ref[...], gmla_ref, glru_ref, wo_ref, gffn_ref, wr_ref, br_ref)
    h1_ref[...] = h1
    t_ref[...] = t
    route_ref[...] = route


def _mix_weight_specs():
    return [_const_spec((CONV_W, LRU_WIDTH)), _const_spec((1, LRU_WIDTH)),
            _const_spec((LRU_BLOCKS, LRU_BLOCK, LRU_BLOCK)), _const_spec((1, LRU_WIDTH)),
            _const_spec((LRU_BLOCKS, LRU_BLOCK, LRU_BLOCK)), _const_spec((1, LRU_WIDTH)),
            _const_spec((1, LRU_WIDTH)), _const_spec((1, MLA_WIDTH)), _const_spec((1, LRU_WIDTH)),
            _const_spec((D_MODEL, D_MODEL)), _const_spec((1, D_MODEL)),
            _const_spec((D_MODEL, ROUTE_LANES)), _const_spec((1, ROUTE_LANES))]


def _mix_weights(w):
    return (w["w_conv"], w["b_conv"], w["w_rg"], w["b_rg"], w["w_ig"], w["b_ig"], w["lru_lambda"],
            w["g_out_mla"], w["g_out_lru"], w["w_o"], w["g_ffn"], w["w_route"], w["b_route"])


def _mix_prompt(xbr, ybr, omla, x2d, w, batch, seq, tc):
    nt = seq // tc
    n = batch * seq
    row = lambda width: pl.BlockSpec((tc, width), lambda b, i: (b * nt + i, 0))
    return pl.pallas_call(
        functools.partial(_mix_prompt_kernel, tc=tc),
        out_shape=(jax.ShapeDtypeStruct((n, D_MODEL), F32), jax.ShapeDtypeStruct((n, D_MODEL), F32),
                   jax.ShapeDtypeStruct((n, ROUTE_LANES), F32), jax.ShapeDtypeStruct((batch, 1, LRU_WIDTH), F32)),
        grid=(batch, nt),
        in_specs=[row(LRU_WIDTH), row(LRU_WIDTH), row(MLA_WIDTH), row(D_MODEL)] + _mix_weight_specs(),
        out_specs=(row(D_MODEL), row(D_MODEL), row(ROUTE_LANES),
                   pl.BlockSpec((None, 1, LRU_WIDTH), lambda b, i: (b, 0, 0))),
        scratch_shapes=[pltpu.VMEM((1, LRU_WIDTH), F32), pltpu.VMEM((8, LRU_WIDTH), F32)],
        compiler_params=_cparams(dimension_semantics=("arbitrary", "arbitrary")),
        name="mix_prompt",
    )(xbr, ybr, omla, x2d, *_mix_weights(w))


def _mix_sample_kernel(xbr_ref, ybr_ref, omla_ref, x_ref, sconv_ref, slru_ref, wconv_ref, bconv_ref, wrg_ref, brg_ref,
                       wig_ref, big_ref, lam_ref, gmla_ref, glru_ref, wo_ref, gffn_ref, wr_ref, br_ref,
                       h1_ref, t_ref, route_ref, lru_ref, *, t_new):
    w_ = LRU_WIDTH
    xs = [sconv_ref[:, k * w_:(k + 1) * w_] for k in range(CONV_W - 1)] + [xbr_ref[:, t * w_:(t + 1) * w_] for t in range(t_new)]
    xc = []
    for t in range(t_new):
        acc = bconv_ref[...] + xs[t] * wconv_ref[0:1, :]
        for k in range(1, CONV_W):
            acc = acc + xs[t + k] * wconv_ref[k:k + 1, :]
        xc.append(acc)
    xc = jnp.concatenate(xc, axis=0)
    a, u = _gates(xc, wrg_ref, brg_ref, wig_ref, big_ref, lam_ref)
    nb = xbr_ref.shape[0]
    h = slru_ref[...]
    hs = []
    for t in range(t_new):
        h = a[t * nb:(t + 1) * nb] * h + u[t * nb:(t + 1) * nb]
        hs.append(h)
    lru_ref[...] = h
    hs = jnp.concatenate(hs, axis=0)
    stack = lambda ref, width: jnp.concatenate([ref[:, t * width:(t + 1) * width] for t in range(t_new)], axis=0)
    o_lru = _gelu(stack(ybr_ref, w_)) * hs
    h1, tt, route = _mix_tail(stack(omla_ref, MLA_WIDTH), o_lru, stack(x_ref, D_MODEL),
                              gmla_ref, glru_ref, wo_ref, gffn_ref, wr_ref, br_ref)
    for t in range(t_new):
        h1_ref[:, t * D_MODEL:(t + 1) * D_MODEL] = h1[t * nb:(t + 1) * nb]
        t_ref[:, t * D_MODEL:(t + 1) * D_MODEL] = tt[t * nb:(t + 1) * nb]
        route_ref[:, t * ROUTE_LANES:(t + 1) * ROUTE_LANES] = route[t * nb:(t + 1) * nb]


def _mix_sample(xbr, ybr, omla, x2d, sconv, slru, w, t_new):
    nb = slru.shape[0]
    full = lambda a: pl.BlockSpec(a.shape, lambda i: (0, 0))
    ins = (xbr, ybr, omla, x2d, sconv, slru)
    return pl.pallas_call(
        functools.partial(_mix_sample_kernel, t_new=t_new),
        out_shape=(jax.ShapeDtypeStruct((nb, t_new * D_MODEL), F32), jax.ShapeDtypeStruct((nb, t_new * D_MODEL), F32),
                   jax.ShapeDtypeStruct((nb, t_new * ROUTE_LANES), F32), jax.ShapeDtypeStruct((nb, LRU_WIDTH), F32)),
        grid=(1,),
        in_specs=[full(a) for a in ins] + _mix_weight_specs(),
        out_specs=(pl.BlockSpec((nb, t_new * D_MODEL), lambda i: (0, 0)), pl.BlockSpec((nb, t_new * D_MODEL), lambda i: (0, 0)),
                   pl.BlockSpec((nb, t_new * ROUTE_LANES), lambda i: (0, 0)), pl.BlockSpec((nb, LRU_WIDTH), lambda i: (0, 0))),
        compiler_params=_cparams(dimension_semantics=("arbitrary",)),
        name="mix_sample",
    )(*ins, *_mix_weights(w))


def _moe_kernel(te_ref, nv_ref, src_ref, dst_ref, t_hbm, wg_ref, wu_ref, wd_ref, y_hbm, xbuf, obuf, gsem, ssem, *, tm):
    del te_ref
    j = pl.program_id(0)
    nv = pl.multiple_of(nv_ref[j], 8)
    base = j * tm

    @pl.when(j == 0)
    def _():
        xbuf[...] = jnp.zeros(xbuf.shape, F32)
        spare = y_hbm.shape[0] - 8
        init = pltpu.make_async_copy(xbuf.at[pl.ds(0, 8)], y_hbm.at[pl.ds(spare, 8)], ssem)
        init.start()
        init.wait()

    @pl.when(nv > 0)
    def _():
        def gather(r, carry):
            pltpu.make_async_copy(t_hbm.at[pl.ds(src_ref[base + r], 1)], xbuf.at[pl.ds(r, 1)], gsem).start()
            return carry

        lax.fori_loop(0, nv, gather, 0)
        pltpu.make_async_copy(t_hbm.at[pl.ds(0, nv)], xbuf.at[pl.ds(0, nv)], gsem).wait()
        x = xbuf[...].astype(BF16)
        hg = _dot(x, wg_ref[...].astype(BF16))
        hu = _dot(x, wu_ref[...].astype(BF16))
        act = (hg * jax.nn.sigmoid(hg) * hu).astype(BF16)
        obuf[...] = _dot(act, wd_ref[...].astype(BF16))

        def scatter(r, carry):
            pltpu.make_async_copy(obuf.at[pl.ds(r, 1)], y_hbm.at[pl.ds(dst_ref[base + r], 1)], ssem).start()
            return carry

        lax.fori_loop(0, nv, scatter, 0)
        pltpu.make_async_copy(obuf.at[pl.ds(0, nv)], y_hbm.at[pl.ds(0, nv)], ssem).wait()


def _moe(t_all, tile_expert, tile_rows, src_tok, dst_pair, w_gate, w_up, w_down, n_pair_rows, tm):
    n_tiles = tile_expert.shape[0]
    wspec = lambda a, b: pl.BlockSpec((None, a, b), lambda j, te, nv, src, dst: (te[j], 0, 0))
    grid_spec = pltpu.PrefetchScalarGridSpec(
        num_scalar_prefetch=4, grid=(n_tiles,),
        in_specs=[pl.BlockSpec(memory_space=pl.ANY), wspec(D_MODEL, D_EXPERT), wspec(D_MODEL, D_EXPERT),
                  wspec(D_EXPERT, D_MODEL)],
        out_specs=pl.BlockSpec(memory_space=pl.ANY),
        scratch_shapes=[pltpu.VMEM((tm, D_MODEL), F32), pltpu.VMEM((tm, D_MODEL), F32),
                        pltpu.SemaphoreType.DMA(()), pltpu.SemaphoreType.DMA(())])
    return pl.pallas_call(
        functools.partial(_moe_kernel, tm=tm),
        out_shape=jax.ShapeDtypeStruct((n_pair_rows, D_MODEL), F32),
        grid_spec=grid_spec,
        compiler_params=_cparams(dimension_semantics=("arbitrary",)),
        name="moe",
    )(tile_expert, tile_rows, src_tok, dst_pair, t_all, w_gate, w_up, w_down)


def _moe_plan(route_all, tm):
    n = route_all.shape[0]
    n_pairs = 2 * n
    eflat = jnp.concatenate([route_all[:, 0], route_all[:, 1]]).astype(jnp.int32)
    onehot = (eflat[:, None] == jnp.arange(N_EXPERTS, dtype=jnp.int32)[None, :]).astype(jnp.int32)
    csum = jnp.cumsum(onehot, axis=0)
    rank = jnp.take_along_axis(csum, eflat[:, None], axis=1)[:, 0] - 1
    counts = csum[-1]
    tiles_e = (counts + tm - 1) // tm
    tile_end = jnp.cumsum(tiles_e)
    tile_start = tile_end - tiles_e
    n_used = tile_end[-1]
    n_tiles = n_pairs // tm + N_EXPERTS
    slot = tile_start[eflat] * tm + rank
    tid = jnp.arange(n_tiles, dtype=jnp.int32)
    te = jnp.minimum(jnp.searchsorted(tile_end, tid, side="right").astype(jnp.int32), N_EXPERTS - 1)
    te = jnp.where(tid < n_used, te, te[jnp.maximum(n_used - 1, 0)])
    rows = jnp.clip(counts[te] - (tid - tile_start[te]) * tm, 0, tm)
    rows = jnp.where(tid < n_used, rows, 0)
    rows8 = (rows + 7) // 8 * 8
    pair = jnp.arange(n_pairs, dtype=jnp.int32)
    src = jnp.zeros((n_tiles * tm,), jnp.int32).at[slot].set(pair % n)
    dst = (n_pairs + jnp.arange(n_tiles * tm, dtype=jnp.int32) % 8).at[slot].set(pair)
    return te, rows8.astype(jnp.int32), src, dst


def _final_kernel(h1_ref, y0_ref, y1_ref, route_ref, p_ref, gple_ref, wpg_ref, bpg_ref, wpp_ref, gfin_ref, y_ref):
    route = route_ref[...]
    h2 = h1_ref[...] + (route[:, 2:3] * y0_ref[...] + route[:, 3:4] * y1_ref[...])
    gate = jax.nn.sigmoid(_dot(_rms(h2, gple_ref[...]).astype(BF16), wpg_ref[...]) + bpg_ref[...])
    h3 = h2 + gate * _dot(p_ref[...].astype(BF16), wpp_ref[...])
    y_ref[...] = _rms(h3, gfin_ref[...])


def _final(h1, y_pairs, route, p2d, w, blk0, n_all, tm):
    n = h1.shape[0]
    nb_all = n_all // tm
    row = lambda width: pl.BlockSpec((tm, width), lambda i: (i, 0))
    return pl.pallas_call(
        _final_kernel,
        out_shape=jax.ShapeDtypeStruct((n, D_MODEL), F32),
        grid=(n // tm,),
        in_specs=[row(D_MODEL),
                  pl.BlockSpec((tm, D_MODEL), lambda i: (blk0 + i, 0)),
                  pl.BlockSpec((tm, D_MODEL), lambda i: (nb_all + blk0 + i, 0)),
                  row(ROUTE_LANES), row(PLE_DIM), _const_spec((1, D_MODEL)), _const_spec((D_MODEL, D_MODEL)),
                  _const_spec((1, D_MODEL)), _const_spec((PLE_DIM, D_MODEL)), _const_spec((1, D_MODEL))],
        out_specs=row(D_MODEL),
        compiler_params=_cparams(dimension_semantics=("arbitrary",)),
        name="final",
    )(h1, y_pairs, y_pairs, route, p2d, w["g_ple"], w["w_ple_gate"], w["b_ple_gate"], w["w_ple_proj"], w["g_final"])


def _rope_tables(pos):
    half = QK_ROPE // 2
    inv = ROPE_THETA ** (-jnp.arange(half, dtype=F32) / half)
    ang = pos.astype(F32)[:, None] * inv[None, :]
    pad = jnp.zeros((pos.shape[0], ROPE_PAD - QK_ROPE), F32)
    cos = jnp.concatenate([jnp.cos(ang), jnp.cos(ang), pad], axis=1)
    sin = jnp.concatenate([jnp.sin(ang), jnp.sin(ang), pad], axis=1)
    return cos, sin


def _rot_cols(wr):
    half = QK_ROPE // 2
    return jnp.concatenate([-wr[..., half:], wr[..., :half]], axis=-1)


def _pad_cols(wr):
    return jnp.concatenate([wr, jnp.zeros(wr.shape[:-1] + (ROPE_PAD - QK_ROPE,), wr.dtype)], axis=-1)


def _prep_weights(g_mix, w_in, g_q, w_uq, g_kv, w_ukv, w_conv, b_conv, w_rg, b_rg, w_ig, b_ig, lru_lambda,
                  g_out_mla, g_out_lru, w_o, g_ffn, w_group, b_group, w_router, b_router, g_ple, w_ple_gate,
                  b_ple_gate, w_ple_proj, g_final):
    row = lambda v: v.reshape(1, -1).astype(F32)
    s0, s1, s2, s3 = Q_LORA, Q_LORA + KV_LORA, Q_LORA + KV_LORA + QK_ROPE, Q_LORA + KV_LORA + QK_ROPE + LRU_WIDTH
    kr = w_in[:, s1:s2]
    w_in_ext = jnp.concatenate([w_in[:, :s1], _pad_cols(kr), _pad_cols(_rot_cols(kr)), w_in[:, s2:]], axis=1)
    uq = w_uq.reshape(Q_LORA, MLA_HEADS, QK_NOPE + QK_ROPE)
    uq_r = uq[..., QK_NOPE:]
    w_uq_ext = jnp.concatenate([uq[..., :QK_NOPE].reshape(Q_LORA, -1), _pad_cols(uq_r).reshape(Q_LORA, -1),
                                _pad_cols(_rot_cols(uq_r)).reshape(Q_LORA, -1)], axis=1)
    w_uk = jnp.transpose(w_ukv[..., :QK_NOPE], (1, 2, 0))
    w_uv = jnp.transpose(w_ukv[..., QK_NOPE:], (1, 0, 2))
    w_route = jnp.concatenate([w_group, w_router, jnp.zeros((D_MODEL, ROUTE_LANES - N_GROUPS - N_EXPERTS), F32)], axis=1)
    b_route = jnp.concatenate([b_group, b_router, jnp.zeros((ROUTE_LANES - N_GROUPS - N_EXPERTS,), F32)])
    return dict(
        g_mix=row(g_mix), w_in=w_in_ext.astype(BF16), g_q=row(g_q), w_uq=w_uq_ext.astype(BF16), g_kv=row(g_kv),
        w_uk=w_uk.astype(BF16), w_uv=w_uv.astype(BF16), w_conv=w_conv.astype(F32), b_conv=row(b_conv),
        w_rg=w_rg.astype(BF16), b_rg=row(b_rg), w_ig=w_ig.astype(BF16), b_ig=row(b_ig), lru_lambda=row(lru_lambda),
        g_out_mla=row(g_out_mla), g_out_lru=row(g_out_lru), w_o=w_o.astype(BF16), g_ffn=row(g_ffn),
        w_route=w_route.astype(BF16), b_route=row(b_route), g_ple=row(g_ple), w_ple_gate=w_ple_gate.astype(BF16),
        b_ple_gate=row(b_ple_gate), w_ple_proj=w_ple_proj.astype(BF16), g_final=row(g_final))


TM_PROJ = 256
TQ_ATTN = 256
TC_MIX = 256
TM_MOE = 256
PAGES_PER_STEP = 32


def kernel(x_prompt, x_sample, p_prompt, p_sample, cache_latent, cache_krope, state_lru, state_conv, page_table, g_mix, w_in, g_q, w_uq, g_kv, w_ukv, w_conv, b_conv, w_rg, b_rg, w_ig, b_ig, lru_lambda, g_out_mla, g_out_lru, w_o, g_ffn, w_group, b_group, w_router, b_router, w_gate, w_up, w_down, g_ple, w_ple_gate, b_ple_gate, w_ple_proj, g_final):
    assert w_in.shape[0] == 1, "single trunk layer"
    batch, seq, _ = x_prompt.shape
    db, t_new, _ = x_sample.shape
    page = cache_latent.shape[2]
    past_len = page_table.shape[1] * page
    n_p, n_s = batch * seq, db * t_new
    n_all = n_p + n_s

    w = _prep_weights(g_mix[0], w_in[0], g_q[0], w_uq[0], g_kv[0], w_ukv[0], w_conv[0], b_conv[0], w_rg[0], b_rg[0],
                      w_ig[0], b_ig[0], lru_lambda[0], g_out_mla[0], g_out_lru[0], w_o[0], g_ffn[0], w_group[0],
                      b_group[0], w_router[0], b_router[0], g_ple[0], w_ple_gate[0], b_ple_gate[0], w_ple_proj[0],
                      g_final)

    xp = x_prompt.reshape(n_p, D_MODEL)
    cos_p, sin_p = _rope_tables(jnp.arange(seq))
    qcat_p, kcat_p, lat_p, krope_p, xbr_p, ybr_p = _proj(xp, cos_p, sin_p, seq // TM_PROJ, w, TM_PROJ)
    omla_p = _prompt_attn(qcat_p, kcat_p, w["w_uv"], batch, seq, TQ_ATTN)
    h1_p, t_p, route_p, lru_p = _mix_prompt(xbr_p, ybr_p, omla_p, xp, w, batch, seq, TC_MIX)

    xs = x_sample.reshape(n_s, D_MODEL)
    tm_s = min(TM_PROJ, n_s)
    cos_s, sin_s = _rope_tables(past_len + (jnp.arange(tm_s) % t_new))
    qcat_s, kcat_s, lat_s, krope_s, xbr_s, ybr_s = _proj(xs, cos_s, sin_s, 1, w, tm_s)
    q_s = qcat_s.reshape(db, t_new, MLA_HEADS, QK_CAT).transpose(0, 2, 1, 3).reshape(db, MLA_HEADS * t_new, QK_CAT)
    knew = jnp.pad(kcat_s.reshape(db, t_new, QK_CAT), ((0, 0), (0, 8 - t_new), (0, 0)))
    omla_s = _sample_attn(page_table, q_s, knew, cache_latent[0], cache_krope[0], w["w_uv"], PAGES_PER_STEP)
    h1_s, t_s, route_s, lru_s = _mix_sample(
        xbr_s.reshape(db, t_new * LRU_WIDTH), ybr_s.reshape(db, t_new * LRU_WIDTH),
        omla_s.reshape(db, t_new * MLA_WIDTH), x_sample.reshape(db, t_new * D_MODEL),
        state_conv[0].reshape(db, (CONV_W - 1) * LRU_WIDTH), state_lru[0], w, t_new)
    h1_s = h1_s.reshape(n_s, D_MODEL)
    route_s = route_s.reshape(n_s, ROUTE_LANES)

    t_all = jnp.concatenate([t_p, t_s.reshape(n_s, D_MODEL)], axis=0)
    route_all = jnp.concatenate([route_p, route_s], axis=0)
    te, rows8, src, dst = _moe_plan(route_all, TM_MOE)
    y_pairs = _moe(t_all, te, rows8, src, dst, w_gate[0], w_up[0], w_down[0], 2 * n_all + 8, TM_MOE)

    y_p = _final(h1_p, y_pairs, route_p, p_prompt[0].reshape(n_p, PLE_DIM), w, 0, n_all, TM_PROJ)
    y_s = _final(h1_s, y_pairs, route_s, p_sample[0].reshape(n_s, PLE_DIM), w, n_p // TM_PROJ, n_all, TM_PROJ)

    new_conv_p = xbr_p.reshape(batch, seq, LRU_WIDTH)[:, seq - (CONV_W - 1):]
    hist = jnp.concatenate([state_conv[0], xbr_s.reshape(db, t_new, LRU_WIDTH)], axis=1)
    new_conv_s = hist[:, hist.shape[1] - (CONV_W - 1):]
    return (y_p.reshape(batch, seq, D_MODEL), y_s.reshape(db, t_new, D_MODEL),
            lat_p.reshape(1, batch, seq, KV_LORA), krope_p.reshape(1, batch, seq, QK_ROPE),
            lru_p.reshape(1, batch, LRU_WIDTH), new_conv_p[None],
            lat_s.reshape(1, db, t_new, KV_LORA), krope_s.reshape(1, db, t_new, QK_ROPE),
            lru_s[None], new_conv_s[None])
```

```python
import functools

import jax
import jax.numpy as jnp
from jax import lax
from jax.experimental import pallas as pl
from jax.experimental.pallas import tpu as pltpu

F32 = jnp.float32
BF16 = jnp.bfloat16

D_MODEL = 2048
MLA_HEADS = 8
V_HEAD = 128
MLA_WIDTH = MLA_HEADS * V_HEAD
LRU_WIDTH = D_MODEL - MLA_WIDTH
QK_NOPE = 128
QK_ROPE = 64
Q_LORA = 512
KV_LORA = 256
ROPE_THETA = 10000.0
SM_SCALE = (QK_NOPE + QK_ROPE) ** -0.5
NEG_INF = -1e30
LRU_BLOCKS = 8
LRU_BLOCK = LRU_WIDTH // LRU_BLOCKS
CONV_W = 4
LRU_C = 8.0
N_GROUPS = 4
EXPERTS_PER_GROUP = 8
N_EXPERTS = N_GROUPS * EXPERTS_PER_GROUP
D_EXPERT = 512
PLE_DIM = 256
EPS = 1e-6

LANES = 128
ROPE_PAD = LANES
QK_CAT = KV_LORA + ROPE_PAD
C_Q0, C_KV0, C_KRA0, C_KRB0, C_X0, C_Y0, IN_EXT = 0, 512, 768, 896, 1024, 2048, 3072
ROUTE_LANES = LANES
VMEM_LIMIT = 56 * 1024 * 1024


def _cparams(**kw):
    return pltpu.CompilerParams(vmem_limit_bytes=VMEM_LIMIT, **kw)


def _rms(x, g):
    return x * lax.rsqrt(jnp.mean(x * x, axis=-1, keepdims=True) + EPS) * g


def _dot(a, b):
    return jnp.dot(a, b, preferred_element_type=F32)


def _dot_nt(a, b):
    return lax.dot_general(a, b, (((1,), (1,)), ((), ())), preferred_element_type=F32)


def _const_spec(shape):
    nd = len(shape)
    return pl.BlockSpec(shape, lambda *_: (0,) * nd, pipeline_mode=pl.Buffered(1))


def _proj_kernel(x_ref, cos_ref, sin_ref, gmix_ref, win_ref, gq_ref, wuq_ref, gkv_ref, wuk_ref,
                 qcat_ref, kcat_ref, lat_ref, krope_ref, xbr_ref, ybr_ref):
    u = _rms(x_ref[...], gmix_ref[...]).astype(BF16)
    z = _dot(u, win_ref[...])
    xbr_ref[...] = z[:, C_X0:C_Y0]
    ybr_ref[...] = z[:, C_Y0:IN_EXT]
    cos = cos_ref[...]
    sin = sin_ref[...]
    lat = _rms(z[:, C_KV0:C_KRA0], gkv_ref[...])
    kr = z[:, C_KRA0:C_KRB0] * cos + z[:, C_KRB0:C_X0] * sin
    lat_ref[...] = lat
    krope_ref[...] = kr[:, :QK_ROPE]
    kcat_ref[:, 0:KV_LORA] = lat.astype(BF16)
    kcat_ref[:, KV_LORA:QK_CAT] = kr.astype(BF16)
    qn = _rms(z[:, C_Q0:C_KV0], gq_ref[...]).astype(BF16)
    q = _dot(qn, wuq_ref[...])
    ra0 = MLA_HEADS * QK_NOPE
    rb0 = ra0 + MLA_HEADS * ROPE_PAD
    for h in range(MLA_HEADS):
        ql = _dot(q[:, h * QK_NOPE:(h + 1) * QK_NOPE].astype(BF16), wuk_ref[h])
        qr = (q[:, ra0 + h * ROPE_PAD:ra0 + (h + 1) * ROPE_PAD] * cos
              + q[:, rb0 + h * ROPE_PAD:rb0 + (h + 1) * ROPE_PAD] * sin)
        qcat_ref[:, h * QK_CAT:h * QK_CAT + KV_LORA] = ql.astype(BF16)
        qcat_ref[:, h * QK_CAT + KV_LORA:(h + 1) * QK_CAT] = qr.astype(BF16)


def _proj(x2d, cos, sin, pos_blocks, w, tm):
    n = x2d.shape[0]
    row = lambda width: pl.BlockSpec((tm, width), lambda i: (i, 0))
    pos = pl.BlockSpec((tm, ROPE_PAD), lambda i: (i % pos_blocks, 0))
    out_shape = (
        jax.ShapeDtypeStruct((n, MLA_HEADS * QK_CAT), BF16),
        jax.ShapeDtypeStruct((n, QK_CAT), BF16),
        jax.ShapeDtypeStruct((n, KV_LORA), F32),
        jax.ShapeDtypeStruct((n, QK_ROPE), F32),
        jax.ShapeDtypeStruct((n, LRU_WIDTH), F32),
        jax.ShapeDtypeStruct((n, LRU_WIDTH), F32),
    )
    return pl.pallas_call(
        _proj_kernel,
        out_shape=out_shape,
        grid=(n // tm,),
        in_specs=[row(D_MODEL), pos, pos, _const_spec((1, D_MODEL)), _const_spec((D_MODEL, IN_EXT)),
                  _const_spec((1, Q_LORA)), _const_spec((Q_LORA, 3 * MLA_HEADS * LANES)),
                  _const_spec((1, KV_LORA)), _const_spec((MLA_HEADS, QK_NOPE, KV_LORA))],
        out_specs=(row(MLA_HEADS * QK_CAT), row(QK_CAT), row(KV_LORA), row(QK_ROPE), row(LRU_WIDTH), row(LRU_WIDTH)),
        compiler_params=_cparams(dimension_semantics=("arbitrary",)),
        name="proj",
    )(x2d, cos, sin, w["g_mix"], w["w_in"], w["g_q"], w["w_uq"], w["g_kv"], w["w_uk"])


def _prompt_attn_kernel(q_ref, k_ref, wuv_ref, o_ref, m_ref, l_ref, acc_ref, *, tq):
    qi = pl.program_id(1)
    q = jnp.concatenate([q_ref[:, h * QK_CAT:(h + 1) * QK_CAT] for h in range(MLA_HEADS)], axis=0)
    m_ref[...] = jnp.full(m_ref.shape, NEG_INF, F32)
    l_ref[...] = jnp.zeros(l_ref.shape, F32)
    acc_ref[...] = jnp.zeros(acc_ref.shape, F32)

    def step(kb, masked):
        k = k_ref[pl.ds(pl.multiple_of(kb * tq, tq), tq), :]
        s = _dot_nt(q, k) * SM_SCALE
        if masked:
            row = lax.broadcasted_iota(jnp.int32, s.shape, 0) & (tq - 1)
            col = lax.broadcasted_iota(jnp.int32, s.shape, 1)
            s = jnp.where(col <= row, s, NEG_INF)
        m_prev = m_ref[...]
        m_new = jnp.maximum(m_prev, jnp.max(s, axis=1, keepdims=True))
        alpha = jnp.exp(m_prev - m_new)
        p = jnp.exp(s - jnp.concatenate([m_new] * (tq // LANES), axis=1))
        l_ref[...] = alpha * l_ref[...] + jnp.sum(p, axis=1, keepdims=True)
        acc_ref[...] = (jnp.concatenate([alpha] * (KV_LORA // LANES), axis=1) * acc_ref[...]
                        + _dot(p.astype(BF16), k[:, :KV_LORA]))
        m_ref[...] = m_new

    def body(kb, carry):
        step(kb, False)
        return carry

    lax.fori_loop(0, qi, body, 0)
    step(qi, True)
    o = acc_ref[...] / jnp.concatenate([l_ref[...]] * (KV_LORA // LANES), axis=1)
    for h in range(MLA_HEADS):
        o_ref[:, h * V_HEAD:(h + 1) * V_HEAD] = _dot(o[h * tq:(h + 1) * tq].astype(BF16), wuv_ref[h])


def _prompt_attn(qcat, kcat, wuv, batch, seq, tq):
    nq = seq // tq
    m = MLA_HEADS * tq
    return pl.pallas_call(
        functools.partial(_prompt_attn_kernel, tq=tq),
        out_shape=jax.ShapeDtypeStruct((batch * seq, MLA_WIDTH), F32),
        grid=(batch, nq),
        in_specs=[pl.BlockSpec((tq, MLA_HEADS * QK_CAT), lambda b, i: (b * nq + i, 0)),
                  pl.BlockSpec((seq, QK_CAT), lambda b, i: (b, 0)),
                  _const_spec((MLA_HEADS, KV_LORA, V_HEAD))],
        out_specs=pl.BlockSpec((tq, MLA_WIDTH), lambda b, i: (b * nq + i, 0)),
        scratch_shapes=[pltpu.VMEM((m, LANES), F32), pltpu.VMEM((m, LANES), F32), pltpu.VMEM((m, KV_LORA), F32)],
        compiler_params=_cparams(dimension_semantics=("arbitrary", "arbitrary")),
        name="prompt_attn",
    )(qcat, kcat, wuv)


def _sample_attn_kernel(pt_ref, q_ref, knew_ref, lat_hbm, ropet_hbm, wuv_ref, o_ref,
                        lat_a, lat_b, rope_a, rope_b, sem, m_ref, l_ref, acc_ref, *, pages, page, n_pairs, t_new, n_sub):
    c = pl.program_id(1)
    step = pl.program_id(0) * n_pairs + c
    n_steps = pl.num_programs(0) * n_pairs
    bufs = ((lat_a, rope_a), (lat_b, rope_b))

    def start_chunk(chunk, k):
        for g in range(pages):
            pid = pt_ref[chunk * pages + g]
            pltpu.make_async_copy(lat_hbm.at[pid], bufs[k][0].at[g], sem.at[0, k]).start()
            pltpu.make_async_copy(ropet_hbm.at[pid], bufs[k][1].at[g], sem.at[1, k]).start()

    def wait_chunk(k):
        pltpu.make_async_copy(lat_hbm.at[pl.ds(0, pages)], bufs[k][0], sem.at[0, k]).wait()
        pltpu.make_async_copy(ropet_hbm.at[pl.ds(0, pages)], bufs[k][1], sem.at[1, k]).wait()

    q = q_ref[0]
    q_lat = q[:, :KV_LORA]
    q_rope = q[:, KV_LORA:KV_LORA + QK_ROPE]
    ps = pages // n_sub

    def attend(k):
        latbuf, ropebuf = bufs[k]
        parts = []
        for sb in range(n_sub):
            lat = latbuf[sb * ps:(sb + 1) * ps].reshape(ps * page, KV_LORA).astype(BF16)
            s_rope = jnp.concatenate([_dot(q_rope, ropebuf[sb * ps + g].astype(BF16)) for g in range(ps)], axis=1)
            s = (_dot_nt(q_lat, lat) + s_rope) * SM_SCALE
            m_i = jnp.max(s, axis=1, keepdims=True)
            p = jnp.exp(s - m_i)
            parts.append((m_i, jnp.sum(p, axis=1, keepdims=True), _dot(p.astype(BF16), lat)))
        m_prev = m_ref[...]
        m_new = m_prev
        for m_i, _, _ in parts:
            m_new = jnp.maximum(m_new, m_i)
        alpha = jnp.exp(m_prev - m_new)
        l = alpha * l_ref[...]
        acc = alpha * acc_ref[...]
        for m_i, l_i, o_i in parts:
            w_i = jnp.exp(m_i - m_new)
            l = l + w_i * l_i
            acc = acc + w_i * o_i
        l_ref[...] = l
        acc_ref[...] = acc
        m_ref[...] = m_new

    @pl.when(step == 0)
    def _():
        start_chunk(0, 0)

    @pl.when(c == 0)
    def _():
        m_ref[...] = jnp.full(m_ref.shape, NEG_INF, F32)
        l_ref[...] = jnp.zeros(l_ref.shape, F32)
        acc_ref[...] = jnp.zeros(acc_ref.shape, F32)

    wait_chunk(0)
    start_chunk(2 * step + 1, 1)
    attend(0)
    wait_chunk(1)
    start_chunk(jnp.minimum(2 * step + 2, 2 * n_steps - 1), 0)
    attend(1)

    @pl.when(step == n_steps - 1)
    def _():
        wait_chunk(0)

    @pl.when(c == n_pairs - 1)
    def _():
        qf = q.astype(F32)
        kn = knew_ref[0].astype(F32)
        tok = lax.broadcasted_iota(jnp.int32, (q.shape[0], 1), 0) & (t_new - 1)
        cols = []
        for j in range(t_new):
            sj = jnp.sum(qf * kn[j:j + 1, :], axis=1, keepdims=True) * SM_SCALE
            cols.append(jnp.where(tok >= j, sj, NEG_INF))
        m_prev = m_ref[...]
        m_new = m_prev
        for sj in cols:
            m_new = jnp.maximum(m_new, sj)
        alpha = jnp.exp(m_prev - m_new)
        l = alpha * l_ref[...]
        acc = alpha * acc_ref[...]
        for j, sj in enumerate(cols):
            pj = jnp.exp(sj - m_new)
            l = l + pj
            acc = acc + pj * kn[j:j + 1, :KV_LORA]
        o = (acc / l).astype(BF16)
        for h in range(MLA_HEADS):
            oh = _dot(o, wuv_ref[h])
            o_ref[0, :, h * V_HEAD:(h + 1) * V_HEAD] = oh[h * t_new:(h + 1) * t_new]


def _sample_attn(page_table, q_s, knew, cache_lat, cache_ropet, wuv, pages):
    db, n_pages = page_table.shape
    page = cache_lat.shape[1]
    n_pairs = n_pages // (2 * pages)
    rows = q_s.shape[1]
    t_new = rows // MLA_HEADS
    in_specs = [pl.BlockSpec((1, rows, QK_CAT), lambda b, c, pt: (b, 0, 0)),
                pl.BlockSpec((1, 8, QK_CAT), lambda b, c, pt: (b, 0, 0)),
                pl.BlockSpec(memory_space=pl.ANY), pl.BlockSpec(memory_space=pl.ANY),
                pl.BlockSpec((MLA_HEADS, KV_LORA, V_HEAD), lambda b, c, pt: (0, 0, 0))]
    grid_spec = pltpu.PrefetchScalarGridSpec(
        num_scalar_prefetch=1, grid=(db, n_pairs), in_specs=in_specs,
        out_specs=pl.BlockSpec((1, t_new, MLA_WIDTH), lambda b, c, pt: (b, 0, 0)),
        scratch_shapes=[pltpu.VMEM((pages, page, KV_LORA), F32), pltpu.VMEM((pages, page, KV_LORA), F32),
                        pltpu.VMEM((pages, QK_ROPE, page), F32), pltpu.VMEM((pages, QK_ROPE, page), F32),
                        pltpu.SemaphoreType.DMA((2, 2)), pltpu.VMEM((rows, 1), F32),
                        pltpu.VMEM((rows, 1), F32), pltpu.VMEM((rows, KV_LORA), F32)])
    return pl.pallas_call(
        functools.partial(_sample_attn_kernel, pages=pages, page=page, n_pairs=n_pairs, t_new=t_new,
                          n_sub=SAMPLE_SUB_BLOCKS),
        out_shape=jax.ShapeDtypeStruct((db, t_new, MLA_WIDTH), F32),
        grid_spec=grid_spec,
        compiler_params=_cparams(dimension_semantics=("arbitrary", "arbitrary")),
        name="sample_attn",
    )(page_table.reshape(-1), q_s, knew, cache_lat, cache_ropet, wuv)


def _gates(xc, wrg_ref, brg_ref, wig_ref, big_ref, lam_ref):
    xb = xc.astype(BF16)
    r = jnp.concatenate([_dot(xb[:, n * LRU_BLOCK:(n + 1) * LRU_BLOCK], wrg_ref[n]) for n in range(LRU_BLOCKS)], axis=1)
    i = jnp.concatenate([_dot(xb[:, n * LRU_BLOCK:(n + 1) * LRU_BLOCK], wig_ref[n]) for n in range(LRU_BLOCKS)], axis=1)
    r = jax.nn.sigmoid(r + brg_ref[...])
    i = jax.nn.sigmoid(i + big_ref[...])
    neg_lam = -lam_ref[...]
    softplus = jnp.maximum(neg_lam, 0.0) + jnp.log(1.0 + jnp.exp(-jnp.abs(neg_lam)))
    log_a = -LRU_C * r * softplus
    a = jnp.exp(log_a)
    u = jnp.sqrt(1.0 - jnp.exp(2.0 * log_a)) * i * xc
    return a, u


def _gelu(y):
    return 0.5 * y * (1.0 + jnp.tanh(0.7978845608028654 * (y + 0.044715 * (y * y * y))))


def _route(logits):
    lane = lax.broadcasted_iota(jnp.int32, logits.shape, 1)
    far = jnp.int32(4 * ROUTE_LANES)
    gmask = lane < N_GROUPS
    gl = jnp.where(gmask, logits, NEG_INF)
    gmax = jnp.max(gl, axis=1, keepdims=True)
    gidx = jnp.min(jnp.where(gl == gmax, lane, far), axis=1, keepdims=True)
    g_w = 1.0 / jnp.sum(jnp.where(gmask, jnp.exp(gl - gmax), 0.0), axis=1, keepdims=True)
    lo = N_GROUPS + gidx * EXPERTS_PER_GROUP
    emask = jnp.logical_and(lane >= lo, lane < lo + EXPERTS_PER_GROUP)
    el = jnp.where(emask, logits, NEG_INF)
    emax = jnp.max(el, axis=1, keepdims=True)
    ex = jnp.where(emask, jnp.exp(el - emax), 0.0)
    prob = jnp.where(emask, ex / jnp.sum(ex, axis=1, keepdims=True), -1.0)
    p1 = jnp.max(prob, axis=1, keepdims=True)
    i1 = jnp.min(jnp.where(prob == p1, lane, far), axis=1, keepdims=True)
    rest = jnp.where(lane == i1, -1.0, prob)
    p2 = jnp.max(rest, axis=1, keepdims=True)
    i2 = jnp.min(jnp.where(rest == p2, lane, far), axis=1, keepdims=True)
    den = p1 + p2
    w1 = g_w * p1 / den
    w2 = g_w * p2 / den
    e1 = (i1 - N_GROUPS).astype(F32)
    e2 = (i2 - N_GROUPS).astype(F32)
    out = jnp.where(lane == 0, e1, jnp.where(lane == 1, e2, jnp.where(lane == 2, w1, jnp.where(lane == 3, w2, 0.0))))
    return out


def _mix_tail(o_mla, o_lru, x, gmla_ref, glru_ref, wo_ref, gffn_ref, wr_ref, br_ref):
    mixed = jnp.concatenate([_rms(o_mla, gmla_ref[...]), _rms(o_lru, glru_ref[...])], axis=1).astype(BF16)
    h1 = x + _dot(mixed, wo_ref[...])
    t = _rms(h1, gffn_ref[...])
    logits = _dot(t.astype(BF16), wr_ref[...]) + br_ref[...]
    return h1, t, _route(logits)


def _mix_prompt_kernel(xbr_ref, ybr_ref, omla_ref, x_ref, wconv_ref, bconv_ref, wrg_ref, brg_ref, wig_ref, big_ref,
                       lam_ref, gmla_ref, glru_ref, wo_ref, gffn_ref, wr_ref, br_ref,
                       h1_ref, t_ref, route_ref, lru_ref, hc_ref, tail_ref, *, tc):
    @pl.when(pl.program_id(1) == 0)
    def _():
        hc_ref[...] = jnp.zeros(hc_ref.shape, F32)
        tail_ref[...] = jnp.zeros(tail_ref.shape, F32)

    xb = xbr_ref[...]
    tail = tail_ref[...]
    row8 = lax.broadcasted_iota(jnp.int32, tail.shape, 0)
    xc = bconv_ref[...] + xb * wconv_ref[CONV_W - 1:CONV_W, :]
    for j in range(1, CONV_W):
        sh = pltpu.roll(xb, j, axis=0)
        top = jnp.where(row8 < j, pltpu.roll(tail, j, axis=0), sh[0:8])
        sh = jnp.concatenate([top, sh[8:]], axis=0)
        xc = xc + sh * wconv_ref[CONV_W - 1 - j:CONV_W - j, :]
    tail_ref[...] = xb[tc - 8:tc]

    a, u = _gates(xc, wrg_ref, brg_ref, wig_ref, big_ref, lam_ref)
    row = lax.broadcasted_iota(jnp.int32, a.shape, 0)
    d = 1
    while d < tc:
        keep = row >= d
        u = jnp.where(keep, a * pltpu.roll(u, d, axis=0) + u, u)
        a = jnp.where(keep, a * pltpu.roll(a, d, axis=0), a)
        d *= 2
    hs = a * hc_ref[...] + u
    hc_ref[...] = hs[tc - 1:tc]
    lru_ref[...] = hs[tc - 1:tc]

    o_lru = _gelu(ybr_ref[...]) * hs
    h1, t, route = _mix_tail(omla_ref[...], o_lru, x_ref[...], gmla_ref, glru_ref, wo_ref, gffn_ref, wr_ref, br_ref)
    h1_ref[...] = h1
    t_ref[...] = t
    route_ref[...] = route


def _mix_weight_specs():
    return [_const_spec((CONV_W, LRU_WIDTH)), _const_spec((1, LRU_WIDTH)),
            _const_spec((LRU_BLOCKS, LRU_BLOCK, LRU_BLOCK)), _const_spec((1, LRU_WIDTH)),
            _const_spec((LRU_BLOCKS, LRU_BLOCK, LRU_BLOCK)), _const_spec((1, LRU_WIDTH)),
            _const_spec((1, LRU_WIDTH)), _const_spec((1, MLA_WIDTH)), _const_spec((1, LRU_WIDTH)),
            _const_spec((D_MODEL, D_MODEL)), _const_spec((1, D_MODEL)),
            _const_spec((D_MODEL, ROUTE_LANES)), _const_spec((1, ROUTE_LANES))]


def _mix_weights(w):
    return (w["w_conv"], w["b_conv"], w["w_rg"], w["b_rg"], w["w_ig"], w["b_ig"], w["lru_lambda"],
            w["g_out_mla"], w["g_out_lru"], w["w_o"], w["g_ffn"], w["w_route"], w["b_route"])


def _mix_prompt(xbr, ybr, omla, x2d, w, batch, seq, tc):
    nt = seq // tc
    n = batch * seq
    row = lambda width: pl.BlockSpec((tc, width), lambda b, i: (b * nt + i, 0))
    return pl.pallas_call(
        functools.partial(_mix_prompt_kernel, tc=tc),
        out_shape=(jax.ShapeDtypeStruct((n, D_MODEL), F32), jax.ShapeDtypeStruct((n, D_MODEL), F32),
                   jax.ShapeDtypeStruct((n, ROUTE_LANES), F32), jax.ShapeDtypeStruct((batch, 1, LRU_WIDTH), F32)),
        grid=(batch, nt),
        in_specs=[row(LRU_WIDTH), row(LRU_WIDTH), row(MLA_WIDTH), row(D_MODEL)] + _mix_weight_specs(),
        out_specs=(row(D_MODEL), row(D_MODEL), row(ROUTE_LANES),
                   pl.BlockSpec((None, 1, LRU_WIDTH), lambda b, i: (b, 0, 0))),
        scratch_shapes=[pltpu.VMEM((1, LRU_WIDTH), F32), pltpu.VMEM((8, LRU_WIDTH), F32)],
        compiler_params=_cparams(dimension_semantics=("arbitrary", "arbitrary")),
        name="mix_prompt",
    )(xbr, ybr, omla, x2d, *_mix_weights(w))


def _mix_sample_kernel(xbr_ref, ybr_ref, omla_ref, x_ref, sconv_ref, slru_ref, wconv_ref, bconv_ref, wrg_ref, brg_ref,
                       wig_ref, big_ref, lam_ref, gmla_ref, glru_ref, wo_ref, gffn_ref, wr_ref, br_ref,
                       h1_ref, t_ref, route_ref, lru_ref, *, t_new):
    w_ = LRU_WIDTH
    xs = [sconv_ref[:, k * w_:(k + 1) * w_] for k in range(CONV_W - 1)] + [xbr_ref[:, t * w_:(t + 1) * w_] for t in range(t_new)]
    xc = []
    for t in range(t_new):
        acc = bconv_ref[...] + xs[t] * wconv_ref[0:1, :]
        for k in range(1, CONV_W):
            acc = acc + xs[t + k] * wconv_ref[k:k + 1, :]
        xc.append(acc)
    xc = jnp.concatenate(xc, axis=0)
    a, u = _gates(xc, wrg_ref, brg_ref, wig_ref, big_ref, lam_ref)
    nb = xbr_ref.shape[0]
    h = slru_ref[...]
    hs = []
    for t in range(t_new):
        h = a[t * nb:(t + 1) * nb] * h + u[t * nb:(t + 1) * nb]
        hs.append(h)
    lru_ref[...] = h
    hs = jnp.concatenate(hs, axis=0)
    stack = lambda ref, width: jnp.concatenate([ref[:, t * width:(t + 1) * width] for t in range(t_new)], axis=0)
    o_lru = _gelu(stack(ybr_ref, w_)) * hs
    h1, tt, route = _mix_tail(stack(omla_ref, MLA_WIDTH), o_lru, stack(x_ref, D_MODEL),
                              gmla_ref, glru_ref, wo_ref, gffn_ref, wr_ref, br_ref)
    for t in range(t_new):
        h1_ref[:, t * D_MODEL:(t + 1) * D_MODEL] = h1[t * nb:(t + 1) * nb]
        t_ref[:, t * D_MODEL:(t + 1) * D_MODEL] = tt[t * nb:(t + 1) * nb]
        route_ref[:, t * ROUTE_LANES:(t + 1) * ROUTE_LANES] = route[t * nb:(t + 1) * nb]


def _mix_sample(xbr, ybr, omla, x2d, sconv, slru, w, t_new):
    nb = slru.shape[0]
    full = lambda a: pl.BlockSpec(a.shape, lambda i: (0, 0))
    ins = (xbr, ybr, omla, x2d, sconv, slru)
    return pl.pallas_call(
        functools.partial(_mix_sample_kernel, t_new=t_new),
        out_shape=(jax.ShapeDtypeStruct((nb, t_new * D_MODEL), F32), jax.ShapeDtypeStruct((nb, t_new * D_MODEL), F32),
                   jax.ShapeDtypeStruct((nb, t_new * ROUTE_LANES), F32), jax.ShapeDtypeStruct((nb, LRU_WIDTH), F32)),
        grid=(1,),
        in_specs=[full(a) for a in ins] + _mix_weight_specs(),
        out_specs=(pl.BlockSpec((nb, t_new * D_MODEL), lambda i: (0, 0)), pl.BlockSpec((nb, t_new * D_MODEL), lambda i: (0, 0)),
                   pl.BlockSpec((nb, t_new * ROUTE_LANES), lambda i: (0, 0)), pl.BlockSpec((nb, LRU_WIDTH), lambda i: (0, 0))),
        compiler_params=_cparams(dimension_semantics=("arbitrary",)),
        name="mix_sample",
    )(*ins, *_mix_weights(w))


MOE_SPARE_ROWS = 16
MOE_DMA_UNROLL = 8


def _moe_kernel(te_ref, nv_ref, pair_ref, t_hbm, wg_ref, wu_ref, wd_ref, y_hbm, xbuf, obuf, gsem, ssem, *, tm, n_tok):
    del te_ref
    j = pl.program_id(0)
    n_tiles = pl.num_programs(0)
    slot = j % 2

    def rows(jj):
        return pl.multiple_of(nv_ref[jj], MOE_DMA_UNROLL)

    def start_gather(jj, sl):
        def body(i, carry):
            for u in range(MOE_DMA_UNROLL):
                r = i * MOE_DMA_UNROLL + u
                pair = pair_ref[jj * tm + r]
                tok = jnp.where(pair >= 2 * n_tok, pair - 2 * n_tok, jnp.where(pair >= n_tok, pair - n_tok, pair))
                pltpu.make_async_copy(t_hbm.at[pl.ds(tok, 1)], xbuf.at[sl, pl.ds(r, 1)], gsem.at[sl]).start()
            return carry

        lax.fori_loop(0, rows(jj) // MOE_DMA_UNROLL, body, 0)

    def wait_gather(jj, sl):
        n = rows(jj)

        @pl.when(n > 0)
        def _():
            pltpu.make_async_copy(t_hbm.at[pl.ds(0, n)], xbuf.at[sl, pl.ds(0, n)], gsem.at[sl]).wait()

    def start_scatter(jj, sl):
        def body(i, carry):
            for u in range(MOE_DMA_UNROLL):
                r = i * MOE_DMA_UNROLL + u
                pltpu.make_async_copy(obuf.at[sl, pl.ds(r, 1)], y_hbm.at[pl.ds(pair_ref[jj * tm + r], 1)],
                                      ssem.at[sl]).start()
            return carry

        lax.fori_loop(0, rows(jj) // MOE_DMA_UNROLL, body, 0)

    def wait_scatter(jj, sl):
        n = rows(jj)

        @pl.when(n > 0)
        def _():
            pltpu.make_async_copy(obuf.at[sl, pl.ds(0, n)], y_hbm.at[pl.ds(0, n)], ssem.at[sl]).wait()

    @pl.when(j == 0)
    def _():
        xbuf[...] = jnp.zeros(xbuf.shape, F32)
        spare = y_hbm.shape[0] - MOE_SPARE_ROWS
        init = pltpu.make_async_copy(xbuf.at[0, pl.ds(0, MOE_SPARE_ROWS)], y_hbm.at[pl.ds(spare, MOE_SPARE_ROWS)],
                                     ssem.at[0])
        init.start()
        init.wait()
        start_gather(0, 0)

    @pl.when(j + 1 < n_tiles)
    def _():
        start_gather(j + 1, 1 - slot)

    wait_gather(j, slot)

    @pl.when(j >= 2)
    def _():
        wait_scatter(j - 2, slot)

    @pl.when(rows(j) > 0)
    def _():
        x = xbuf[slot].astype(BF16)
        hg = _dot(x, wg_ref[...].astype(BF16))
        hu = _dot(x, wu_ref[...].astype(BF16))
        act = (hg * jax.nn.sigmoid(hg) * hu).astype(BF16)
        obuf[slot] = _dot(act, wd_ref[...].astype(BF16))

    start_scatter(j, slot)

    @pl.when(j == n_tiles - 1)
    def _():
        wait_scatter(j - 1, 1 - slot)
        wait_scatter(j, slot)


def _moe(t_all, tile_expert, tile_rows, pair_of_slot, w_gate, w_up, w_down, tm):
    n_tiles = tile_expert.shape[0]
    n_tok = t_all.shape[0]
    assert n_tiles >= 2
    wspec = lambda a, b: pl.BlockSpec((None, a, b), lambda j, te, nv, pr: (te[j], 0, 0))
    grid_spec = pltpu.PrefetchScalarGridSpec(
        num_scalar_prefetch=3, grid=(n_tiles,),
        in_specs=[pl.BlockSpec(memory_space=pl.ANY), wspec(D_MODEL, D_EXPERT), wspec(D_MODEL, D_EXPERT),
                  wspec(D_EXPERT, D_MODEL)],
        out_specs=pl.BlockSpec(memory_space=pl.ANY),
        scratch_shapes=[pltpu.VMEM((2, tm, D_MODEL), F32), pltpu.VMEM((2, tm, D_MODEL), F32),
                        pltpu.SemaphoreType.DMA((2,)), pltpu.SemaphoreType.DMA((2,))])
    return pl.pallas_call(
        functools.partial(_moe_kernel, tm=tm, n_tok=n_tok),
        out_shape=jax.ShapeDtypeStruct((2 * n_tok + MOE_SPARE_ROWS, D_MODEL), F32),
        grid_spec=grid_spec,
        compiler_params=_cparams(dimension_semantics=("arbitrary",)),
        name="moe",
    )(tile_expert, tile_rows, pair_of_slot, t_all, w_gate, w_up, w_down)


def _moe_plan(route_all, tm):
    n = route_all.shape[0]
    n_pairs = 2 * n
    eflat = jnp.concatenate([route_all[:, 0], route_all[:, 1]]).astype(jnp.int32)
    onehot = (eflat[:, None] == jnp.arange(N_EXPERTS, dtype=jnp.int32)[None, :]).astype(jnp.int32)
    csum = jnp.cumsum(onehot, axis=0)
    rank = jnp.take_along_axis(csum, eflat[:, None], axis=1)[:, 0] - 1
    counts = csum[-1]
    tiles_e = (counts + tm - 1) // tm
    tile_end = jnp.cumsum(tiles_e)
    tile_start = tile_end - tiles_e
    n_used = tile_end[-1]
    n_tiles = n_pairs // tm + N_EXPERTS
    slot = tile_start[eflat] * tm + rank
    tid = jnp.arange(n_tiles, dtype=jnp.int32)
    te = jnp.minimum(jnp.sum((tile_end[None, :] <= tid[:, None]).astype(jnp.int32), axis=1), N_EXPERTS - 1)
    te = jnp.where(tid < n_used, te, te[jnp.maximum(n_used - 1, 0)])
    rows = jnp.clip(counts[te] - (tid - tile_start[te]) * tm, 0, tm)
    rows = jnp.where(tid < n_used, rows, 0)
    rows8 = (rows + MOE_DMA_UNROLL - 1) // MOE_DMA_UNROLL * MOE_DMA_UNROLL
    sid = jnp.arange(n_tiles * tm, dtype=jnp.int32)
    pad_pair = n_pairs + ((sid // tm) % 2) * 8 + sid % 8
    pair_of_slot = pad_pair.at[slot].set(jnp.arange(n_pairs, dtype=jnp.int32))
    return te, rows8.astype(jnp.int32), pair_of_slot


def _final_kernel(h1_ref, y0_ref, y1_ref, route_ref, p_ref, gple_ref, wpg_ref, bpg_ref, wpp_ref, gfin_ref, y_ref):
    route = route_ref[...]
    h2 = h1_ref[...] + (route[:, 2:3] * y0_ref[...] + route[:, 3:4] * y1_ref[...])
    gate = jax.nn.sigmoid(_dot(_rms(h2, gple_ref[...]).astype(BF16), wpg_ref[...]) + bpg_ref[...])
    h3 = h2 + gate * _dot(p_ref[...].astype(BF16), wpp_ref[...])
    y_ref[...] = _rms(h3, gfin_ref[...])


def _final(h1, y_pairs, route, p2d, w, blk0, n_all, tm):
    n = h1.shape[0]
    nb_all = n_all // tm
    row = lambda width: pl.BlockSpec((tm, width), lambda i: (i, 0))
    return pl.pallas_call(
        _final_kernel,
        out_shape=jax.ShapeDtypeStruct((n, D_MODEL), F32),
        grid=(n // tm,),
        in_specs=[row(D_MODEL),
                  pl.BlockSpec((tm, D_MODEL), lambda i: (blk0 + i, 0)),
                  pl.BlockSpec((tm, D_MODEL), lambda i: (nb_all + blk0 + i, 0)),
                  row(ROUTE_LANES), row(PLE_DIM), _const_spec((1, D_MODEL)), _const_spec((D_MODEL, D_MODEL)),
                  _const_spec((1, D_MODEL)), _const_spec((PLE_DIM, D_MODEL)), _const_spec((1, D_MODEL))],
        out_specs=row(D_MODEL),
        compiler_params=_cparams(dimension_semantics=("arbitrary",)),
        name="final",
    )(h1, y_pairs, y_pairs, route, p2d, w["g_ple"], w["w_ple_gate"], w["b_ple_gate"], w["w_ple_proj"], w["g_final"])


def _rope_tables(pos):
    half = QK_ROPE // 2
    inv = ROPE_THETA ** (-jnp.arange(half, dtype=F32) / half)
    ang = pos.astype(F32)[:, None] * inv[None, :]
    pad = jnp.zeros((pos.shape[0], ROPE_PAD - QK_ROPE), F32)
    cos = jnp.concatenate([jnp.cos(ang), jnp.cos(ang), pad], axis=1)
    sin = jnp.concatenate([jnp.sin(ang), jnp.sin(ang), pad], axis=1)
    return cos, sin


def _rot_cols(wr):
    half = QK_ROPE // 2
    return jnp.concatenate([-wr[..., half:], wr[..., :half]], axis=-1)


def _pad_cols(wr):
    return jnp.concatenate([wr, jnp.zeros(wr.shape[:-1] + (ROPE_PAD - QK_ROPE,), wr.dtype)], axis=-1)


def _prep_weights(g_mix, w_in, g_q, w_uq, g_kv, w_ukv, w_conv, b_conv, w_rg, b_rg, w_ig, b_ig, lru_lambda,
                  g_out_mla, g_out_lru, w_o, g_ffn, w_group, b_group, w_router, b_router, g_ple, w_ple_gate,
                  b_ple_gate, w_ple_proj, g_final):
    row = lambda v: v.reshape(1, -1).astype(F32)
    s0, s1, s2, s3 = Q_LORA, Q_LORA + KV_LORA, Q_LORA + KV_LORA + QK_ROPE, Q_LORA + KV_LORA + QK_ROPE + LRU_WIDTH
    kr = w_in[:, s1:s2]
    w_in_ext = jnp.concatenate([w_in[:, :s1], _pad_cols(kr), _pad_cols(_rot_cols(kr)), w_in[:, s2:]], axis=1)
    uq = w_uq.reshape(Q_LORA, MLA_HEADS, QK_NOPE + QK_ROPE)
    uq_r = uq[..., QK_NOPE:]
    w_uq_ext = jnp.concatenate([uq[..., :QK_NOPE].reshape(Q_LORA, -1), _pad_cols(uq_r).reshape(Q_LORA, -1),
                                _pad_cols(_rot_cols(uq_r)).reshape(Q_LORA, -1)], axis=1)
    w_uk = jnp.transpose(w_ukv[..., :QK_NOPE], (1, 2, 0))
    w_uv = jnp.transpose(w_ukv[..., QK_NOPE:], (1, 0, 2))
    w_route = jnp.concatenate([w_group, w_router, jnp.zeros((D_MODEL, ROUTE_LANES - N_GROUPS - N_EXPERTS), F32)], axis=1)
    b_route = jnp.concatenate([b_group, b_router, jnp.zeros((ROUTE_LANES - N_GROUPS - N_EXPERTS,), F32)])
    return dict(
        g_mix=row(g_mix), w_in=w_in_ext.astype(BF16), g_q=row(g_q), w_uq=w_uq_ext.astype(BF16), g_kv=row(g_kv),
        w_uk=w_uk.astype(BF16), w_uv=w_uv.astype(BF16), w_conv=w_conv.astype(F32), b_conv=row(b_conv),
        w_rg=w_rg.astype(BF16), b_rg=row(b_rg), w_ig=w_ig.astype(BF16), b_ig=row(b_ig), lru_lambda=row(lru_lambda),
        g_out_mla=row(g_out_mla), g_out_lru=row(g_out_lru), w_o=w_o.astype(BF16), g_ffn=row(g_ffn),
        w_route=w_route.astype(BF16), b_route=row(b_route), g_ple=row(g_ple), w_ple_gate=w_ple_gate.astype(BF16),
        b_ple_gate=row(b_ple_gate), w_ple_proj=w_ple_proj.astype(BF16), g_final=row(g_final))


TM_PROJ = 256
TQ_ATTN = 256
TC_MIX = 256
TM_MOE = 256
PAGES_PER_STEP = 32
SAMPLE_SUB_BLOCKS = 4


def kernel(x_prompt, x_sample, p_prompt, p_sample, cache_latent, cache_krope, state_lru, state_conv, page_table, g_mix, w_in, g_q, w_uq, g_kv, w_ukv, w_conv, b_conv, w_rg, b_rg, w_ig, b_ig, lru_lambda, g_out_mla, g_out_lru, w_o, g_ffn, w_group, b_group, w_router, b_router, w_gate, w_up, w_down, g_ple, w_ple_gate, b_ple_gate, w_ple_proj, g_final):
    assert w_in.shape[0] == 1, "single trunk layer"
    batch, seq, _ = x_prompt.shape
    db, t_new, _ = x_sample.shape
    page = cache_latent.shape[2]
    past_len = page_table.shape[1] * page
    n_p, n_s = batch * seq, db * t_new
    n_all = n_p + n_s

    w = _prep_weights(g_mix[0], w_in[0], g_q[0], w_uq[0], g_kv[0], w_ukv[0], w_conv[0], b_conv[0], w_rg[0], b_rg[0],
                      w_ig[0], b_ig[0], lru_lambda[0], g_out_mla[0], g_out_lru[0], w_o[0], g_ffn[0], w_group[0],
                      b_group[0], w_router[0], b_router[0], g_ple[0], w_ple_gate[0], b_ple_gate[0], w_ple_proj[0],
                      g_final)

    xp = x_prompt.reshape(n_p, D_MODEL)
    cos_p, sin_p = _rope_tables(jnp.arange(seq))
    qcat_p, kcat_p, lat_p, krope_p, xbr_p, ybr_p = _proj(xp, cos_p, sin_p, seq // TM_PROJ, w, TM_PROJ)
    omla_p = _prompt_attn(qcat_p, kcat_p, w["w_uv"], batch, seq, TQ_ATTN)
    h1_p, t_p, route_p, lru_p = _mix_prompt(xbr_p, ybr_p, omla_p, xp, w, batch, seq, TC_MIX)

    xs = x_sample.reshape(n_s, D_MODEL)
    tm_s = min(TM_PROJ, n_s)
    cos_s, sin_s = _rope_tables(past_len + (jnp.arange(tm_s) % t_new))
    qcat_s, kcat_s, lat_s, krope_s, xbr_s, ybr_s = _proj(xs, cos_s, sin_s, 1, w, tm_s)
    q_s = qcat_s.reshape(db, t_new, MLA_HEADS, QK_CAT).transpose(0, 2, 1, 3).reshape(db, MLA_HEADS * t_new, QK_CAT)
    knew = jnp.pad(kcat_s.reshape(db, t_new, QK_CAT), ((0, 0), (0, 8 - t_new), (0, 0)))
    cache_ropet = jnp.swapaxes(cache_krope[0], 1, 2)
    omla_s = _sample_attn(page_table, q_s, knew, cache_latent[0], cache_ropet, w["w_uv"], PAGES_PER_STEP)
    h1_s, t_s, route_s, lru_s = _mix_sample(
        xbr_s.reshape(db, t_new * LRU_WIDTH), ybr_s.reshape(db, t_new * LRU_WIDTH),
        omla_s.reshape(db, t_new * MLA_WIDTH), x_sample.reshape(db, t_new * D_MODEL),
        state_conv[0].reshape(db, (CONV_W - 1) * LRU_WIDTH), state_lru[0], w, t_new)
    h1_s = h1_s.reshape(n_s, D_MODEL)
    route_s = route_s.reshape(n_s, ROUTE_LANES)

    t_all = jnp.concatenate([t_p, t_s.reshape(n_s, D_MODEL)], axis=0)
    route_all = jnp.concatenate([route_p, route_s], axis=0)
    te, rows8, pair_of_slot = _moe_plan(route_all, TM_MOE)
    y_pairs = _moe(t_all, te, rows8, pair_of_slot, w_gate[0], w_up[0], w_down[0], TM_MOE)

    y_p = _final(h1_p, y_pairs, route_p, p_prompt[0].reshape(n_p, PLE_DIM), w, 0, n_all, TM_PROJ)
    y_s = _final(h1_s, y_pairs, route_s, p_sample[0].reshape(n_s, PLE_DIM), w, n_p // TM_PROJ, n_all, TM_PROJ)

    new_conv_p = xbr_p.reshape(batch, seq, LRU_WIDTH)[:, seq - (CONV_W - 1):]
    hist = jnp.concatenate([state_conv[0], xbr_s.reshape(db, t_new, LRU_WIDTH)], axis=1)
    new_conv_s = hist[:, hist.shape[1] - (CONV_W - 1):]
    return (y_p.reshape(batch, seq, D_MODEL), y_s.reshape(db, t_new, D_MODEL),
            lat_p.reshape(1, batch, seq, KV_LORA), krope_p.reshape(1, batch, seq, QK_ROPE),
            lru_p.reshape(1, batch, LRU_WIDTH), new_conv_p[None],
            lat_s.reshape(1, db, t_new, KV_LORA), krope_s.reshape(1, db, t_new, QK_ROPE),
            lru_s[None], new_conv_s[None])
```

```python
import functools

import jax
import jax.numpy as jnp
from jax import lax
from jax.experimental import pallas as pl
from jax.experimental.pallas import tpu as pltpu

F32 = jnp.float32
BF16 = jnp.bfloat16

D_MODEL = 2048
MLA_HEADS = 8
V_HEAD = 128
MLA_WIDTH = MLA_HEADS * V_HEAD
LRU_WIDTH = D_MODEL - MLA_WIDTH
QK_NOPE = 128
QK_ROPE = 64
Q_LORA = 512
KV_LORA = 256
ROPE_THETA = 10000.0
SM_SCALE = (QK_NOPE + QK_ROPE) ** -0.5
NEG_INF = -1e30
LRU_BLOCKS = 8
LRU_BLOCK = LRU_WIDTH // LRU_BLOCKS
CONV_W = 4
LRU_C = 8.0
N_GROUPS = 4
EXPERTS_PER_GROUP = 8
N_EXPERTS = N_GROUPS * EXPERTS_PER_GROUP
D_EXPERT = 512
PLE_DIM = 256
EPS = 1e-6

LANES = 128
ROPE_PAD = LANES
QK_CAT = KV_LORA + ROPE_PAD
C_Q0, C_KV0, C_KRA0, C_KRB0, C_X0, C_Y0, IN_EXT = 0, 512, 768, 896, 1024, 2048, 3072
ROUTE_LANES = LANES
VMEM_LIMIT = 56 * 1024 * 1024


def _cparams(**kw):
    return pltpu.CompilerParams(vmem_limit_bytes=VMEM_LIMIT, **kw)


def _rms(x, g):
    return x * lax.rsqrt(jnp.mean(x * x, axis=-1, keepdims=True) + EPS) * g


def _dot(a, b):
    return jnp.dot(a, b, preferred_element_type=F32)


def _dot_nt(a, b):
    return lax.dot_general(a, b, (((1,), (1,)), ((), ())), preferred_element_type=F32)


def _const_spec(shape):
    nd = len(shape)
    return pl.BlockSpec(shape, lambda *_: (0,) * nd, pipeline_mode=pl.Buffered(1))


def _proj_kernel(x_ref, cos_ref, sin_ref, gmix_ref, win_ref, gq_ref, wuq_ref, gkv_ref, wuk_ref,
                 qcat_ref, kcat_ref, lat_ref, krope_ref, xbr_ref, ybr_ref):
    u = _rms(x_ref[...], gmix_ref[...]).astype(BF16)
    z = _dot(u, win_ref[...])
    xbr_ref[...] = z[:, C_X0:C_Y0]
    ybr_ref[...] = z[:, C_Y0:IN_EXT]
    cos = cos_ref[...]
    sin = sin_ref[...]
    lat = _rms(z[:, C_KV0:C_KRA0], gkv_ref[...])
    kr = z[:, C_KRA0:C_KRB0] * cos + z[:, C_KRB0:C_X0] * sin
    lat_ref[...] = lat
    krope_ref[...] = kr[:, :QK_ROPE]
    kcat_ref[:, 0:KV_LORA] = lat.astype(BF16)
    kcat_ref[:, KV_LORA:QK_CAT] = kr.astype(BF16)
    qn = _rms(z[:, C_Q0:C_KV0], gq_ref[...]).astype(BF16)
    q = _dot(qn, wuq_ref[...])
    ra0 = MLA_HEADS * QK_NOPE
    rb0 = ra0 + MLA_HEADS * ROPE_PAD
    for h in range(MLA_HEADS):
        ql = _dot(q[:, h * QK_NOPE:(h + 1) * QK_NOPE].astype(BF16), wuk_ref[h])
        qr = (q[:, ra0 + h * ROPE_PAD:ra0 + (h + 1) * ROPE_PAD] * cos
              + q[:, rb0 + h * ROPE_PAD:rb0 + (h + 1) * ROPE_PAD] * sin)
        qcat_ref[:, h * QK_CAT:h * QK_CAT + KV_LORA] = ql.astype(BF16)
        qcat_ref[:, h * QK_CAT + KV_LORA:(h + 1) * QK_CAT] = qr.astype(BF16)


def _proj(x2d, cos, sin, pos_blocks, w, tm):
    n = x2d.shape[0]
    row = lambda width: pl.BlockSpec((tm, width), lambda i: (i, 0))
    pos = pl.BlockSpec((tm, ROPE_PAD), lambda i: (i % pos_blocks, 0))
    out_shape = (
        jax.ShapeDtypeStruct((n, MLA_HEADS * QK_CAT), BF16),
        jax.ShapeDtypeStruct((n, QK_CAT), BF16),
        jax.ShapeDtypeStruct((n, KV_LORA), F32),
        jax.ShapeDtypeStruct((n, QK_ROPE), F32),
        jax.ShapeDtypeStruct((n, LRU_WIDTH), F32),
        jax.ShapeDtypeStruct((n, LRU_WIDTH), F32),
    )
    return pl.pallas_call(
        _proj_kernel,
        out_shape=out_shape,
        grid=(n // tm,),
        in_specs=[row(D_MODEL), pos, pos, _const_spec((1, D_MODEL)), _const_spec((D_MODEL, IN_EXT)),
                  _const_spec((1, Q_LORA)), _const_spec((Q_LORA, 3 * MLA_HEADS * LANES)),
                  _const_spec((1, KV_LORA)), _const_spec((MLA_HEADS, QK_NOPE, KV_LORA))],
        out_specs=(row(MLA_HEADS * QK_CAT), row(QK_CAT), row(KV_LORA), row(QK_ROPE), row(LRU_WIDTH), row(LRU_WIDTH)),
        compiler_params=_cparams(dimension_semantics=("arbitrary",)),
        name="proj",
    )(x2d, cos, sin, w["g_mix"], w["w_in"], w["g_q"], w["w_uq"], w["g_kv"], w["w_uk"])


def _prompt_attn_kernel(q_ref, k_ref, wuv_ref, o_ref, m_ref, l_ref, acc_ref, *, tq):
    qi = pl.program_id(1)
    q = jnp.concatenate([q_ref[:, h * QK_CAT:(h + 1) * QK_CAT] for h in range(MLA_HEADS)], axis=0)
    m_ref[...] = jnp.full(m_ref.shape, NEG_INF, F32)
    l_ref[...] = jnp.zeros(l_ref.shape, F32)
    acc_ref[...] = jnp.zeros(acc_ref.shape, F32)

    def step(kb, masked):
        k = k_ref[pl.ds(pl.multiple_of(kb * tq, tq), tq), :]
        s = _dot_nt(q, k) * SM_SCALE
        if masked:
            row = lax.broadcasted_iota(jnp.int32, s.shape, 0) & (tq - 1)
            col = lax.broadcasted_iota(jnp.int32, s.shape, 1)
            s = jnp.where(col <= row, s, NEG_INF)
        m_prev = m_ref[...]
        m_new = jnp.maximum(m_prev, jnp.max(s, axis=1, keepdims=True))
        alpha = jnp.exp(m_prev - m_new)
        p = jnp.exp(s - jnp.concatenate([m_new] * (tq // LANES), axis=1))
        l_ref[...] = alpha * l_ref[...] + jnp.sum(p, axis=1, keepdims=True)
        acc_ref[...] = (jnp.concatenate([alpha] * (KV_LORA // LANES), axis=1) * acc_ref[...]
                        + _dot(p.astype(BF16), k[:, :KV_LORA]))
        m_ref[...] = m_new

    def body(kb, carry):
        step(kb, False)
        return carry

    lax.fori_loop(0, qi, body, 0)
    step(qi, True)
    o = acc_ref[...] / jnp.concatenate([l_ref[...]] * (KV_LORA // LANES), axis=1)
    for h in range(MLA_HEADS):
        o_ref[:, h * V_HEAD:(h + 1) * V_HEAD] = _dot(o[h * tq:(h + 1) * tq].astype(BF16), wuv_ref[h])


def _prompt_attn(qcat, kcat, wuv, batch, seq, tq):
    nq = seq // tq
    m = MLA_HEADS * tq
    return pl.pallas_call(
        functools.partial(_prompt_attn_kernel, tq=tq),
        out_shape=jax.ShapeDtypeStruct((batch * seq, MLA_WIDTH), F32),
        grid=(batch, nq),
        in_specs=[pl.BlockSpec((tq, MLA_HEADS * QK_CAT), lambda b, i: (b * nq + i, 0)),
                  pl.BlockSpec((seq, QK_CAT), lambda b, i: (b, 0)),
                  _const_spec((MLA_HEADS, KV_LORA, V_HEAD))],
        out_specs=pl.BlockSpec((tq, MLA_WIDTH), lambda b, i: (b * nq + i, 0)),
        scratch_shapes=[pltpu.VMEM((m, LANES), F32), pltpu.VMEM((m, LANES), F32), pltpu.VMEM((m, KV_LORA), F32)],
        compiler_params=_cparams(dimension_semantics=("arbitrary", "arbitrary")),
        name="prompt_attn",
    )(qcat, kcat, wuv)


def _sample_attn_kernel(pt_ref, q_ref, knew_ref, lat_hbm, ropet_hbm, wuv_ref, o_ref, *scratch,
                        pages, page, n_groups, n_buf, t_new, n_sub):
    lat_bufs, rope_bufs = scratch[:n_buf], scratch[n_buf:2 * n_buf]
    sem, m_ref, l_ref, acc_ref = scratch[2 * n_buf:]
    c = pl.program_id(1)
    step = pl.program_id(0) * n_groups + c
    n_steps = pl.num_programs(0) * n_groups
    bufs = tuple(zip(lat_bufs, rope_bufs))

    def start_chunk(chunk, k):
        for g in range(pages):
            pid = pt_ref[chunk * pages + g]
            pltpu.make_async_copy(lat_hbm.at[pid], bufs[k][0].at[g], sem.at[0, k]).start()
            pltpu.make_async_copy(ropet_hbm.at[pid], bufs[k][1].at[g], sem.at[1, k]).start()

    def wait_chunk(k):
        pltpu.make_async_copy(lat_hbm.at[pl.ds(0, pages)], bufs[k][0], sem.at[0, k]).wait()
        pltpu.make_async_copy(ropet_hbm.at[pl.ds(0, pages)], bufs[k][1], sem.at[1, k]).wait()

    q = q_ref[0]
    q_lat = q[:, :KV_LORA]
    q_rope = q[:, KV_LORA:KV_LORA + QK_ROPE]
    ps = pages // n_sub

    def attend(k):
        latbuf, ropebuf = bufs[k]
        lats, scores, parts = [], [], []
        for sb in range(n_sub):
            lat = latbuf[sb * ps:(sb + 1) * ps].reshape(ps * page, KV_LORA).astype(BF16)
            s_rope = jnp.concatenate([_dot(q_rope, ropebuf[sb * ps + g].astype(BF16)) for g in range(ps)], axis=1)
            lats.append(lat)
            scores.append((_dot_nt(q_lat, lat) + s_rope) * SM_SCALE)
        for sb in range(n_sub):
            m_i = jnp.max(scores[sb], axis=1, keepdims=True)
            p = jnp.exp(scores[sb] - m_i)
            parts.append((m_i, jnp.sum(p, axis=1, keepdims=True), p.astype(BF16)))
        parts = [(m_i, l_i, _dot(p, lats[sb])) for sb, (m_i, l_i, p) in enumerate(parts)]
        m_prev = m_ref[...]
        m_new = m_prev
        for m_i, _, _ in parts:
            m_new = jnp.maximum(m_new, m_i)
        alpha = jnp.exp(m_prev - m_new)
        l = alpha * l_ref[...]
        acc = alpha * acc_ref[...]
        for m_i, l_i, o_i in parts:
            w_i = jnp.exp(m_i - m_new)
            l = l + w_i * l_i
            acc = acc + w_i * o_i
        l_ref[...] = l
        acc_ref[...] = acc
        m_ref[...] = m_new

    @pl.when(step == 0)
    def _():
        for k in range(n_buf):
            start_chunk(k, k)

    @pl.when(c == 0)
    def _():
        m_ref[...] = jnp.full(m_ref.shape, NEG_INF, F32)
        l_ref[...] = jnp.zeros(l_ref.shape, F32)
        acc_ref[...] = jnp.zeros(acc_ref.shape, F32)

    last_chunk = n_buf * n_steps - 1
    for k in range(n_buf):
        wait_chunk(k)
        attend(k)
        start_chunk(jnp.minimum(n_buf * (step + 1) + k, last_chunk), k)

    @pl.when(step == n_steps - 1)
    def _():
        for k in range(n_buf):
            wait_chunk(k)

    @pl.when(c == n_groups - 1)
    def _():
        qf = q.astype(F32)
        kn = knew_ref[0].astype(F32)
        tok = lax.broadcasted_iota(jnp.int32, (q.shape[0], 1), 0) & (t_new - 1)
        cols = []
        for j in range(t_new):
            sj = jnp.sum(qf * kn[j:j + 1, :], axis=1, keepdims=True) * SM_SCALE
            cols.append(jnp.where(tok >= j, sj, NEG_INF))
        m_prev = m_ref[...]
        m_new = m_prev
        for sj in cols:
            m_new = jnp.maximum(m_new, sj)
        alpha = jnp.exp(m_prev - m_new)
        l = alpha * l_ref[...]
        acc = alpha * acc_ref[...]
        for j, sj in enumerate(cols):
            pj = jnp.exp(sj - m_new)
            l = l + pj
            acc = acc + pj * kn[j:j + 1, :KV_LORA]
        o = (acc / l).astype(BF16)
        for h in range(MLA_HEADS):
            oh = _dot(o, wuv_ref[h])
            o_ref[0, :, h * V_HEAD:(h + 1) * V_HEAD] = oh[h * t_new:(h + 1) * t_new]


def _sample_attn(page_table, q_s, knew, cache_lat, cache_ropet, wuv, pages):
    db, n_pages = page_table.shape
    page = cache_lat.shape[1]
    n_buf = SAMPLE_BUFFERS
    n_groups = n_pages // (n_buf * pages)
    assert n_groups * n_buf * pages == n_pages
    rows = q_s.shape[1]
    t_new = rows // MLA_HEADS
    in_specs = [pl.BlockSpec((1, rows, QK_CAT), lambda b, c, pt: (b, 0, 0)),
                pl.BlockSpec((1, 8, QK_CAT), lambda b, c, pt: (b, 0, 0)),
                pl.BlockSpec(memory_space=pl.ANY), pl.BlockSpec(memory_space=pl.ANY),
                pl.BlockSpec((MLA_HEADS, KV_LORA, V_HEAD), lambda b, c, pt: (0, 0, 0))]
    grid_spec = pltpu.PrefetchScalarGridSpec(
        num_scalar_prefetch=1, grid=(db, n_groups), in_specs=in_specs,
        out_specs=pl.BlockSpec((1, t_new, MLA_WIDTH), lambda b, c, pt: (b, 0, 0)),
        scratch_shapes=([pltpu.VMEM((pages, page, KV_LORA), F32)] * n_buf
                        + [pltpu.VMEM((pages, QK_ROPE, page), F32)] * n_buf
                        + [pltpu.SemaphoreType.DMA((2, n_buf)), pltpu.VMEM((rows, 1), F32),
                           pltpu.VMEM((rows, 1), F32), pltpu.VMEM((rows, KV_LORA), F32)]))
    return pl.pallas_call(
        functools.partial(_sample_attn_kernel, pages=pages, page=page, n_groups=n_groups, n_buf=n_buf, t_new=t_new,
                          n_sub=SAMPLE_SUB_BLOCKS),
        out_shape=jax.ShapeDtypeStruct((db, t_new, MLA_WIDTH), F32),
        grid_spec=grid_spec,
        compiler_params=_cparams(dimension_semantics=("arbitrary", "arbitrary")),
        name="sample_attn",
    )(page_table.reshape(-1), q_s, knew, cache_lat, cache_ropet, wuv)


def _gates(xc, wrg_ref, brg_ref, wig_ref, big_ref, lam_ref):
    xb = xc.astype(BF16)
    r = jnp.concatenate([_dot(xb[:, n * LRU_BLOCK:(n + 1) * LRU_BLOCK], wrg_ref[n]) for n in range(LRU_BLOCKS)], axis=1)
    i = jnp.concatenate([_dot(xb[:, n * LRU_BLOCK:(n + 1) * LRU_BLOCK], wig_ref[n]) for n in range(LRU_BLOCKS)], axis=1)
    r = jax.nn.sigmoid(r + brg_ref[...])
    i = jax.nn.sigmoid(i + big_ref[...])
    neg_lam = -lam_ref[...]
    softplus = jnp.maximum(neg_lam, 0.0) + jnp.log(1.0 + jnp.exp(-jnp.abs(neg_lam)))
    log_a = -LRU_C * r * softplus
    a = jnp.exp(log_a)
    u = jnp.sqrt(1.0 - jnp.exp(2.0 * log_a)) * i * xc
    return a, u


def _gelu(y):
    return 0.5 * y * (1.0 + jnp.tanh(0.7978845608028654 * (y + 0.044715 * (y * y * y))))


def _route(logits):
    lane = lax.broadcasted_iota(jnp.int32, logits.shape, 1)
    far = jnp.int32(4 * ROUTE_LANES)
    gmask = lane < N_GROUPS
    gl = jnp.where(gmask, logits, NEG_INF)
    gmax = jnp.max(gl, axis=1, keepdims=True)
    gidx = jnp.min(jnp.where(gl == gmax, lane, far), axis=1, keepdims=True)
    g_w = 1.0 / jnp.sum(jnp.where(gmask, jnp.exp(gl - gmax), 0.0), axis=1, keepdims=True)
    lo = N_GROUPS + gidx * EXPERTS_PER_GROUP
    emask = jnp.logical_and(lane >= lo, lane < lo + EXPERTS_PER_GROUP)
    el = jnp.where(emask, logits, NEG_INF)
    emax = jnp.max(el, axis=1, keepdims=True)
    ex = jnp.where(emask, jnp.exp(el - emax), 0.0)
    prob = jnp.where(emask, ex / jnp.sum(ex, axis=1, keepdims=True), -1.0)
    p1 = jnp.max(prob, axis=1, keepdims=True)
    i1 = jnp.min(jnp.where(prob == p1, lane, far), axis=1, keepdims=True)
    rest = jnp.where(lane == i1, -1.0, prob)
    p2 = jnp.max(rest, axis=1, keepdims=True)
    i2 = jnp.min(jnp.where(rest == p2, lane, far), axis=1, keepdims=True)
    den = p1 + p2
    w1 = g_w * p1 / den
    w2 = g_w * p2 / den
    e1 = (i1 - N_GROUPS).astype(F32)
    e2 = (i2 - N_GROUPS).astype(F32)
    out = jnp.where(lane == 0, e1, jnp.where(lane == 1, e2, jnp.where(lane == 2, w1, jnp.where(lane == 3, w2, 0.0))))
    return out


def _mix_tail(o_mla, o_lru, x, gmla_ref, glru_ref, wo_ref, gffn_ref, wr_ref, br_ref):
    mixed = jnp.concatenate([_rms(o_mla, gmla_ref[...]), _rms(o_lru, glru_ref[...])], axis=1).astype(BF16)
    h1 = x + _dot(mixed, wo_ref[...])
    t = _rms(h1, gffn_ref[...])
    logits = _dot(t.astype(BF16), wr_ref[...]) + br_ref[...]
    return h1, t, _route(logits)


def _mix_prompt_kernel(xbr_ref, ybr_ref, omla_ref, x_ref, wconv_ref, bconv_ref, wrg_ref, brg_ref, wig_ref, big_ref,
                       lam_ref, gmla_ref, glru_ref, wo_ref, gffn_ref, wr_ref, br_ref,
                       h1_ref, t_ref, route_ref, lru_ref, hc_ref, tail_ref, *, tc):
    @pl.when(pl.program_id(1) == 0)
    def _():
        hc_ref[...] = jnp.zeros(hc_ref.shape, F32)
        tail_ref[...] = jnp.zeros(tail_ref.shape, F32)

    xb = xbr_ref[...]
    tail = tail_ref[...]
    row8 = lax.broadcasted_iota(jnp.int32, tail.shape, 0)
    xc = bconv_ref[...] + xb * wconv_ref[CONV_W - 1:CONV_W, :]
    for j in range(1, CONV_W):
        sh = pltpu.roll(xb, j, axis=0)
        top = jnp.where(row8 < j, pltpu.roll(tail, j, axis=0), sh[0:8])
        sh = jnp.concatenate([top, sh[8:]], axis=0)
        xc = xc + sh * wconv_ref[CONV_W - 1 - j:CONV_W - j, :]
    tail_ref[...] = xb[tc - 8:tc]

    a, u = _gates(xc, wrg_ref, brg_ref, wig_ref, big_ref, lam_ref)
    row = lax.broadcasted_iota(jnp.int32, a.shape, 0)
    d = 1
    while d < tc:
        keep = row >= d
        u = jnp.where(keep, a * pltpu.roll(u, d, axis=0) + u, u)
        a = jnp.where(keep, a * pltpu.roll(a, d, axis=0), a)
        d *= 2
    hs = a * hc_ref[...] + u
    hc_ref[...] = hs[tc - 1:tc]
    lru_ref[...] = hs[tc - 1:tc]

    o_lru = _gelu(ybr_ref[...]) * hs
    h1, t, route = _mix_tail(omla_ref[...], o_lru, x_ref[...], gmla_ref, glru_ref, wo_ref, gffn_ref, wr_ref, br_ref)
    h1_ref[...] = h1
    t_ref[...] = t
    route_ref[...] = route


def _mix_weight_specs():
    return [_const_spec((CONV_W, LRU_WIDTH)), _const_spec((1, LRU_WIDTH)),
            _const_spec((LRU_BLOCKS, LRU_BLOCK, LRU_BLOCK)), _const_spec((1, LRU_WIDTH)),
            _const_spec((LRU_BLOCKS, LRU_BLOCK, LRU_BLOCK)), _const_spec((1, LRU_WIDTH)),
            _const_spec((1, LRU_WIDTH)), _const_spec((1, MLA_WIDTH)), _const_spec((1, LRU_WIDTH)),
            _const_spec((D_MODEL, D_MODEL)), _const_spec((1, D_MODEL)),
            _const_spec((D_MODEL, ROUTE_LANES)), _const_spec((1, ROUTE_LANES))]


def _mix_weights(w):
    return (w["w_conv"], w["b_conv"], w["w_rg"], w["b_rg"], w["w_ig"], w["b_ig"], w["lru_lambda"],
            w["g_out_mla"], w["g_out_lru"], w["w_o"], w["g_ffn"], w["w_route"], w["b_route"])


def _mix_prompt(xbr, ybr, omla, x2d, w, batch, seq, tc):
    nt = seq // tc
    n = batch * seq
    row = lambda width: pl.BlockSpec((tc, width), lambda b, i: (b * nt + i, 0))
    return pl.pallas_call(
        functools.partial(_mix_prompt_kernel, tc=tc),
        out_shape=(jax.ShapeDtypeStruct((n, D_MODEL), F32), jax.ShapeDtypeStruct((n, D_MODEL), F32),
                   jax.ShapeDtypeStruct((n, ROUTE_LANES), F32), jax.ShapeDtypeStruct((batch, 1, LRU_WIDTH), F32)),
        grid=(batch, nt),
        in_specs=[row(LRU_WIDTH), row(LRU_WIDTH), row(MLA_WIDTH), row(D_MODEL)] + _mix_weight_specs(),
        out_specs=(row(D_MODEL), row(D_MODEL), row(ROUTE_LANES),
                   pl.BlockSpec((None, 1, LRU_WIDTH), lambda b, i: (b, 0, 0))),
        scratch_shapes=[pltpu.VMEM((1, LRU_WIDTH), F32), pltpu.VMEM((8, LRU_WIDTH), F32)],
        compiler_params=_cparams(dimension_semantics=("arbitrary", "arbitrary")),
        name="mix_prompt",
    )(xbr, ybr, omla, x2d, *_mix_weights(w))


def _mix_sample_kernel(xbr_ref, ybr_ref, omla_ref, x_ref, sconv_ref, slru_ref, wconv_ref, bconv_ref, wrg_ref, brg_ref,
                       wig_ref, big_ref, lam_ref, gmla_ref, glru_ref, wo_ref, gffn_ref, wr_ref, br_ref,
                       h1_ref, t_ref, route_ref, lru_ref, *, t_new):
    w_ = LRU_WIDTH
    xs = [sconv_ref[:, k * w_:(k + 1) * w_] for k in range(CONV_W - 1)] + [xbr_ref[:, t * w_:(t + 1) * w_] for t in range(t_new)]
    xc = []
    for t in range(t_new):
        acc = bconv_ref[...] + xs[t] * wconv_ref[0:1, :]
        for k in range(1, CONV_W):
            acc = acc + xs[t + k] * wconv_ref[k:k + 1, :]
        xc.append(acc)
    xc = jnp.concatenate(xc, axis=0)
    a, u = _gates(xc, wrg_ref, brg_ref, wig_ref, big_ref, lam_ref)
    nb = xbr_ref.shape[0]
    h = slru_ref[...]
    hs = []
    for t in range(t_new):
        h = a[t * nb:(t + 1) * nb] * h + u[t * nb:(t + 1) * nb]
        hs.append(h)
    lru_ref[...] = h
    hs = jnp.concatenate(hs, axis=0)
    stack = lambda ref, width: jnp.concatenate([ref[:, t * width:(t + 1) * width] for t in range(t_new)], axis=0)
    o_lru = _gelu(stack(ybr_ref, w_)) * hs
    h1, tt, route = _mix_tail(stack(omla_ref, MLA_WIDTH), o_lru, stack(x_ref, D_MODEL),
                              gmla_ref, glru_ref, wo_ref, gffn_ref, wr_ref, br_ref)
    for t in range(t_new):
        h1_ref[:, t * D_MODEL:(t + 1) * D_MODEL] = h1[t * nb:(t + 1) * nb]
        t_ref[:, t * D_MODEL:(t + 1) * D_MODEL] = tt[t * nb:(t + 1) * nb]
        route_ref[:, t * ROUTE_LANES:(t + 1) * ROUTE_LANES] = route[t * nb:(t + 1) * nb]


def _mix_sample(xbr, ybr, omla, x2d, sconv, slru, w, t_new):
    nb = slru.shape[0]
    full = lambda a: pl.BlockSpec(a.shape, lambda i: (0, 0))
    ins = (xbr, ybr, omla, x2d, sconv, slru)
    return pl.pallas_call(
        functools.partial(_mix_sample_kernel, t_new=t_new),
        out_shape=(jax.ShapeDtypeStruct((nb, t_new * D_MODEL), F32), jax.ShapeDtypeStruct((nb, t_new * D_MODEL), F32),
                   jax.ShapeDtypeStruct((nb, t_new * ROUTE_LANES), F32), jax.ShapeDtypeStruct((nb, LRU_WIDTH), F32)),
        grid=(1,),
        in_specs=[full(a) for a in ins] + _mix_weight_specs(),
        out_specs=(pl.BlockSpec((nb, t_new * D_MODEL), lambda i: (0, 0)), pl.BlockSpec((nb, t_new * D_MODEL), lambda i: (0, 0)),
                   pl.BlockSpec((nb, t_new * ROUTE_LANES), lambda i: (0, 0)), pl.BlockSpec((nb, LRU_WIDTH), lambda i: (0, 0))),
        compiler_params=_cparams(dimension_semantics=("arbitrary",)),
        name="mix_sample",
    )(*ins, *_mix_weights(w))


MOE_DMA_UNROLL = 32
MOE_SPARE_ROWS = 2 * MOE_DMA_UNROLL


def _moe_kernel(te_ref, nv_ref, tok_ref, pair_ref, t_hbm, wg_ref, wu_ref, wd_ref, y_hbm, xbuf, obuf, gsem, ssem, *, tm):
    del te_ref
    j = pl.program_id(0)
    n_tiles = pl.num_programs(0)
    slot = j % 2

    def rows(jj):
        return pl.multiple_of(nv_ref[jj], MOE_DMA_UNROLL)

    def row_groups(jj, issue):
        n = rows(jj)
        for g in range(tm // MOE_DMA_UNROLL):
            @pl.when(n > g * MOE_DMA_UNROLL)
            def _(g=g):
                for u in range(MOE_DMA_UNROLL):
                    issue(g * MOE_DMA_UNROLL + u)

    def start_gather(jj, sl):
        def issue(r):
            pltpu.make_async_copy(t_hbm.at[pl.ds(tok_ref[jj * tm + r], 1)], xbuf.at[sl, pl.ds(r, 1)], gsem.at[sl]).start()

        row_groups(jj, issue)

    def wait_gather(jj, sl):
        n = rows(jj)

        @pl.when(n > 0)
        def _():
            pltpu.make_async_copy(t_hbm.at[pl.ds(0, n)], xbuf.at[sl, pl.ds(0, n)], gsem.at[sl]).wait()

    def start_scatter(jj, sl):
        def issue(r):
            pltpu.make_async_copy(obuf.at[sl, pl.ds(r, 1)], y_hbm.at[pl.ds(pair_ref[jj * tm + r], 1)], ssem.at[sl]).start()

        row_groups(jj, issue)

    def wait_scatter(jj, sl):
        n = rows(jj)

        @pl.when(n > 0)
        def _():
            pltpu.make_async_copy(obuf.at[sl, pl.ds(0, n)], y_hbm.at[pl.ds(0, n)], ssem.at[sl]).wait()

    @pl.when(j == 0)
    def _():
        xbuf[...] = jnp.zeros(xbuf.shape, F32)
        spare = y_hbm.shape[0] - MOE_SPARE_ROWS
        init = pltpu.make_async_copy(xbuf.at[0, pl.ds(0, MOE_SPARE_ROWS)], y_hbm.at[pl.ds(spare, MOE_SPARE_ROWS)],
                                     ssem.at[0])
        init.start()
        init.wait()
        start_gather(0, 0)

    @pl.when(j + 1 < n_tiles)
    def _():
        start_gather(j + 1, 1 - slot)

    wait_gather(j, slot)

    @pl.when(j >= 2)
    def _():
        wait_scatter(j - 2, slot)

    @pl.when(rows(j) > 0)
    def _():
        x = xbuf[slot].astype(BF16)
        hg = _dot(x, wg_ref[...].astype(BF16))
        hu = _dot(x, wu_ref[...].astype(BF16))
        act = (hg * jax.nn.sigmoid(hg) * hu).astype(BF16)
        obuf[slot] = _dot(act, wd_ref[...].astype(BF16))

    start_scatter(j, slot)

    @pl.when(j == n_tiles - 1)
    def _():
        wait_scatter(j - 1, 1 - slot)
        wait_scatter(j, slot)


def _moe(t_all, tile_expert, tile_rows, tok_of_slot, pair_of_slot, w_gate, w_up, w_down, tm):
    n_tiles = tile_expert.shape[0]
    n_tok = t_all.shape[0]
    assert n_tiles >= 2
    wspec = lambda a, b: pl.BlockSpec((None, a, b), lambda j, te, nv, tk, pr: (te[j], 0, 0))
    grid_spec = pltpu.PrefetchScalarGridSpec(
        num_scalar_prefetch=4, grid=(n_tiles,),
        in_specs=[pl.BlockSpec(memory_space=pl.ANY), wspec(D_MODEL, D_EXPERT), wspec(D_MODEL, D_EXPERT),
                  wspec(D_EXPERT, D_MODEL)],
        out_specs=pl.BlockSpec(memory_space=pl.ANY),
        scratch_shapes=[pltpu.VMEM((2, tm, D_MODEL), F32), pltpu.VMEM((2, tm, D_MODEL), F32),
                        pltpu.SemaphoreType.DMA((2,)), pltpu.SemaphoreType.DMA((2,))])
    return pl.pallas_call(
        functools.partial(_moe_kernel, tm=tm),
        out_shape=jax.ShapeDtypeStruct((2 * n_tok + MOE_SPARE_ROWS, D_MODEL), F32),
        grid_spec=grid_spec,
        compiler_params=_cparams(dimension_semantics=("arbitrary",)),
        name="moe",
    )(tile_expert, tile_rows, tok_of_slot, pair_of_slot, t_all, w_gate, w_up, w_down)


def _moe_plan(route_all, tm):
    n = route_all.shape[0]
    n_pairs = 2 * n
    eflat = jnp.concatenate([route_all[:, 0], route_all[:, 1]]).astype(jnp.int32)
    onehot = (eflat[:, None] == jnp.arange(N_EXPERTS, dtype=jnp.int32)[None, :]).astype(jnp.int32)
    csum = jnp.cumsum(onehot, axis=0)
    rank = jnp.take_along_axis(csum, eflat[:, None], axis=1)[:, 0] - 1
    counts = csum[-1]
    tiles_e = (counts + tm - 1) // tm
    tile_end = jnp.cumsum(tiles_e)
    tile_start = tile_end - tiles_e
    n_used = tile_end[-1]
    n_tiles = n_pairs // tm + N_EXPERTS
    slot = tile_start[eflat] * tm + rank
    tid = jnp.arange(n_tiles, dtype=jnp.int32)
    te = jnp.minimum(jnp.sum((tile_end[None, :] <= tid[:, None]).astype(jnp.int32), axis=1), N_EXPERTS - 1)
    te = jnp.where(tid < n_used, te, te[jnp.maximum(n_used - 1, 0)])
    rows = jnp.clip(counts[te] - (tid - tile_start[te]) * tm, 0, tm)
    rows = jnp.where(tid < n_used, rows, 0)
    rows8 = (rows + MOE_DMA_UNROLL - 1) // MOE_DMA_UNROLL * MOE_DMA_UNROLL
    sid = jnp.arange(n_tiles * tm, dtype=jnp.int32)
    pad_pair = n_pairs + ((sid // tm) % 2) * MOE_DMA_UNROLL + sid % MOE_DMA_UNROLL
    pair_of_slot = pad_pair.at[slot].set(jnp.arange(n_pairs, dtype=jnp.int32))
    tok_of_slot = jnp.where(pair_of_slot >= n_pairs, pair_of_slot - n_pairs,
                            jnp.where(pair_of_slot >= n, pair_of_slot - n, pair_of_slot))
    return te, rows8.astype(jnp.int32), tok_of_slot, pair_of_slot


def _final_kernel(h1_ref, y0_ref, y1_ref, route_ref, p_ref, gple_ref, wpg_ref, bpg_ref, wpp_ref, gfin_ref, y_ref):
    route = route_ref[...]
    h2 = h1_ref[...] + (route[:, 2:3] * y0_ref[...] + route[:, 3:4] * y1_ref[...])
    gate = jax.nn.sigmoid(_dot(_rms(h2, gple_ref[...]).astype(BF16), wpg_ref[...]) + bpg_ref[...])
    h3 = h2 + gate * _dot(p_ref[...].astype(BF16), wpp_ref[...])
    y_ref[...] = _rms(h3, gfin_ref[...])


def _final(h1, y_pairs, route, p2d, w, blk0, n_all, tm):
    n = h1.shape[0]
    nb_all = n_all // tm
    row = lambda width: pl.BlockSpec((tm, width), lambda i: (i, 0))
    return pl.pallas_call(
        _final_kernel,
        out_shape=jax.ShapeDtypeStruct((n, D_MODEL), F32),
        grid=(n // tm,),
        in_specs=[row(D_MODEL),
                  pl.BlockSpec((tm, D_MODEL), lambda i: (blk0 + i, 0)),
                  pl.BlockSpec((tm, D_MODEL), lambda i: (nb_all + blk0 + i, 0)),
                  row(ROUTE_LANES), row(PLE_DIM), _const_spec((1, D_MODEL)), _const_spec((D_MODEL, D_MODEL)),
                  _const_spec((1, D_MODEL)), _const_spec((PLE_DIM, D_MODEL)), _const_spec((1, D_MODEL))],
        out_specs=row(D_MODEL),
        compiler_params=_cparams(dimension_semantics=("arbitrary",)),
        name="final",
    )(h1, y_pairs, y_pairs, route, p2d, w["g_ple"], w["w_ple_gate"], w["b_ple_gate"], w["w_ple_proj"], w["g_final"])


def _rope_tables(pos):
    half = QK_ROPE // 2
    inv = ROPE_THETA ** (-jnp.arange(half, dtype=F32) / half)
    ang = pos.astype(F32)[:, None] * inv[None, :]
    pad = jnp.zeros((pos.shape[0], ROPE_PAD - QK_ROPE), F32)
    cos = jnp.concatenate([jnp.cos(ang), jnp.cos(ang), pad], axis=1)
    sin = jnp.concatenate([jnp.sin(ang), jnp.sin(ang), pad], axis=1)
    return cos, sin


def _rot_cols(wr):
    half = QK_ROPE // 2
    return jnp.concatenate([-wr[..., half:], wr[..., :half]], axis=-1)


def _pad_cols(wr):
    return jnp.concatenate([wr, jnp.zeros(wr.shape[:-1] + (ROPE_PAD - QK_ROPE,), wr.dtype)], axis=-1)


def _prep_weights(g_mix, w_in, g_q, w_uq, g_kv, w_ukv, w_conv, b_conv, w_rg, b_rg, w_ig, b_ig, lru_lambda,
                  g_out_mla, g_out_lru, w_o, g_ffn, w_group, b_group, w_router, b_router, g_ple, w_ple_gate,
                  b_ple_gate, w_ple_proj, g_final):
    row = lambda v: v.reshape(1, -1).astype(F32)
    s0, s1, s2, s3 = Q_LORA, Q_LORA + KV_LORA, Q_LORA + KV_LORA + QK_ROPE, Q_LORA + KV_LORA + QK_ROPE + LRU_WIDTH
    kr = w_in[:, s1:s2]
    w_in_ext = jnp.concatenate([w_in[:, :s1], _pad_cols(kr), _pad_cols(_rot_cols(kr)), w_in[:, s2:]], axis=1)
    uq = w_uq.reshape(Q_LORA, MLA_HEADS, QK_NOPE + QK_ROPE)
    uq_r = uq[..., QK_NOPE:]
    w_uq_ext = jnp.concatenate([uq[..., :QK_NOPE].reshape(Q_LORA, -1), _pad_cols(uq_r).reshape(Q_LORA, -1),
                                _pad_cols(_rot_cols(uq_r)).reshape(Q_LORA, -1)], axis=1)
    w_uk = jnp.transpose(w_ukv[..., :QK_NOPE], (1, 2, 0))
    w_uv = jnp.transpose(w_ukv[..., QK_NOPE:], (1, 0, 2))
    w_route = jnp.concatenate([w_group, w_router, jnp.zeros((D_MODEL, ROUTE_LANES - N_GROUPS - N_EXPERTS), F32)], axis=1)
    b_route = jnp.concatenate([b_group, b_router, jnp.zeros((ROUTE_LANES - N_GROUPS - N_EXPERTS,), F32)])
    return dict(
        g_mix=row(g_mix), w_in=w_in_ext.astype(BF16), g_q=row(g_q), w_uq=w_uq_ext.astype(BF16), g_kv=row(g_kv),
        w_uk=w_uk.astype(BF16), w_uv=w_uv.astype(BF16), w_conv=w_conv.astype(F32), b_conv=row(b_conv),
        w_rg=w_rg.astype(BF16), b_rg=row(b_rg), w_ig=w_ig.astype(BF16), b_ig=row(b_ig), lru_lambda=row(lru_lambda),
        g_out_mla=row(g_out_mla), g_out_lru=row(g_out_lru), w_o=w_o.astype(BF16), g_ffn=row(g_ffn),
        w_route=w_route.astype(BF16), b_route=row(b_route), g_ple=row(g_ple), w_ple_gate=w_ple_gate.astype(BF16),
        b_ple_gate=row(b_ple_gate), w_ple_proj=w_ple_proj.astype(BF16), g_final=row(g_final))


TM_PROJ = 256
TQ_ATTN = 256
TC_MIX = 256
TM_MOE = 256
PAGES_PER_STEP = 32
SAMPLE_BUFFERS = 4
SAMPLE_SUB_BLOCKS = 4


def kernel(x_prompt, x_sample, p_prompt, p_sample, cache_latent, cache_krope, state_lru, state_conv, page_table, g_mix, w_in, g_q, w_uq, g_kv, w_ukv, w_conv, b_conv, w_rg, b_rg, w_ig, b_ig, lru_lambda, g_out_mla, g_out_lru, w_o, g_ffn, w_group, b_group, w_router, b_router, w_gate, w_up, w_down, g_ple, w_ple_gate, b_ple_gate, w_ple_proj, g_final):
    assert w_in.shape[0] == 1, "single trunk layer"
    batch, seq, _ = x_prompt.shape
    db, t_new, _ = x_sample.shape
    page = cache_latent.shape[2]
    past_len = page_table.shape[1] * page
    n_p, n_s = batch * seq, db * t_new
    n_all = n_p + n_s

    w = _prep_weights(g_mix[0], w_in[0], g_q[0], w_uq[0], g_kv[0], w_ukv[0], w_conv[0], b_conv[0], w_rg[0], b_rg[0],
                      w_ig[0], b_ig[0], lru_lambda[0], g_out_mla[0], g_out_lru[0], w_o[0], g_ffn[0], w_group[0],
                      b_group[0], w_router[0], b_router[0], g_ple[0], w_ple_gate[0], b_ple_gate[0], w_ple_proj[0],
                      g_final)

    xp = x_prompt.reshape(n_p, D_MODEL)
    cos_p, sin_p = _rope_tables(jnp.arange(seq))
    qcat_p, kcat_p, lat_p, krope_p, xbr_p, ybr_p = _proj(xp, cos_p, sin_p, seq // TM_PROJ, w, TM_PROJ)
    omla_p = _prompt_attn(qcat_p, kcat_p, w["w_uv"], batch, seq, TQ_ATTN)
    h1_p, t_p, route_p, lru_p = _mix_prompt(xbr_p, ybr_p, omla_p, xp, w, batch, seq, TC_MIX)

    xs = x_sample.reshape(n_s, D_MODEL)
    tm_s = min(TM_PROJ, n_s)
    cos_s, sin_s = _rope_tables(past_len + (jnp.arange(tm_s) % t_new))
    qcat_s, kcat_s, lat_s, krope_s, xbr_s, ybr_s = _proj(xs, cos_s, sin_s, 1, w, tm_s)
    q_s = qcat_s.reshape(db, t_new, MLA_HEADS, QK_CAT).transpose(0, 2, 1, 3).reshape(db, MLA_HEADS * t_new, QK_CAT)
    knew = jnp.pad(kcat_s.reshape(db, t_new, QK_CAT), ((0, 0), (0, 8 - t_new), (0, 0)))
    cache_ropet = jnp.swapaxes(cache_krope[0], 1, 2)
    omla_s = _sample_attn(page_table, q_s, knew, cache_latent[0], cache_ropet, w["w_uv"], PAGES_PER_STEP)
    h1_s, t_s, route_s, lru_s = _mix_sample(
        xbr_s.reshape(db, t_new * LRU_WIDTH), ybr_s.reshape(db, t_new * LRU_WIDTH),
        omla_s.reshape(db, t_new * MLA_WIDTH), x_sample.reshape(db, t_new * D_MODEL),
        state_conv[0].reshape(db, (CONV_W - 1) * LRU_WIDTH), state_lru[0], w, t_new)
    h1_s = h1_s.reshape(n_s, D_MODEL)
    route_s = route_s.reshape(n_s, ROUTE_LANES)

    t_all = jnp.concatenate([t_p, t_s.reshape(n_s, D_MODEL)], axis=0)
    route_all = jnp.concatenate([route_p, route_s], axis=0)
    te, rows8, tok_of_slot, pair_of_slot = _moe_plan(route_all, TM_MOE)
    y_pairs = _moe(t_all, te, rows8, tok_of_slot, pair_of_slot, w_gate[0], w_up[0], w_down[0], TM_MOE)

    y_p = _final(h1_p, y_pairs, route_p, p_prompt[0].reshape(n_p, PLE_DIM), w, 0, n_all, TM_PROJ)
    y_s = _final(h1_s, y_pairs, route_s, p_sample[0].reshape(n_s, PLE_DIM), w, n_p // TM_PROJ, n_all, TM_PROJ)

    new_conv_p = xbr_p.reshape(batch, seq, LRU_WIDTH)[:, seq - (CONV_W - 1):]
    hist = jnp.concatenate([state_conv[0], xbr_s.reshape(db, t_new, LRU_WIDTH)], axis=1)
    new_conv_s = hist[:, hist.shape[1] - (CONV_W - 1):]
    return (y_p.reshape(batch, seq, D_MODEL), y_s.reshape(db, t_new, D_MODEL),
            lat_p.reshape(1, batch, seq, KV_LORA), krope_p.reshape(1, batch, seq, QK_ROPE),
            lru_p.reshape(1, batch, LRU_WIDTH), new_conv_p[None],
            lat_s.reshape(1, db, t_new, KV_LORA), krope_s.reshape(1, db, t_new, QK_ROPE),
            lru_s[None], new_conv_s[None])
```

```python
import functools

import jax
import jax.numpy as jnp
from jax import lax
from jax.experimental import pallas as pl
from jax.experimental.pallas import tpu as pltpu

F32 = jnp.float32
BF16 = jnp.bfloat16

D_MODEL = 2048
MLA_HEADS = 8
V_HEAD = 128
MLA_WIDTH = MLA_HEADS * V_HEAD
LRU_WIDTH = D_MODEL - MLA_WIDTH
QK_NOPE = 128
QK_ROPE = 64
Q_LORA = 512
KV_LORA = 256
ROPE_THETA = 10000.0
SM_SCALE = (QK_NOPE + QK_ROPE) ** -0.5
NEG_INF = -1e30
LRU_BLOCKS = 8
LRU_BLOCK = LRU_WIDTH // LRU_BLOCKS
CONV_W = 4
LRU_C = 8.0
N_GROUPS = 4
EXPERTS_PER_GROUP = 8
N_EXPERTS = N_GROUPS * EXPERTS_PER_GROUP
D_EXPERT = 512
PLE_DIM = 256
EPS = 1e-6

LANES = 128
ROPE_PAD = LANES
QK_CAT = KV_LORA + ROPE_PAD
C_Q0, C_KV0, C_KRA0, C_KRB0, C_X0, C_Y0, IN_EXT = 0, 512, 768, 896, 1024, 2048, 3072
ROUTE_LANES = LANES
VMEM_LIMIT = 56 * 1024 * 1024


def _cparams(**kw):
    return pltpu.CompilerParams(vmem_limit_bytes=VMEM_LIMIT, **kw)


def _rms(x, g):
    return x * lax.rsqrt(jnp.mean(x * x, axis=-1, keepdims=True) + EPS) * g


def _dot(a, b):
    return jnp.dot(a, b, preferred_element_type=F32)


def _dot_nt(a, b):
    return lax.dot_general(a, b, (((1,), (1,)), ((), ())), preferred_element_type=F32)


def _const_spec(shape):
    nd = len(shape)
    return pl.BlockSpec(shape, lambda *_: (0,) * nd, pipeline_mode=pl.Buffered(1))


def _proj_kernel(x_ref, cos_ref, sin_ref, gmix_ref, win_ref, gq_ref, wuq_ref, gkv_ref, wuk_ref,
                 qcat_ref, kcat_ref, lat_ref, krope_ref, xbr_ref, ybr_ref):
    u = _rms(x_ref[...], gmix_ref[...]).astype(BF16)
    z = _dot(u, win_ref[...])
    xbr_ref[...] = z[:, C_X0:C_Y0]
    ybr_ref[...] = z[:, C_Y0:IN_EXT]
    cos = cos_ref[...]
    sin = sin_ref[...]
    lat = _rms(z[:, C_KV0:C_KRA0], gkv_ref[...])
    kr = z[:, C_KRA0:C_KRB0] * cos + z[:, C_KRB0:C_X0] * sin
    lat_ref[...] = lat
    krope_ref[...] = kr[:, :QK_ROPE]
    kcat_ref[:, 0:KV_LORA] = lat.astype(BF16)
    kcat_ref[:, KV_LORA:QK_CAT] = kr.astype(BF16)
    qn = _rms(z[:, C_Q0:C_KV0], gq_ref[...]).astype(BF16)
    q = _dot(qn, wuq_ref[...])
    ra0 = MLA_HEADS * QK_NOPE
    rb0 = ra0 + MLA_HEADS * ROPE_PAD
    for h in range(MLA_HEADS):
        ql = _dot(q[:, h * QK_NOPE:(h + 1) * QK_NOPE].astype(BF16), wuk_ref[h])
        qr = (q[:, ra0 + h * ROPE_PAD:ra0 + (h + 1) * ROPE_PAD] * cos
              + q[:, rb0 + h * ROPE_PAD:rb0 + (h + 1) * ROPE_PAD] * sin)
        qcat_ref[:, h * QK_CAT:h * QK_CAT + KV_LORA] = ql.astype(BF16)
        qcat_ref[:, h * QK_CAT + KV_LORA:(h + 1) * QK_CAT] = qr.astype(BF16)


def _proj(x2d, cos, sin, pos_blocks, w, tm):
    n = x2d.shape[0]
    row = lambda width: pl.BlockSpec((tm, width), lambda i: (i, 0))
    pos = pl.BlockSpec((tm, ROPE_PAD), lambda i: (i % pos_blocks, 0))
    out_shape = (
        jax.ShapeDtypeStruct((n, MLA_HEADS * QK_CAT), BF16),
        jax.ShapeDtypeStruct((n, QK_CAT), BF16),
        jax.ShapeDtypeStruct((n, KV_LORA), F32),
        jax.ShapeDtypeStruct((n, QK_ROPE), F32),
        jax.ShapeDtypeStruct((n, LRU_WIDTH), F32),
        jax.ShapeDtypeStruct((n, LRU_WIDTH), F32),
    )
    return pl.pallas_call(
        _proj_kernel,
        out_shape=out_shape,
        grid=(n // tm,),
        in_specs=[row(D_MODEL), pos, pos, _const_spec((1, D_MODEL)), _const_spec((D_MODEL, IN_EXT)),
                  _const_spec((1, Q_LORA)), _const_spec((Q_LORA, 3 * MLA_HEADS * LANES)),
                  _const_spec((1, KV_LORA)), _const_spec((MLA_HEADS, QK_NOPE, KV_LORA))],
        out_specs=(row(MLA_HEADS * QK_CAT), row(QK_CAT), row(KV_LORA), row(QK_ROPE), row(LRU_WIDTH), row(LRU_WIDTH)),
        compiler_params=_cparams(dimension_semantics=("arbitrary",)),
        name="proj",
    )(x2d, cos, sin, w["g_mix"], w["w_in"], w["g_q"], w["w_uq"], w["g_kv"], w["w_uk"])


def _prompt_attn_kernel(q_ref, k_ref, wuv_ref, o_ref, *scratch, tq, group):
    m_refs, l_refs, acc_refs = scratch[:MLA_HEADS], scratch[MLA_HEADS:2 * MLA_HEADS], scratch[2 * MLA_HEADS:]
    qi = pl.program_id(1)
    for h in range(MLA_HEADS):
        m_refs[h][...] = jnp.full(m_refs[h].shape, NEG_INF, F32)
        l_refs[h][...] = jnp.zeros(l_refs[h].shape, F32)
        acc_refs[h][...] = jnp.zeros(acc_refs[h].shape, F32)

    def step(kb, masked):
        k = k_ref[pl.ds(pl.multiple_of(kb * tq, tq), tq), :]
        v = k[:, :KV_LORA]
        if masked:
            row = lax.broadcasted_iota(jnp.int32, (tq, tq), 0)
            col = lax.broadcasted_iota(jnp.int32, (tq, tq), 1)
            keep = col <= row
        for h0 in range(0, MLA_HEADS, group):
            heads = range(h0, h0 + group)
            scores = [_dot_nt(q_ref[:, h * QK_CAT:(h + 1) * QK_CAT], k) * SM_SCALE for h in heads]
            probs = []
            for h, s in zip(heads, scores):
                if masked:
                    s = jnp.where(keep, s, NEG_INF)
                m_prev = m_refs[h][...]
                m_new = jnp.maximum(m_prev, jnp.max(s, axis=1, keepdims=True))
                alpha = jnp.exp(m_prev - m_new)
                p = jnp.exp(s - jnp.concatenate([m_new] * (tq // LANES), axis=1))
                l_refs[h][...] = alpha * l_refs[h][...] + jnp.sum(p, axis=1, keepdims=True)
                m_refs[h][...] = m_new
                probs.append((alpha, p.astype(BF16)))
            for h, (alpha, p) in zip(heads, probs):
                acc_refs[h][...] = jnp.concatenate([alpha] * (KV_LORA // LANES), axis=1) * acc_refs[h][...] + _dot(p, v)

    def body(kb, carry):
        step(kb, False)
        return carry

    lax.fori_loop(0, qi, body, 0)
    step(qi, True)
    for h in range(MLA_HEADS):
        o = acc_refs[h][...] / jnp.concatenate([l_refs[h][...]] * (KV_LORA // LANES), axis=1)
        o_ref[:, h * V_HEAD:(h + 1) * V_HEAD] = _dot(o.astype(BF16), wuv_ref[h])


def _prompt_attn(qcat, kcat, wuv, batch, seq, tq):
    nq = seq // tq
    return pl.pallas_call(
        functools.partial(_prompt_attn_kernel, tq=tq, group=ATTN_HEAD_GROUP),
        out_shape=jax.ShapeDtypeStruct((batch * seq, MLA_WIDTH), F32),
        grid=(batch, nq),
        in_specs=[pl.BlockSpec((tq, MLA_HEADS * QK_CAT), lambda b, i: (b * nq + i, 0)),
                  pl.BlockSpec((seq, QK_CAT), lambda b, i: (b, 0)),
                  _const_spec((MLA_HEADS, KV_LORA, V_HEAD))],
        out_specs=pl.BlockSpec((tq, MLA_WIDTH), lambda b, i: (b * nq + i, 0)),
        scratch_shapes=([pltpu.VMEM((tq, LANES), F32)] * (2 * MLA_HEADS) + [pltpu.VMEM((tq, KV_LORA), F32)] * MLA_HEADS),
        compiler_params=_cparams(dimension_semantics=("arbitrary", "arbitrary")),
        name="prompt_attn",
    )(qcat, kcat, wuv)


def _sample_attn_kernel(pt_ref, q_ref, knew_ref, lat_hbm, ropet_hbm, wuv_ref, o_ref, *scratch,
                        pages, page, n_groups, n_buf, t_new, n_sub):
    lat_bufs, rope_bufs = scratch[:n_buf], scratch[n_buf:2 * n_buf]
    sem, m_ref, l_ref, acc_ref = scratch[2 * n_buf:]
    c = pl.program_id(1)
    step = pl.program_id(0) * n_groups + c
    n_steps = pl.num_programs(0) * n_groups
    bufs = tuple(zip(lat_bufs, rope_bufs))

    def start_chunk(chunk, k):
        for g in range(pages):
            pid = pt_ref[chunk * pages + g]
            pltpu.make_async_copy(lat_hbm.at[pid], bufs[k][0].at[g], sem.at[0, k]).start()
            pltpu.make_async_copy(ropet_hbm.at[pid], bufs[k][1].at[g], sem.at[1, k]).start()

    def wait_chunk(k):
        pltpu.make_async_copy(lat_hbm.at[pl.ds(0, pages)], bufs[k][0], sem.at[0, k]).wait()
        pltpu.make_async_copy(ropet_hbm.at[pl.ds(0, pages)], bufs[k][1], sem.at[1, k]).wait()

    q = q_ref[0]
    q_lat = q[:, :KV_LORA]
    q_rope = q[:, KV_LORA:KV_LORA + QK_ROPE]
    ps = pages // n_sub

    def attend(k):
        latbuf, ropebuf = bufs[k]
        lats, scores, parts = [], [], []
        for sb in range(n_sub):
            lat = latbuf[sb * ps:(sb + 1) * ps].reshape(ps * page, KV_LORA).astype(BF16)
            s_rope = jnp.concatenate([_dot(q_rope, ropebuf[sb * ps + g].astype(BF16)) for g in range(ps)], axis=1)
            lats.append(lat)
            scores.append((_dot_nt(q_lat, lat) + s_rope) * SM_SCALE)
        for sb in range(n_sub):
            m_i = jnp.max(scores[sb], axis=1, keepdims=True)
            p = jnp.exp(scores[sb] - m_i)
            parts.append((m_i, jnp.sum(p, axis=1, keepdims=True), p.astype(BF16)))
        parts = [(m_i, l_i, _dot(p, lats[sb])) for sb, (m_i, l_i, p) in enumerate(parts)]
        m_prev = m_ref[...]
        m_new = m_prev
        for m_i, _, _ in parts:
            m_new = jnp.maximum(m_new, m_i)
        alpha = jnp.exp(m_prev - m_new)
        l = alpha * l_ref[...]
        acc = alpha * acc_ref[...]
        for m_i, l_i, o_i in parts:
            w_i = jnp.exp(m_i - m_new)
            l = l + w_i * l_i
            acc = acc + w_i * o_i
        l_ref[...] = l
        acc_ref[...] = acc
        m_ref[...] = m_new

    @pl.when(step == 0)
    def _():
        for k in range(n_buf):
            start_chunk(k, k)

    @pl.when(c == 0)
    def _():
        m_ref[...] = jnp.full(m_ref.shape, NEG_INF, F32)
        l_ref[...] = jnp.zeros(l_ref.shape, F32)
        acc_ref[...] = jnp.zeros(acc_ref.shape, F32)

    last_chunk = n_buf * n_steps - 1
    for k in range(n_buf):
        wait_chunk(k)
        attend(k)
        start_chunk(jnp.minimum(n_buf * (step + 1) + k, last_chunk), k)

    @pl.when(step == n_steps - 1)
    def _():
        for k in range(n_buf):
            wait_chunk(k)

    @pl.when(c == n_groups - 1)
    def _():
        qf = q.astype(F32)
        kn = knew_ref[0].astype(F32)
        tok = lax.broadcasted_iota(jnp.int32, (q.shape[0], 1), 0) & (t_new - 1)
        cols = []
        for j in range(t_new):
            sj = jnp.sum(qf * kn[j:j + 1, :], axis=1, keepdims=True) * SM_SCALE
            cols.append(jnp.where(tok >= j, sj, NEG_INF))
        m_prev = m_ref[...]
        m_new = m_prev
        for sj in cols:
            m_new = jnp.maximum(m_new, sj)
        alpha = jnp.exp(m_prev - m_new)
        l = alpha * l_ref[...]
        acc = alpha * acc_ref[...]
        for j, sj in enumerate(cols):
            pj = jnp.exp(sj - m_new)
            l = l + pj
            acc = acc + pj * kn[j:j + 1, :KV_LORA]
        o = (acc / l).astype(BF16)
        for h in range(MLA_HEADS):
            oh = _dot(o, wuv_ref[h])
            o_ref[0, :, h * V_HEAD:(h + 1) * V_HEAD] = oh[h * t_new:(h + 1) * t_new]


def _sample_attn(page_table, q_s, knew, cache_lat, cache_ropet, wuv, pages):
    db, n_pages = page_table.shape
    page = cache_lat.shape[1]
    n_buf = SAMPLE_BUFFERS
    n_groups = n_pages // (n_buf * pages)
    assert n_groups * n_buf * pages == n_pages
    rows = q_s.shape[1]
    t_new = rows // MLA_HEADS
    in_specs = [pl.BlockSpec((1, rows, QK_CAT), lambda b, c, pt: (b, 0, 0)),
                pl.BlockSpec((1, 8, QK_CAT), lambda b, c, pt: (b, 0, 0)),
                pl.BlockSpec(memory_space=pl.ANY), pl.BlockSpec(memory_space=pl.ANY),
                pl.BlockSpec((MLA_HEADS, KV_LORA, V_HEAD), lambda b, c, pt: (0, 0, 0))]
    grid_spec = pltpu.PrefetchScalarGridSpec(
        num_scalar_prefetch=1, grid=(db, n_groups), in_specs=in_specs,
        out_specs=pl.BlockSpec((1, t_new, MLA_WIDTH), lambda b, c, pt: (b, 0, 0)),
        scratch_shapes=([pltpu.VMEM((pages, page, KV_LORA), F32)] * n_buf
                        + [pltpu.VMEM((pages, QK_ROPE, page), F32)] * n_buf
                        + [pltpu.SemaphoreType.DMA((2, n_buf)), pltpu.VMEM((rows, 1), F32),
                           pltpu.VMEM((rows, 1), F32), pltpu.VMEM((rows, KV_LORA), F32)]))
    return pl.pallas_call(
        functools.partial(_sample_attn_kernel, pages=pages, page=page, n_groups=n_groups, n_buf=n_buf, t_new=t_new,
                          n_sub=SAMPLE_SUB_BLOCKS),
        out_shape=jax.ShapeDtypeStruct((db, t_new, MLA_WIDTH), F32),
        grid_spec=grid_spec,
        compiler_params=_cparams(dimension_semantics=("arbitrary", "arbitrary")),
        name="sample_attn",
    )(page_table.reshape(-1), q_s, knew, cache_lat, cache_ropet, wuv)


def _gate_block(xc, n, wrg_ref, brg_ref, wig_ref, big_ref, lam_ref):
    cs = slice(n * LRU_BLOCK, (n + 1) * LRU_BLOCK)
    xb = xc.astype(BF16)
    r = jax.nn.sigmoid(_dot(xb, wrg_ref[n]) + brg_ref[:, cs])
    i = jax.nn.sigmoid(_dot(xb, wig_ref[n]) + big_ref[:, cs])
    neg_lam = -lam_ref[:, cs]
    softplus = jnp.maximum(neg_lam, 0.0) + jnp.log(1.0 + jnp.exp(-jnp.abs(neg_lam)))
    log_a = -LRU_C * r * softplus
    a = jnp.exp(log_a)
    gap = 1.0 - a * a
    root = jnp.where(gap > 0.0, gap * lax.rsqrt(gap), 0.0)
    u = root * i * xc
    return a, u


def _gates(xc, wrg_ref, brg_ref, wig_ref, big_ref, lam_ref):
    blocks = [_gate_block(xc[:, n * LRU_BLOCK:(n + 1) * LRU_BLOCK], n, wrg_ref, brg_ref, wig_ref, big_ref, lam_ref)
              for n in range(LRU_BLOCKS)]
    return (jnp.concatenate([a for a, _ in blocks], axis=1), jnp.concatenate([u for _, u in blocks], axis=1))


def _gelu(y):
    return 0.5 * y * (1.0 + jnp.tanh(0.7978845608028654 * (y + 0.044715 * (y * y * y))))


def _route(logits):
    lane = lax.broadcasted_iota(jnp.int32, logits.shape, 1)
    far = jnp.int32(4 * ROUTE_LANES)
    gmask = lane < N_GROUPS
    gl = jnp.where(gmask, logits, NEG_INF)
    gmax = jnp.max(gl, axis=1, keepdims=True)
    gidx = jnp.min(jnp.where(gl == gmax, lane, far), axis=1, keepdims=True)
    g_w = 1.0 / jnp.sum(jnp.where(gmask, jnp.exp(gl - gmax), 0.0), axis=1, keepdims=True)
    lo = N_GROUPS + gidx * EXPERTS_PER_GROUP
    emask = jnp.logical_and(lane >= lo, lane < lo + EXPERTS_PER_GROUP)
    el = jnp.where(emask, logits, NEG_INF)
    emax = jnp.max(el, axis=1, keepdims=True)
    ex = jnp.where(emask, jnp.exp(el - emax), 0.0)
    prob = jnp.where(emask, ex / jnp.sum(ex, axis=1, keepdims=True), -1.0)
    p1 = jnp.max(prob, axis=1, keepdims=True)
    i1 = jnp.min(jnp.where(prob == p1, lane, far), axis=1, keepdims=True)
    rest = jnp.where(lane == i1, -1.0, prob)
    p2 = jnp.max(rest, axis=1, keepdims=True)
    i2 = jnp.min(jnp.where(rest == p2, lane, far), axis=1, keepdims=True)
    den = p1 + p2
    w1 = g_w * p1 / den
    w2 = g_w * p2 / den
    e1 = (i1 - N_GROUPS).astype(F32)
    e2 = (i2 - N_GROUPS).astype(F32)
    out = jnp.where(lane == 0, e1, jnp.where(lane == 1, e2, jnp.where(lane == 2, w1, jnp.where(lane == 3, w2, 0.0))))
    return out


def _mix_tail(o_mla, o_lru, x, gmla_ref, glru_ref, wo_ref, gffn_ref, wr_ref, br_ref):
    mixed = jnp.concatenate([_rms(o_mla, gmla_ref[...]), _rms(o_lru, glru_ref[...])], axis=1).astype(BF16)
    h1 = x + _dot(mixed, wo_ref[...])
    t = _rms(h1, gffn_ref[...])
    logits = _dot(t.astype(BF16), wr_ref[...]) + br_ref[...]
    return h1, t, _route(logits)


def _mix_prompt_kernel(xbr_ref, ybr_ref, omla_ref, x_ref, wconv_ref, bconv_ref, wrg_ref, brg_ref, wig_ref, big_ref,
                       lam_ref, gmla_ref, glru_ref, wo_ref, gffn_ref, wr_ref, br_ref,
                       h1_ref, t_ref, route_ref, lru_ref, hc_ref, tail_ref, olru_ref, *, tc):
    @pl.when(pl.program_id(1) == 0)
    def _():
        hc_ref[...] = jnp.zeros(hc_ref.shape, F32)
        tail_ref[...] = jnp.zeros(tail_ref.shape, F32)

    sub8 = lax.broadcasted_iota(jnp.int32, (tc // 8, 8, LRU_BLOCK), 1)
    row = lax.broadcasted_iota(jnp.int32, (tc, LRU_BLOCK), 0)
    for n in range(LRU_BLOCKS):
        cs = slice(n * LRU_BLOCK, (n + 1) * LRU_BLOCK)
        xb = xbr_ref[:, cs]
        tail = tail_ref[:, cs]
        xc = bconv_ref[:, cs] + xb * wconv_ref[CONV_W - 1:CONV_W, cs]
        x3 = jnp.concatenate([tail, xb], axis=0).reshape(tc // 8 + 1, 8, LRU_BLOCK)
        for j in range(1, CONV_W):
            rot = pltpu.roll(x3, j, axis=1)
            sh = jnp.where(sub8 < j, rot[:-1], rot[1:]).reshape(tc, LRU_BLOCK)
            xc = xc + sh * wconv_ref[CONV_W - 1 - j:CONV_W - j, cs]
        tail_ref[:, cs] = xb[tc - 8:tc]

        a, u = _gate_block(xc, n, wrg_ref, brg_ref, wig_ref, big_ref, lam_ref)
        d = 1
        while d < tc:
            keep = row >= d
            u = jnp.where(keep, a * pltpu.roll(u, d, axis=0) + u, u)
            a = jnp.where(keep, a * pltpu.roll(a, d, axis=0), a)
            d *= 2
        hs = a * hc_ref[:, cs] + u
        hc_ref[:, cs] = hs[tc - 1:tc]
        lru_ref[:, cs] = hs[tc - 1:tc]
        olru_ref[:, cs] = _gelu(ybr_ref[:, cs]) * hs

    h1, t, route = _mix_tail(omla_ref[...], olru_ref[...], x_ref[...], gmla_ref, glru_ref, wo_ref, gffn_ref, wr_ref, br_ref)
    h1_ref[...] = h1
    t_ref[...] = t
    route_ref[...] = route


def _mix_weight_specs():
    return [_const_spec((CONV_W, LRU_WIDTH)), _const_spec((1, LRU_WIDTH)),
            _const_spec((LRU_BLOCKS, LRU_BLOCK, LRU_BLOCK)), _const_spec((1, LRU_WIDTH)),
            _const_spec((LRU_BLOCKS, LRU_BLOCK, LRU_BLOCK)), _const_spec((1, LRU_WIDTH)),
            _const_spec((1, LRU_WIDTH)), _const_spec((1, MLA_WIDTH)), _const_spec((1, LRU_WIDTH)),
            _const_spec((D_MODEL, D_MODEL)), _const_spec((1, D_MODEL)),
            _const_spec((D_MODEL, ROUTE_LANES)), _const_spec((1, ROUTE_LANES))]


def _mix_weights(w):
    return (w["w_conv"], w["b_conv"], w["w_rg"], w["b_rg"], w["w_ig"], w["b_ig"], w["lru_lambda"],
            w["g_out_mla"], w["g_out_lru"], w["w_o"], w["g_ffn"], w["w_route"], w["b_route"])


def _mix_prompt(xbr, ybr, omla, x2d, w, batch, seq, tc):
    nt = seq // tc
    n = batch * seq
    row = lambda width: pl.BlockSpec((tc, width), lambda b, i: (b * nt + i, 0))
    return pl.pallas_call(
        functools.partial(_mix_prompt_kernel, tc=tc),
        out_shape=(jax.ShapeDtypeStruct((n, D_MODEL), F32), jax.ShapeDtypeStruct((n, D_MODEL), F32),
                   jax.ShapeDtypeStruct((n, ROUTE_LANES), F32), jax.ShapeDtypeStruct((batch, 1, LRU_WIDTH), F32)),
        grid=(batch, nt),
        in_specs=[row(LRU_WIDTH), row(LRU_WIDTH), row(MLA_WIDTH), row(D_MODEL)] + _mix_weight_specs(),
        out_specs=(row(D_MODEL), row(D_MODEL), row(ROUTE_LANES),
                   pl.BlockSpec((None, 1, LRU_WIDTH), lambda b, i: (b, 0, 0))),
        scratch_shapes=[pltpu.VMEM((1, LRU_WIDTH), F32), pltpu.VMEM((8, LRU_WIDTH), F32),
                        pltpu.VMEM((tc, LRU_WIDTH), F32)],
        compiler_params=_cparams(dimension_semantics=("arbitrary", "arbitrary")),
        name="mix_prompt",
    )(xbr, ybr, omla, x2d, *_mix_weights(w))


def _mix_sample_kernel(xbr_ref, ybr_ref, omla_ref, x_ref, sconv_ref, slru_ref, wconv_ref, bconv_ref, wrg_ref, brg_ref,
                       wig_ref, big_ref, lam_ref, gmla_ref, glru_ref, wo_ref, gffn_ref, wr_ref, br_ref,
                       h1_ref, t_ref, route_ref, lru_ref, *, t_new):
    w_ = LRU_WIDTH
    xs = [sconv_ref[:, k * w_:(k + 1) * w_] for k in range(CONV_W - 1)] + [xbr_ref[:, t * w_:(t + 1) * w_] for t in range(t_new)]
    xc = []
    for t in range(t_new):
        acc = bconv_ref[...] + xs[t] * wconv_ref[0:1, :]
        for k in range(1, CONV_W):
            acc = acc + xs[t + k] * wconv_ref[k:k + 1, :]
        xc.append(acc)
    xc = jnp.concatenate(xc, axis=0)
    a, u = _gates(xc, wrg_ref, brg_ref, wig_ref, big_ref, lam_ref)
    nb = xbr_ref.shape[0]
    h = slru_ref[...]
    hs = []
    for t in range(t_new):
        h = a[t * nb:(t + 1) * nb] * h + u[t * nb:(t + 1) * nb]
        hs.append(h)
    lru_ref[...] = h
    hs = jnp.concatenate(hs, axis=0)
    stack = lambda ref, width: jnp.concatenate([ref[:, t * width:(t + 1) * width] for t in range(t_new)], axis=0)
    o_lru = _gelu(stack(ybr_ref, w_)) * hs
    h1, tt, route = _mix_tail(stack(omla_ref, MLA_WIDTH), o_lru, stack(x_ref, D_MODEL),
                              gmla_ref, glru_ref, wo_ref, gffn_ref, wr_ref, br_ref)
    for t in range(t_new):
        h1_ref[:, t * D_MODEL:(t + 1) * D_MODEL] = h1[t * nb:(t + 1) * nb]
        t_ref[:, t * D_MODEL:(t + 1) * D_MODEL] = tt[t * nb:(t + 1) * nb]
        route_ref[:, t * ROUTE_LANES:(t + 1) * ROUTE_LANES] = route[t * nb:(t + 1) * nb]


def _mix_sample(xbr, ybr, omla, x2d, sconv, slru, w, t_new):
    nb = slru.shape[0]
    full = lambda a: pl.BlockSpec(a.shape, lambda i: (0, 0))
    ins = (xbr, ybr, omla, x2d, sconv, slru)
    return pl.pallas_call(
        functools.partial(_mix_sample_kernel, t_new=t_new),
        out_shape=(jax.ShapeDtypeStruct((nb, t_new * D_MODEL), F32), jax.ShapeDtypeStruct((nb, t_new * D_MODEL), F32),
                   jax.ShapeDtypeStruct((nb, t_new * ROUTE_LANES), F32), jax.ShapeDtypeStruct((nb, LRU_WIDTH), F32)),
        grid=(1,),
        in_specs=[full(a) for a in ins] + _mix_weight_specs(),
        out_specs=(pl.BlockSpec((nb, t_new * D_MODEL), lambda i: (0, 0)), pl.BlockSpec((nb, t_new * D_MODEL), lambda i: (0, 0)),
                   pl.BlockSpec((nb, t_new * ROUTE_LANES), lambda i: (0, 0)), pl.BlockSpec((nb, LRU_WIDTH), lambda i: (0, 0))),
        compiler_params=_cparams(dimension_semantics=("arbitrary",)),
        name="mix_sample",
    )(*ins, *_mix_weights(w))


def _moe_kernel(te_ref, nu_ref, tok_ref, pair_ref, t_hbm, wg_ref, wu_ref, wd_ref, y_hbm, xa, xb, oa, ob, gsem, ssem, *, tm):
    del te_ref
    j = pl.program_id(0)
    n_used = nu_ref[0]
    xbufs, obufs = (xa, xb), (oa, ob)
    spare = y_hbm.shape[0] - 2 * tm

    def start_gather(jj, k):
        for r in range(tm):
            pltpu.make_async_copy(t_hbm.at[pl.ds(tok_ref[jj * tm + r], 1)], xbufs[k].at[pl.ds(r, 1)], gsem.at[k]).start()

    def wait_gather(k):
        pltpu.make_async_copy(t_hbm.at[pl.ds(0, tm)], xbufs[k], gsem.at[k]).wait()

    def start_scatter(jj, k):
        for r in range(tm):
            pltpu.make_async_copy(obufs[k].at[pl.ds(r, 1)], y_hbm.at[pl.ds(pair_ref[(jj + 1) * tm + r], 1)], ssem.at[k]).start()

    def wait_scatter(k):
        pltpu.make_async_copy(obufs[k], y_hbm.at[pl.ds(0, tm)], ssem.at[k]).wait()

    @pl.when(j == 0)
    def _():
        oa[...] = jnp.zeros(oa.shape, F32)
        ob[...] = jnp.zeros(ob.shape, F32)
        pltpu.make_async_copy(oa, y_hbm.at[pl.ds(spare, tm)], ssem.at[0]).start()
        start_gather(0, 0)

    def tile(k):
        wait_gather(k)
        wait_scatter(k)
        start_gather(jnp.minimum(j + 1, n_used - 1), 1 - k)
        start_scatter(j - 1, 1 - k)
        x = xbufs[k][...].astype(BF16)
        hg = _dot(x, wg_ref[...].astype(BF16))
        hu = _dot(x, wu_ref[...].astype(BF16))
        act = (hg * jax.nn.sigmoid(hg) * hu).astype(BF16)
        obufs[k][...] = _dot(act, wd_ref[...].astype(BF16))

    def drain(k):
        start_scatter(j, k)
        wait_gather(1 - k)
        wait_scatter(1 - k)
        wait_scatter(k)

    for k in range(2):
        pl.when(jnp.logical_and(j < n_used, j % 2 == k))(functools.partial(tile, k))
        pl.when(jnp.logical_and(j == n_used - 1, j % 2 == k))(functools.partial(drain, k))


def _moe(t_all, tile_expert, n_used, tok_of_slot, pair_of_slot, w_gate, w_up, w_down, tm):
    n_tiles = tile_expert.shape[0]
    n_tok = t_all.shape[0]
    wspec = lambda a, b: pl.BlockSpec((None, a, b), lambda j, te, nu, tk, pr: (te[j], 0, 0))
    grid_spec = pltpu.PrefetchScalarGridSpec(
        num_scalar_prefetch=4, grid=(n_tiles,),
        in_specs=[pl.BlockSpec(memory_space=pl.ANY), wspec(D_MODEL, D_EXPERT), wspec(D_MODEL, D_EXPERT),
                  wspec(D_EXPERT, D_MODEL)],
        out_specs=pl.BlockSpec(memory_space=pl.ANY),
        scratch_shapes=[pltpu.VMEM((tm, D_MODEL), F32)] * 4 + [pltpu.SemaphoreType.DMA((2,)), pltpu.SemaphoreType.DMA((2,))])
    return pl.pallas_call(
        functools.partial(_moe_kernel, tm=tm),
        out_shape=jax.ShapeDtypeStruct((2 * n_tok + 2 * tm, D_MODEL), F32),
        grid_spec=grid_spec,
        compiler_params=_cparams(dimension_semantics=("arbitrary",)),
        name="moe",
    )(tile_expert, n_used, tok_of_slot, pair_of_slot, t_all, w_gate, w_up, w_down)


def _moe_plan(route_all, tm):
    n = route_all.shape[0]
    n_pairs = 2 * n
    eflat = jnp.concatenate([route_all[:, 0], route_all[:, 1]]).astype(jnp.int32)
    onehot = (eflat[:, None] == jnp.arange(N_EXPERTS, dtype=jnp.int32)[None, :]).astype(jnp.int32)
    csum = jnp.cumsum(onehot, axis=0)
    rank = jnp.take_along_axis(csum, eflat[:, None], axis=1)[:, 0] - 1
    counts = csum[-1]
    tiles_e = (counts + tm - 1) // tm
    tile_end = jnp.cumsum(tiles_e)
    tile_start = tile_end - tiles_e
    n_used = tile_end[-1]
    n_tiles = n_pairs // tm + N_EXPERTS
    slot = tile_start[eflat] * tm + rank
    tid = jnp.arange(n_tiles, dtype=jnp.int32)
    te = jnp.minimum(jnp.sum((tile_end[None, :] <= tid[:, None]).astype(jnp.int32), axis=1), N_EXPERTS - 1)
    te = jnp.where(tid < n_used, te, te[jnp.maximum(n_used - 1, 0)])
    sid = jnp.arange(n_tiles * tm, dtype=jnp.int32)
    pad_pair = n_pairs + sid % (2 * tm)
    pair_of_slot = pad_pair.at[slot].set(jnp.arange(n_pairs, dtype=jnp.int32))
    tok_of_slot = jnp.where(pair_of_slot >= n_pairs, pair_of_slot - n_pairs,
                            jnp.where(pair_of_slot >= n, pair_of_slot - n, pair_of_slot))
    pair_shifted = jnp.concatenate([n_pairs + tm + jnp.arange(tm, dtype=jnp.int32), pair_of_slot])
    return te, n_used.reshape(1).astype(jnp.int32), tok_of_slot, pair_shifted


def _final_kernel(h1_ref, y0_ref, y1_ref, route_ref, p_ref, gple_ref, wpg_ref, bpg_ref, wpp_ref, gfin_ref, y_ref):
    route = route_ref[...]
    h2 = h1_ref[...] + (route[:, 2:3] * y0_ref[...] + route[:, 3:4] * y1_ref[...])
    gate = jax.nn.sigmoid(_dot(_rms(h2, gple_ref[...]).astype(BF16), wpg_ref[...]) + bpg_ref[...])
    h3 = h2 + gate * _dot(p_ref[...].astype(BF16), wpp_ref[...])
    y_ref[...] = _rms(h3, gfin_ref[...])


def _final(h1, y_pairs, route, p2d, w, blk0, n_all, tm):
    n = h1.shape[0]
    nb_all = n_all // tm
    row = lambda width: pl.BlockSpec((tm, width), lambda i: (i, 0))
    return pl.pallas_call(
        _final_kernel,
        out_shape=jax.ShapeDtypeStruct((n, D_MODEL), F32),
        grid=(n // tm,),
        in_specs=[row(D_MODEL),
                  pl.BlockSpec((tm, D_MODEL), lambda i: (blk0 + i, 0)),
                  pl.BlockSpec((tm, D_MODEL), lambda i: (nb_all + blk0 + i, 0)),
                  row(ROUTE_LANES), row(PLE_DIM), _const_spec((1, D_MODEL)), _const_spec((D_MODEL, D_MODEL)),
                  _const_spec((1, D_MODEL)), _const_spec((PLE_DIM, D_MODEL)), _const_spec((1, D_MODEL))],
        out_specs=row(D_MODEL),
        compiler_params=_cparams(dimension_semantics=("arbitrary",)),
        name="final",
    )(h1, y_pairs, y_pairs, route, p2d, w["g_ple"], w["w_ple_gate"], w["b_ple_gate"], w["w_ple_proj"], w["g_final"])


def _rope_tables(pos):
    half = QK_ROPE // 2
    inv = ROPE_THETA ** (-jnp.arange(half, dtype=F32) / half)
    ang = pos.astype(F32)[:, None] * inv[None, :]
    pad = jnp.zeros((pos.shape[0], ROPE_PAD - QK_ROPE), F32)
    cos = jnp.concatenate([jnp.cos(ang), jnp.cos(ang), pad], axis=1)
    sin = jnp.concatenate([jnp.sin(ang), jnp.sin(ang), pad], axis=1)
    return cos, sin


def _rot_cols(wr):
    half = QK_ROPE // 2
    return jnp.concatenate([-wr[..., half:], wr[..., :half]], axis=-1)


def _pad_cols(wr):
    return jnp.concatenate([wr, jnp.zeros(wr.shape[:-1] + (ROPE_PAD - QK_ROPE,), wr.dtype)], axis=-1)


def _prep_weights(g_mix, w_in, g_q, w_uq, g_kv, w_ukv, w_conv, b_conv, w_rg, b_rg, w_ig, b_ig, lru_lambda,
                  g_out_mla, g_out_lru, w_o, g_ffn, w_group, b_group, w_router, b_router, g_ple, w_ple_gate,
                  b_ple_gate, w_ple_proj, g_final):
    row = lambda v: v.reshape(1, -1).astype(F32)
    s0, s1, s2, s3 = Q_LORA, Q_LORA + KV_LORA, Q_LORA + KV_LORA + QK_ROPE, Q_LORA + KV_LORA + QK_ROPE + LRU_WIDTH
    kr = w_in[:, s1:s2]
    w_in_ext = jnp.concatenate([w_in[:, :s1], _pad_cols(kr), _pad_cols(_rot_cols(kr)), w_in[:, s2:]], axis=1)
    uq = w_uq.reshape(Q_LORA, MLA_HEADS, QK_NOPE + QK_ROPE)
    uq_r = uq[..., QK_NOPE:]
    w_uq_ext = jnp.concatenate([uq[..., :QK_NOPE].reshape(Q_LORA, -1), _pad_cols(uq_r).reshape(Q_LORA, -1),
                                _pad_cols(_rot_cols(uq_r)).reshape(Q_LORA, -1)], axis=1)
    w_uk = jnp.transpose(w_ukv[..., :QK_NOPE], (1, 2, 0))
    w_uv = jnp.transpose(w_ukv[..., QK_NOPE:], (1, 0, 2))
    w_route = jnp.concatenate([w_group, w_router, jnp.zeros((D_MODEL, ROUTE_LANES - N_GROUPS - N_EXPERTS), F32)], axis=1)
    b_route = jnp.concatenate([b_group, b_router, jnp.zeros((ROUTE_LANES - N_GROUPS - N_EXPERTS,), F32)])
    return dict(
        g_mix=row(g_mix), w_in=w_in_ext.astype(BF16), g_q=row(g_q), w_uq=w_uq_ext.astype(BF16), g_kv=row(g_kv),
        w_uk=w_uk.astype(BF16), w_uv=w_uv.astype(BF16), w_conv=w_conv.astype(F32), b_conv=row(b_conv),
        w_rg=w_rg.astype(BF16), b_rg=row(b_rg), w_ig=w_ig.astype(BF16), b_ig=row(b_ig), lru_lambda=row(lru_lambda),
        g_out_mla=row(g_out_mla), g_out_lru=row(g_out_lru), w_o=w_o.astype(BF16), g_ffn=row(g_ffn),
        w_route=w_route.astype(BF16), b_route=row(b_route), g_ple=row(g_ple), w_ple_gate=w_ple_gate.astype(BF16),
        b_ple_gate=row(b_ple_gate), w_ple_proj=w_ple_proj.astype(BF16), g_final=row(g_final))


TM_PROJ = 512
TM_FINAL = 256
TQ_ATTN = 256
ATTN_HEAD_GROUP = 4
TC_MIX = 256
TM_MOE = 256
PAGES_PER_STEP = 32
SAMPLE_BUFFERS = 4
SAMPLE_SUB_BLOCKS = 4


def kernel(x_prompt, x_sample, p_prompt, p_sample, cache_latent, cache_krope, state_lru, state_conv, page_table, g_mix, w_in, g_q, w_uq, g_kv, w_ukv, w_conv, b_conv, w_rg, b_rg, w_ig, b_ig, lru_lambda, g_out_mla, g_out_lru, w_o, g_ffn, w_group, b_group, w_router, b_router, w_gate, w_up, w_down, g_ple, w_ple_gate, b_ple_gate, w_ple_proj, g_final):
    assert w_in.shape[0] == 1, "single trunk layer"
    batch, seq, _ = x_prompt.shape
    db, t_new, _ = x_sample.shape
    page = cache_latent.shape[2]
    past_len = page_table.shape[1] * page
    n_p, n_s = batch * seq, db * t_new
    n_all = n_p + n_s

    w = _prep_weights(g_mix[0], w_in[0], g_q[0], w_uq[0], g_kv[0], w_ukv[0], w_conv[0], b_conv[0], w_rg[0], b_rg[0],
                      w_ig[0], b_ig[0], lru_lambda[0], g_out_mla[0], g_out_lru[0], w_o[0], g_ffn[0], w_group[0],
                      b_group[0], w_router[0], b_router[0], g_ple[0], w_ple_gate[0], b_ple_gate[0], w_ple_proj[0],
                      g_final)

    xp = x_prompt.reshape(n_p, D_MODEL)
    cos_p, sin_p = _rope_tables(jnp.arange(seq))
    qcat_p, kcat_p, lat_p, krope_p, xbr_p, ybr_p = _proj(xp, cos_p, sin_p, seq // TM_PROJ, w, TM_PROJ)
    omla_p = _prompt_attn(qcat_p, kcat_p, w["w_uv"], batch, seq, TQ_ATTN)
    h1_p, t_p, route_p, lru_p = _mix_prompt(xbr_p, ybr_p, omla_p, xp, w, batch, seq, TC_MIX)

    xs = x_sample.reshape(n_s, D_MODEL)
    tm_s = min(TM_PROJ, n_s)
    cos_s, sin_s = _rope_tables(past_len + (jnp.arange(tm_s) % t_new))
    qcat_s, kcat_s, lat_s, krope_s, xbr_s, ybr_s = _proj(xs, cos_s, sin_s, 1, w, tm_s)
    q_s = qcat_s.reshape(db, t_new, MLA_HEADS, QK_CAT).transpose(0, 2, 1, 3).reshape(db, MLA_HEADS * t_new, QK_CAT)
    knew = jnp.pad(kcat_s.reshape(db, t_new, QK_CAT), ((0, 0), (0, 8 - t_new), (0, 0)))
    cache_ropet = jnp.swapaxes(cache_krope[0], 1, 2)
    omla_s = _sample_attn(page_table, q_s, knew, cache_latent[0], cache_ropet, w["w_uv"], PAGES_PER_STEP)
    h1_s, t_s, route_s, lru_s = _mix_sample(
        xbr_s.reshape(db, t_new * LRU_WIDTH), ybr_s.reshape(db, t_new * LRU_WIDTH),
        omla_s.reshape(db, t_new * MLA_WIDTH), x_sample.reshape(db, t_new * D_MODEL),
        state_conv[0].reshape(db, (CONV_W - 1) * LRU_WIDTH), state_lru[0], w, t_new)
    h1_s = h1_s.reshape(n_s, D_MODEL)
    route_s = route_s.reshape(n_s, ROUTE_LANES)

    t_all = jnp.concatenate([t_p, t_s.reshape(n_s, D_MODEL)], axis=0)
    route_all = jnp.concatenate([route_p, route_s], axis=0)
    te, n_used, tok_of_slot, pair_of_slot = _moe_plan(route_all, TM_MOE)
    y_pairs = _moe(t_all, te, n_used, tok_of_slot, pair_of_slot, w_gate[0], w_up[0], w_down[0], TM_MOE)

    y_p = _final(h1_p, y_pairs, route_p, p_prompt[0].reshape(n_p, PLE_DIM), w, 0, n_all, TM_FINAL)
    y_s = _final(h1_s, y_pairs, route_s, p_sample[0].reshape(n_s, PLE_DIM), w, n_p // TM_FINAL, n_all, TM_FINAL)

    new_conv_p = xbr_p.reshape(batch, seq, LRU_WIDTH)[:, seq - (CONV_W - 1):]
    hist = jnp.concatenate([state_conv[0], xbr_s.reshape(db, t_new, LRU_WIDTH)], axis=1)
    new_conv_s = hist[:, hist.shape[1] - (CONV_W - 1):]
    return (y_p.reshape(batch, seq, D_MODEL), y_s.reshape(db, t_new, D_MODEL),
            lat_p.reshape(1, batch, seq, KV_LORA), krope_p.reshape(1, batch, seq, QK_ROPE),
            lru_p.reshape(1, batch, LRU_WIDTH), new_conv_p[None],
            lat_s.reshape(1, db, t_new, KV_LORA), krope_s.reshape(1, db, t_new, QK_ROPE),
            lru_s[None], new_conv_s[None])
```

```python
import functools

import jax
import jax.numpy as jnp
from jax import lax
from jax.experimental import pallas as pl
from jax.experimental.pallas import tpu as pltpu

F32 = jnp.float32
BF16 = jnp.bfloat16

D_MODEL = 2048
MLA_HEADS = 8
V_HEAD = 128
MLA_WIDTH = MLA_HEADS * V_HEAD
LRU_WIDTH = D_MODEL - MLA_WIDTH
QK_NOPE = 128
QK_ROPE = 64
Q_LORA = 512
KV_LORA = 256
ROPE_THETA = 10000.0
SM_SCALE = (QK_NOPE + QK_ROPE) ** -0.5
NEG_INF = -1e30
LRU_BLOCKS = 8
LRU_BLOCK = LRU_WIDTH // LRU_BLOCKS
CONV_W = 4
LRU_C = 8.0
N_GROUPS = 4
EXPERTS_PER_GROUP = 8
N_EXPERTS = N_GROUPS * EXPERTS_PER_GROUP
D_EXPERT = 512
PLE_DIM = 256
EPS = 1e-6

LANES = 128
ROPE_PAD = LANES
QK_CAT = KV_LORA + ROPE_PAD
C_Q0, C_KV0, C_KRA0, C_KRB0, C_X0, C_Y0, IN_EXT = 0, 512, 768, 896, 1024, 2048, 3072
ROUTE_LANES = LANES
VMEM_LIMIT = 56 * 1024 * 1024


def _cparams(**kw):
    return pltpu.CompilerParams(vmem_limit_bytes=VMEM_LIMIT, **kw)


def _rms(x, g):
    return x * lax.rsqrt(jnp.mean(x * x, axis=-1, keepdims=True) + EPS) * g


def _dot(a, b):
    return jnp.dot(a, b, preferred_element_type=F32)


def _dot_nt(a, b):
    return lax.dot_general(a, b, (((1,), (1,)), ((), ())), preferred_element_type=F32)


def _const_spec(shape):
    nd = len(shape)
    return pl.BlockSpec(shape, lambda *_: (0,) * nd, pipeline_mode=pl.Buffered(1))


def _proj_kernel(x_ref, cos_ref, sin_ref, gmix_ref, win_ref, gq_ref, wuq_ref, gkv_ref, wuk_ref,
                 qcat_ref, kcat_ref, lat_ref, krope_ref, xbr_ref, ybr_ref):
    u = _rms(x_ref[...], gmix_ref[...]).astype(BF16)
    z = _dot(u, win_ref[...])
    xbr_ref[...] = z[:, C_X0:C_Y0]
    ybr_ref[...] = z[:, C_Y0:IN_EXT]
    cos = cos_ref[...]
    sin = sin_ref[...]
    lat = _rms(z[:, C_KV0:C_KRA0], gkv_ref[...])
    kr = z[:, C_KRA0:C_KRB0] * cos + z[:, C_KRB0:C_X0] * sin
    lat_ref[...] = lat
    krope_ref[...] = kr[:, :QK_ROPE]
    kcat_ref[:, 0:KV_LORA] = lat.astype(BF16)
    kcat_ref[:, KV_LORA:QK_CAT] = kr.astype(BF16)
    qn = _rms(z[:, C_Q0:C_KV0], gq_ref[...]).astype(BF16)
    q = _dot(qn, wuq_ref[...])
    ra0 = MLA_HEADS * QK_NOPE
    rb0 = ra0 + MLA_HEADS * ROPE_PAD
    for h in range(MLA_HEADS):
        ql = _dot(q[:, h * QK_NOPE:(h + 1) * QK_NOPE].astype(BF16), wuk_ref[h])
        qr = (q[:, ra0 + h * ROPE_PAD:ra0 + (h + 1) * ROPE_PAD] * cos
              + q[:, rb0 + h * ROPE_PAD:rb0 + (h + 1) * ROPE_PAD] * sin)
        qcat_ref[:, h * QK_CAT:h * QK_CAT + KV_LORA] = ql.astype(BF16)
        qcat_ref[:, h * QK_CAT + KV_LORA:(h + 1) * QK_CAT] = qr.astype(BF16)


def _proj(x2d, cos, sin, pos_blocks, w, tm):
    n = x2d.shape[0]
    row = lambda width: pl.BlockSpec((tm, width), lambda i: (i, 0))
    pos = pl.BlockSpec((tm, ROPE_PAD), lambda i: (i % pos_blocks, 0))
    out_shape = (
        jax.ShapeDtypeStruct((n, MLA_HEADS * QK_CAT), BF16),
        jax.ShapeDtypeStruct((n, QK_CAT), BF16),
        jax.ShapeDtypeStruct((n, KV_LORA), F32),
        jax.ShapeDtypeStruct((n, QK_ROPE), F32),
        jax.ShapeDtypeStruct((n, LRU_WIDTH), F32),
        jax.ShapeDtypeStruct((n, LRU_WIDTH), F32),
    )
    return pl.pallas_call(
        _proj_kernel,
        out_shape=out_shape,
        grid=(n // tm,),
        in_specs=[row(D_MODEL), pos, pos, _const_spec((1, D_MODEL)), _const_spec((D_MODEL, IN_EXT)),
                  _const_spec((1, Q_LORA)), _const_spec((Q_LORA, 3 * MLA_HEADS * LANES)),
                  _const_spec((1, KV_LORA)), _const_spec((MLA_HEADS, QK_NOPE, KV_LORA))],
        out_specs=(row(MLA_HEADS * QK_CAT), row(QK_CAT), row(KV_LORA), row(QK_ROPE), row(LRU_WIDTH), row(LRU_WIDTH)),
        compiler_params=_cparams(dimension_semantics=("arbitrary",)),
        name="proj",
    )(x2d, cos, sin, w["g_mix"], w["w_in"], w["g_q"], w["w_uq"], w["g_kv"], w["w_uk"])


def _prompt_attn_kernel(q_ref, k_ref, wuv_ref, o_ref, *scratch, tq, group):
    m_refs, l_refs, acc_refs = scratch[:MLA_HEADS], scratch[MLA_HEADS:2 * MLA_HEADS], scratch[2 * MLA_HEADS:]
    qi = pl.program_id(1)
    for h in range(MLA_HEADS):
        m_refs[h][...] = jnp.full(m_refs[h].shape, NEG_INF, F32)
        l_refs[h][...] = jnp.zeros(l_refs[h].shape, F32)
        acc_refs[h][...] = jnp.zeros(acc_refs[h].shape, F32)

    def step(kb, masked):
        k = k_ref[pl.ds(pl.multiple_of(kb * tq, tq), tq), :]
        v = k[:, :KV_LORA]
        if masked:
            row = lax.broadcasted_iota(jnp.int32, (tq, tq), 0)
            col = lax.broadcasted_iota(jnp.int32, (tq, tq), 1)
            keep = col <= row
        for h0 in range(0, MLA_HEADS, group):
            heads = range(h0, h0 + group)
            scores = [_dot_nt(q_ref[:, h * QK_CAT:(h + 1) * QK_CAT], k) * SM_SCALE for h in heads]
            probs = []
            for h, s in zip(heads, scores):
                if masked:
                    s = jnp.where(keep, s, NEG_INF)
                m_prev = m_refs[h][...]
                m_new = jnp.maximum(m_prev, jnp.max(s, axis=1, keepdims=True))
                alpha = jnp.exp(m_prev - m_new)
                p = jnp.exp(s - jnp.concatenate([m_new] * (tq // LANES), axis=1))
                l_refs[h][...] = alpha * l_refs[h][...] + jnp.sum(p, axis=1, keepdims=True)
                m_refs[h][...] = m_new
                probs.append((alpha, p.astype(BF16)))
            for h, (alpha, p) in zip(heads, probs):
                acc_refs[h][...] = jnp.concatenate([alpha] * (KV_LORA // LANES), axis=1) * acc_refs[h][...] + _dot(p, v)

    def body(kb, carry):
        step(kb, False)
        return carry

    lax.fori_loop(0, qi, body, 0)
    step(qi, True)
    for h in range(MLA_HEADS):
        o = acc_refs[h][...] / jnp.concatenate([l_refs[h][...]] * (KV_LORA // LANES), axis=1)
        o_ref[:, h * V_HEAD:(h + 1) * V_HEAD] = _dot(o.astype(BF16), wuv_ref[h])


def _prompt_attn(qcat, kcat, wuv, batch, seq, tq):
    nq = seq // tq
    return pl.pallas_call(
        functools.partial(_prompt_attn_kernel, tq=tq, group=ATTN_HEAD_GROUP),
        out_shape=jax.ShapeDtypeStruct((batch * seq, MLA_WIDTH), F32),
        grid=(batch, nq),
        in_specs=[pl.BlockSpec((tq, MLA_HEADS * QK_CAT), lambda b, i: (b * nq + i, 0)),
                  pl.BlockSpec((seq, QK_CAT), lambda b, i: (b, 0)),
                  _const_spec((MLA_HEADS, KV_LORA, V_HEAD))],
        out_specs=pl.BlockSpec((tq, MLA_WIDTH), lambda b, i: (b * nq + i, 0)),
        scratch_shapes=([pltpu.VMEM((tq, LANES), F32)] * (2 * MLA_HEADS) + [pltpu.VMEM((tq, KV_LORA), F32)] * MLA_HEADS),
        compiler_params=_cparams(dimension_semantics=("arbitrary", "arbitrary")),
        name="prompt_attn",
    )(qcat, kcat, wuv)


def _sample_attn_kernel(pt_ref, q_ref, knew_ref, lat_hbm, ropet_hbm, wuv_ref, o_ref, *scratch,
                        pages, page, n_groups, n_buf, t_new, n_sub):
    lat_bufs, rope_bufs = scratch[:n_buf], scratch[n_buf:2 * n_buf]
    sem, m_ref, l_ref, acc_ref = scratch[2 * n_buf:]
    c = pl.program_id(1)
    step = pl.program_id(0) * n_groups + c
    n_steps = pl.num_programs(0) * n_groups
    bufs = tuple(zip(lat_bufs, rope_bufs))

    def start_chunk(chunk, k):
        for g in range(pages):
            pid = pt_ref[chunk * pages + g]
            pltpu.make_async_copy(lat_hbm.at[pid], bufs[k][0].at[g], sem.at[0, k]).start()
            pltpu.make_async_copy(ropet_hbm.at[pid], bufs[k][1].at[g], sem.at[1, k]).start()

    def wait_chunk(k):
        pltpu.make_async_copy(lat_hbm.at[pl.ds(0, pages)], bufs[k][0], sem.at[0, k]).wait()
        pltpu.make_async_copy(ropet_hbm.at[pl.ds(0, pages)], bufs[k][1], sem.at[1, k]).wait()

    q = q_ref[0]
    q_lat = q[:, :KV_LORA]
    q_rope = q[:, KV_LORA:KV_LORA + QK_ROPE]
    ps = pages // n_sub

    def attend(k):
        latbuf, ropebuf = bufs[k]
        lats, scores, parts = [], [], []
        for sb in range(n_sub):
            lat = latbuf[sb * ps:(sb + 1) * ps].reshape(ps * page, KV_LORA).astype(BF16)
            s_rope = jnp.concatenate([_dot(q_rope, ropebuf[sb * ps + g].astype(BF16)) for g in range(ps)], axis=1)
            lats.append(lat)
            scores.append((_dot_nt(q_lat, lat) + s_rope) * SM_SCALE)
        for sb in range(n_sub):
            m_i = jnp.max(scores[sb], axis=1, keepdims=True)
            p = jnp.exp(scores[sb] - m_i)
            parts.append((m_i, jnp.sum(p, axis=1, keepdims=True), p.astype(BF16)))
        parts = [(m_i, l_i, _dot(p, lats[sb])) for sb, (m_i, l_i, p) in enumerate(parts)]
        m_prev = m_ref[...]
        m_new = m_prev
        for m_i, _, _ in parts:
            m_new = jnp.maximum(m_new, m_i)
        alpha = jnp.exp(m_prev - m_new)
        l = alpha * l_ref[...]
        acc = alpha * acc_ref[...]
        for m_i, l_i, o_i in parts:
            w_i = jnp.exp(m_i - m_new)
            l = l + w_i * l_i
            acc = acc + w_i * o_i
        l_ref[...] = l
        acc_ref[...] = acc
        m_ref[...] = m_new

    @pl.when(step == 0)
    def _():
        for k in range(n_buf):
            start_chunk(k, k)

    @pl.when(c == 0)
    def _():
        m_ref[...] = jnp.full(m_ref.shape, NEG_INF, F32)
        l_ref[...] = jnp.zeros(l_ref.shape, F32)
        acc_ref[...] = jnp.zeros(acc_ref.shape, F32)

    last_chunk = n_buf * n_steps - 1
    for k in range(n_buf):
        wait_chunk(k)
        attend(k)
        start_chunk(jnp.minimum(n_buf * (step + 1) + k, last_chunk), k)

    @pl.when(step == n_steps - 1)
    def _():
        for k in range(n_buf):
            wait_chunk(k)

    @pl.when(c == n_groups - 1)
    def _():
        qf = q.astype(F32)
        kn = knew_ref[0].astype(F32)
        tok = lax.broadcasted_iota(jnp.int32, (q.shape[0], 1), 0) & (t_new - 1)
        cols = []
        for j in range(t_new):
            sj = jnp.sum(qf * kn[j:j + 1, :], axis=1, keepdims=True) * SM_SCALE
            cols.append(jnp.where(tok >= j, sj, NEG_INF))
        m_prev = m_ref[...]
        m_new = m_prev
        for sj in cols:
            m_new = jnp.maximum(m_new, sj)
        alpha = jnp.exp(m_prev - m_new)
        l = alpha * l_ref[...]
        acc = alpha * acc_ref[...]
        for j, sj in enumerate(cols):
            pj = jnp.exp(sj - m_new)
            l = l + pj
            acc = acc + pj * kn[j:j + 1, :KV_LORA]
        o = (acc / l).astype(BF16)
        for h in range(MLA_HEADS):
            oh = _dot(o, wuv_ref[h])
            o_ref[0, :, h * V_HEAD:(h + 1) * V_HEAD] = oh[h * t_new:(h + 1) * t_new]


def _sample_attn(page_table, q_s, knew, cache_lat, cache_ropet, wuv, pages):
    db, n_pages = page_table.shape
    page = cache_lat.shape[1]
    n_buf = SAMPLE_BUFFERS
    n_groups = n_pages // (n_buf * pages)
    assert n_groups * n_buf * pages == n_pages
    rows = q_s.shape[1]
    t_new = rows // MLA_HEADS
    in_specs = [pl.BlockSpec((1, rows, QK_CAT), lambda b, c, pt: (b, 0, 0)),
                pl.BlockSpec((1, 8, QK_CAT), lambda b, c, pt: (b, 0, 0)),
                pl.BlockSpec(memory_space=pl.ANY), pl.BlockSpec(memory_space=pl.ANY),
                pl.BlockSpec((MLA_HEADS, KV_LORA, V_HEAD), lambda b, c, pt: (0, 0, 0))]
    grid_spec = pltpu.PrefetchScalarGridSpec(
        num_scalar_prefetch=1, grid=(db, n_groups), in_specs=in_specs,
        out_specs=pl.BlockSpec((1, t_new, MLA_WIDTH), lambda b, c, pt: (b, 0, 0)),
        scratch_shapes=([pltpu.VMEM((pages, page, KV_LORA), F32)] * n_buf
                        + [pltpu.VMEM((pages, QK_ROPE, page), F32)] * n_buf
                        + [pltpu.SemaphoreType.DMA((2, n_buf)), pltpu.VMEM((rows, 1), F32),
                           pltpu.VMEM((rows, 1), F32), pltpu.VMEM((rows, KV_LORA), F32)]))
    return pl.pallas_call(
        functools.partial(_sample_attn_kernel, pages=pages, page=page, n_groups=n_groups, n_buf=n_buf, t_new=t_new,
                          n_sub=SAMPLE_SUB_BLOCKS),
        out_shape=jax.ShapeDtypeStruct((db, t_new, MLA_WIDTH), F32),
        grid_spec=grid_spec,
        compiler_params=_cparams(dimension_semantics=("arbitrary", "arbitrary")),
        name="sample_attn",
    )(page_table.reshape(-1), q_s, knew, cache_lat, cache_ropet, wuv)


def _gate_block(xc, n, wrg_ref, brg_ref, wig_ref, big_ref, lam_ref):
    cs = slice(n * LRU_BLOCK, (n + 1) * LRU_BLOCK)
    xb = xc.astype(BF16)
    r = jax.nn.sigmoid(_dot(xb, wrg_ref[n]) + brg_ref[:, cs])
    i = jax.nn.sigmoid(_dot(xb, wig_ref[n]) + big_ref[:, cs])
    neg_lam = -lam_ref[:, cs]
    softplus = jnp.maximum(neg_lam, 0.0) + jnp.log(1.0 + jnp.exp(-jnp.abs(neg_lam)))
    log_a = -LRU_C * r * softplus
    a = jnp.exp(log_a)
    gap = 1.0 - a * a
    root = jnp.where(gap > 0.0, gap * lax.rsqrt(gap), 0.0)
    u = root * i * xc
    return a, u


def _gates(xc, wrg_ref, brg_ref, wig_ref, big_ref, lam_ref):
    blocks = [_gate_block(xc[:, n * LRU_BLOCK:(n + 1) * LRU_BLOCK], n, wrg_ref, brg_ref, wig_ref, big_ref, lam_ref)
              for n in range(LRU_BLOCKS)]
    return (jnp.concatenate([a for a, _ in blocks], axis=1), jnp.concatenate([u for _, u in blocks], axis=1))


def _gelu(y):
    return 0.5 * y * (1.0 + jnp.tanh(0.7978845608028654 * (y + 0.044715 * (y * y * y))))


def _route(logits):
    lane = lax.broadcasted_iota(jnp.int32, logits.shape, 1)
    far = jnp.int32(4 * ROUTE_LANES)
    gmask = lane < N_GROUPS
    gl = jnp.where(gmask, logits, NEG_INF)
    gmax = jnp.max(gl, axis=1, keepdims=True)
    gidx = jnp.min(jnp.where(gl == gmax, lane, far), axis=1, keepdims=True)
    g_w = 1.0 / jnp.sum(jnp.where(gmask, jnp.exp(gl - gmax), 0.0), axis=1, keepdims=True)
    lo = N_GROUPS + gidx * EXPERTS_PER_GROUP
    emask = jnp.logical_and(lane >= lo, lane < lo + EXPERTS_PER_GROUP)
    el = jnp.where(emask, logits, NEG_INF)
    emax = jnp.max(el, axis=1, keepdims=True)
    ex = jnp.where(emask, jnp.exp(el - emax), 0.0)
    prob = jnp.where(emask, ex / jnp.sum(ex, axis=1, keepdims=True), -1.0)
    p1 = jnp.max(prob, axis=1, keepdims=True)
    i1 = jnp.min(jnp.where(prob == p1, lane, far), axis=1, keepdims=True)
    rest = jnp.where(lane == i1, -1.0, prob)
    p2 = jnp.max(rest, axis=1, keepdims=True)
    i2 = jnp.min(jnp.where(rest == p2, lane, far), axis=1, keepdims=True)
    den = p1 + p2
    w1 = g_w * p1 / den
    w2 = g_w * p2 / den
    e1 = (i1 - N_GROUPS).astype(F32)
    e2 = (i2 - N_GROUPS).astype(F32)
    out = jnp.where(lane == 0, e1, jnp.where(lane == 1, e2, jnp.where(lane == 2, w1, jnp.where(lane == 3, w2, 0.0))))
    return out


ROW_CHUNKS = D_MODEL // LANES


def _store_token_rows(ref, val, first, count, step):
    for c in range(ROW_CHUNKS):
        ref[pl.ds(first * ROW_CHUNKS + c, count, stride=step * ROW_CHUNKS), :] = val[:, c * LANES:(c + 1) * LANES]


def _load_token_rows(ref, count):
    return jnp.concatenate([ref[pl.ds(c, count, stride=ROW_CHUNKS), :] for c in range(ROW_CHUNKS)], axis=1)


def _mix_tail(o_mla, o_lru, x, gmla_ref, glru_ref, wo_ref, gffn_ref, wr_ref, br_ref):
    mixed = jnp.concatenate([_rms(o_mla, gmla_ref[...]), _rms(o_lru, glru_ref[...])], axis=1).astype(BF16)
    h1 = x + _dot(mixed, wo_ref[...])
    t = _rms(h1, gffn_ref[...])
    logits = _dot(t.astype(BF16), wr_ref[...]) + br_ref[...]
    return h1, t, _route(logits)


def _mix_prompt_kernel(xbr_ref, ybr_ref, omla_ref, x_ref, wconv_ref, bconv_ref, wrg_ref, brg_ref, wig_ref, big_ref,
                       lam_ref, gmla_ref, glru_ref, wo_ref, gffn_ref, wr_ref, br_ref,
                       h1_ref, t_ref, route_ref, lru_ref, hc_ref, tail_ref, olru_ref, *, tc):
    @pl.when(pl.program_id(1) == 0)
    def _():
        hc_ref[...] = jnp.zeros(hc_ref.shape, F32)
        tail_ref[...] = jnp.zeros(tail_ref.shape, F32)

    sub8 = lax.broadcasted_iota(jnp.int32, (tc // 8, 8, LRU_BLOCK), 1)
    row = lax.broadcasted_iota(jnp.int32, (tc, LRU_BLOCK), 0)
    for n in range(LRU_BLOCKS):
        cs = slice(n * LRU_BLOCK, (n + 1) * LRU_BLOCK)
        xb = xbr_ref[:, cs]
        tail = tail_ref[:, cs]
        xc = bconv_ref[:, cs] + xb * wconv_ref[CONV_W - 1:CONV_W, cs]
        x3 = jnp.concatenate([tail, xb], axis=0).reshape(tc // 8 + 1, 8, LRU_BLOCK)
        for j in range(1, CONV_W):
            rot = pltpu.roll(x3, j, axis=1)
            sh = jnp.where(sub8 < j, rot[:-1], rot[1:]).reshape(tc, LRU_BLOCK)
            xc = xc + sh * wconv_ref[CONV_W - 1 - j:CONV_W - j, cs]
        tail_ref[:, cs] = xb[tc - 8:tc]

        a, u = _gate_block(xc, n, wrg_ref, brg_ref, wig_ref, big_ref, lam_ref)
        d = 1
        while d < tc:
            keep = row >= d
            u = jnp.where(keep, a * pltpu.roll(u, d, axis=0) + u, u)
            a = jnp.where(keep, a * pltpu.roll(a, d, axis=0), a)
            d *= 2
        hs = a * hc_ref[:, cs] + u
        hc_ref[:, cs] = hs[tc - 1:tc]
        lru_ref[:, cs] = hs[tc - 1:tc]
        olru_ref[:, cs] = _gelu(ybr_ref[:, cs]) * hs

    h1, t, route = _mix_tail(omla_ref[...], olru_ref[...], x_ref[...], gmla_ref, glru_ref, wo_ref, gffn_ref, wr_ref, br_ref)
    h1_ref[...] = h1
    _store_token_rows(t_ref, t, 0, tc, 1)
    route_ref[...] = route


def _mix_weight_specs():
    return [_const_spec((CONV_W, LRU_WIDTH)), _const_spec((1, LRU_WIDTH)),
            _const_spec((LRU_BLOCKS, LRU_BLOCK, LRU_BLOCK)), _const_spec((1, LRU_WIDTH)),
            _const_spec((LRU_BLOCKS, LRU_BLOCK, LRU_BLOCK)), _const_spec((1, LRU_WIDTH)),
            _const_spec((1, LRU_WIDTH)), _const_spec((1, MLA_WIDTH)), _const_spec((1, LRU_WIDTH)),
            _const_spec((D_MODEL, D_MODEL)), _const_spec((1, D_MODEL)),
            _const_spec((D_MODEL, ROUTE_LANES)), _const_spec((1, ROUTE_LANES))]


def _mix_weights(w):
    return (w["w_conv"], w["b_conv"], w["w_rg"], w["b_rg"], w["w_ig"], w["b_ig"], w["lru_lambda"],
            w["g_out_mla"], w["g_out_lru"], w["w_o"], w["g_ffn"], w["w_route"], w["b_route"])


def _mix_prompt(xbr, ybr, omla, x2d, w, batch, seq, tc):
    nt = seq // tc
    n = batch * seq
    row = lambda width: pl.BlockSpec((tc, width), lambda b, i: (b * nt + i, 0))
    return pl.pallas_call(
        functools.partial(_mix_prompt_kernel, tc=tc),
        out_shape=(jax.ShapeDtypeStruct((n, D_MODEL), F32), jax.ShapeDtypeStruct((n * ROW_CHUNKS, LANES), F32),
                   jax.ShapeDtypeStruct((n, ROUTE_LANES), F32), jax.ShapeDtypeStruct((batch, 1, LRU_WIDTH), F32)),
        grid=(batch, nt),
        in_specs=[row(LRU_WIDTH), row(LRU_WIDTH), row(MLA_WIDTH), row(D_MODEL)] + _mix_weight_specs(),
        out_specs=(row(D_MODEL), pl.BlockSpec((tc * ROW_CHUNKS, LANES), lambda b, i: (b * nt + i, 0)), row(ROUTE_LANES),
                   pl.BlockSpec((None, 1, LRU_WIDTH), lambda b, i: (b, 0, 0))),
        scratch_shapes=[pltpu.VMEM((1, LRU_WIDTH), F32), pltpu.VMEM((8, LRU_WIDTH), F32),
                        pltpu.VMEM((tc, LRU_WIDTH), F32)],
        compiler_params=_cparams(dimension_semantics=("arbitrary", "arbitrary")),
        name="mix_prompt",
    )(xbr, ybr, omla, x2d, *_mix_weights(w))


def _mix_sample_kernel(xbr_ref, ybr_ref, omla_ref, x_ref, sconv_ref, slru_ref, wconv_ref, bconv_ref, wrg_ref, brg_ref,
                       wig_ref, big_ref, lam_ref, gmla_ref, glru_ref, wo_ref, gffn_ref, wr_ref, br_ref,
                       h1_ref, t_ref, route_ref, lru_ref, *, t_new):
    w_ = LRU_WIDTH
    xs = [sconv_ref[:, k * w_:(k + 1) * w_] for k in range(CONV_W - 1)] + [xbr_ref[:, t * w_:(t + 1) * w_] for t in range(t_new)]
    xc = []
    for t in range(t_new):
        acc = bconv_ref[...] + xs[t] * wconv_ref[0:1, :]
        for k in range(1, CONV_W):
            acc = acc + xs[t + k] * wconv_ref[k:k + 1, :]
        xc.append(acc)
    xc = jnp.concatenate(xc, axis=0)
    a, u = _gates(xc, wrg_ref, brg_ref, wig_ref, big_ref, lam_ref)
    nb = xbr_ref.shape[0]
    h = slru_ref[...]
    hs = []
    for t in range(t_new):
        h = a[t * nb:(t + 1) * nb] * h + u[t * nb:(t + 1) * nb]
        hs.append(h)
    lru_ref[...] = h
    hs = jnp.concatenate(hs, axis=0)
    stack = lambda ref, width: jnp.concatenate([ref[:, t * width:(t + 1) * width] for t in range(t_new)], axis=0)
    o_lru = _gelu(stack(ybr_ref, w_)) * hs
    h1, tt, route = _mix_tail(stack(omla_ref, MLA_WIDTH), o_lru, stack(x_ref, D_MODEL),
                              gmla_ref, glru_ref, wo_ref, gffn_ref, wr_ref, br_ref)
    for t in range(t_new):
        h1_ref[:, t * D_MODEL:(t + 1) * D_MODEL] = h1[t * nb:(t + 1) * nb]
        _store_token_rows(t_ref, tt[t * nb:(t + 1) * nb], t, nb, t_new)
        route_ref[:, t * ROUTE_LANES:(t + 1) * ROUTE_LANES] = route[t * nb:(t + 1) * nb]


def _mix_sample(xbr, ybr, omla, x2d, sconv, slru, w, t_new):
    nb = slru.shape[0]
    full = lambda a: pl.BlockSpec(a.shape, lambda i: (0, 0))
    ins = (xbr, ybr, omla, x2d, sconv, slru)
    return pl.pallas_call(
        functools.partial(_mix_sample_kernel, t_new=t_new),
        out_shape=(jax.ShapeDtypeStruct((nb, t_new * D_MODEL), F32),
                   jax.ShapeDtypeStruct((nb * t_new * ROW_CHUNKS, LANES), F32),
                   jax.ShapeDtypeStruct((nb, t_new * ROUTE_LANES), F32), jax.ShapeDtypeStruct((nb, LRU_WIDTH), F32)),
        grid=(1,),
        in_specs=[full(a) for a in ins] + _mix_weight_specs(),
        out_specs=(pl.BlockSpec((nb, t_new * D_MODEL), lambda i: (0, 0)),
                   pl.BlockSpec((nb * t_new * ROW_CHUNKS, LANES), lambda i: (0, 0)),
                   pl.BlockSpec((nb, t_new * ROUTE_LANES), lambda i: (0, 0)), pl.BlockSpec((nb, LRU_WIDTH), lambda i: (0, 0))),
        compiler_params=_cparams(dimension_semantics=("arbitrary",)),
        name="mix_sample",
    )(*ins, *_mix_weights(w))


def _moe_kernel(te_ref, nu_ref, tok_ref, pair_ref, t_hbm, wg_ref, wu_ref, wd_ref, y_hbm, xa, xb, oa, ob, gsem, ssem, *, tm):
    del te_ref
    j = pl.program_id(0)
    n_used = nu_ref[0]
    xbufs, obufs = (xa, xb), (oa, ob)
    rc = ROW_CHUNKS
    spare = y_hbm.shape[0] - 2 * tm * rc

    def start_gather(jj, k):
        for r in range(tm):
            src = pl.multiple_of(tok_ref[jj * tm + r], rc)
            pltpu.make_async_copy(t_hbm.at[pl.ds(src, rc)], xbufs[k].at[pl.ds(r * rc, rc)], gsem.at[k]).start()

    def wait_gather(k):
        pltpu.make_async_copy(t_hbm.at[pl.ds(0, tm * rc)], xbufs[k], gsem.at[k]).wait()

    def start_scatter(jj, k):
        for r in range(tm):
            dst = pl.multiple_of(pair_ref[(jj + 1) * tm + r], rc)
            pltpu.make_async_copy(obufs[k].at[pl.ds(r * rc, rc)], y_hbm.at[pl.ds(dst, rc)], ssem.at[k]).start()

    def wait_scatter(k):
        pltpu.make_async_copy(obufs[k], y_hbm.at[pl.ds(0, tm * rc)], ssem.at[k]).wait()

    @pl.when(j == 0)
    def _():
        oa[...] = jnp.zeros(oa.shape, F32)
        ob[...] = jnp.zeros(ob.shape, F32)
        pltpu.make_async_copy(oa, y_hbm.at[pl.ds(spare, tm * rc)], ssem.at[0]).start()
        start_gather(0, 0)

    def tile(k):
        wait_gather(k)
        wait_scatter(k)
        start_gather(jnp.minimum(j + 1, n_used - 1), 1 - k)
        start_scatter(j - 1, 1 - k)
        x = _load_token_rows(xbufs[k], tm).astype(BF16)
        hg = _dot(x, wg_ref[...].astype(BF16))
        hu = _dot(x, wu_ref[...].astype(BF16))
        act = (hg * jax.nn.sigmoid(hg) * hu).astype(BF16)
        _store_token_rows(obufs[k], _dot(act, wd_ref[...].astype(BF16)), 0, tm, 1)

    def drain(k):
        start_scatter(j, k)
        wait_gather(1 - k)
        wait_scatter(1 - k)
        wait_scatter(k)

    for k in range(2):
        pl.when(jnp.logical_and(j < n_used, j % 2 == k))(functools.partial(tile, k))
        pl.when(jnp.logical_and(j == n_used - 1, j % 2 == k))(functools.partial(drain, k))


def _moe(t_all, tile_expert, n_used, tok_of_slot, pair_of_slot, w_gate, w_up, w_down, tm):
    n_tiles = tile_expert.shape[0]
    n_tok = t_all.shape[0] // ROW_CHUNKS
    wspec = lambda a, b: pl.BlockSpec((None, a, b), lambda j, te, nu, tk, pr: (te[j], 0, 0))
    grid_spec = pltpu.PrefetchScalarGridSpec(
        num_scalar_prefetch=4, grid=(n_tiles,),
        in_specs=[pl.BlockSpec(memory_space=pl.ANY), wspec(D_MODEL, D_EXPERT), wspec(D_MODEL, D_EXPERT),
                  wspec(D_EXPERT, D_MODEL)],
        out_specs=pl.BlockSpec(memory_space=pl.ANY),
        scratch_shapes=([pltpu.VMEM((tm * ROW_CHUNKS, LANES), F32)] * 4
                        + [pltpu.SemaphoreType.DMA((2,)), pltpu.SemaphoreType.DMA((2,))]))
    return pl.pallas_call(
        functools.partial(_moe_kernel, tm=tm),
        out_shape=jax.ShapeDtypeStruct(((2 * n_tok + 2 * tm) * ROW_CHUNKS, LANES), F32),
        grid_spec=grid_spec,
        compiler_params=_cparams(dimension_semantics=("arbitrary",)),
        name="moe",
    )(tile_expert, n_used, tok_of_slot, pair_of_slot, t_all, w_gate, w_up, w_down)


def _moe_plan(route_all, tm):
    n = route_all.shape[0]
    n_pairs = 2 * n
    eflat = jnp.concatenate([route_all[:, 0], route_all[:, 1]]).astype(jnp.int32)
    onehot = (eflat[:, None] == jnp.arange(N_EXPERTS, dtype=jnp.int32)[None, :]).astype(jnp.int32)
    csum = jnp.cumsum(onehot, axis=0)
    rank = jnp.take_along_axis(csum, eflat[:, None], axis=1)[:, 0] - 1
    counts = csum[-1]
    tiles_e = (counts + tm - 1) // tm
    tile_end = jnp.cumsum(tiles_e)
    tile_start = tile_end - tiles_e
    n_used = tile_end[-1]
    n_tiles = n_pairs // tm + N_EXPERTS
    slot = tile_start[eflat] * tm + rank
    tid = jnp.arange(n_tiles, dtype=jnp.int32)
    te = jnp.minimum(jnp.sum((tile_end[None, :] <= tid[:, None]).astype(jnp.int32), axis=1), N_EXPERTS - 1)
    te = jnp.where(tid < n_used, te, te[jnp.maximum(n_used - 1, 0)])
    sid = jnp.arange(n_tiles * tm, dtype=jnp.int32)
    pad_pair = n_pairs + sid % (2 * tm)
    pair_of_slot = pad_pair.at[slot].set(jnp.arange(n_pairs, dtype=jnp.int32))
    tok_of_slot = jnp.where(pair_of_slot >= n_pairs, pair_of_slot - n_pairs,
                            jnp.where(pair_of_slot >= n, pair_of_slot - n, pair_of_slot))
    pair_shifted = jnp.concatenate([n_pairs + tm + jnp.arange(tm, dtype=jnp.int32), pair_of_slot])
    return te, n_used.reshape(1).astype(jnp.int32), tok_of_slot * ROW_CHUNKS, pair_shifted * ROW_CHUNKS


def _final_kernel(h1_ref, y0_ref, y1_ref, route_ref, p_ref, gple_ref, wpg_ref, bpg_ref, wpp_ref, gfin_ref, y_ref):
    route = route_ref[...]
    tm = h1_ref.shape[0]
    h2 = h1_ref[...] + (route[:, 2:3] * _load_token_rows(y0_ref, tm) + route[:, 3:4] * _load_token_rows(y1_ref, tm))
    gate = jax.nn.sigmoid(_dot(_rms(h2, gple_ref[...]).astype(BF16), wpg_ref[...]) + bpg_ref[...])
    h3 = h2 + gate * _dot(p_ref[...].astype(BF16), wpp_ref[...])
    y_ref[...] = _rms(h3, gfin_ref[...])


def _final(h1, y_pairs, route, p2d, w, blk0, n_all, tm):
    n = h1.shape[0]
    nb_all = n_all // tm
    row = lambda width: pl.BlockSpec((tm, width), lambda i: (i, 0))
    return pl.pallas_call(
        _final_kernel,
        out_shape=jax.ShapeDtypeStruct((n, D_MODEL), F32),
        grid=(n // tm,),
        in_specs=[row(D_MODEL),
                  pl.BlockSpec((tm * ROW_CHUNKS, LANES), lambda i: (blk0 + i, 0)),
                  pl.BlockSpec((tm * ROW_CHUNKS, LANES), lambda i: (nb_all + blk0 + i, 0)),
                  row(ROUTE_LANES), row(PLE_DIM), _const_spec((1, D_MODEL)), _const_spec((D_MODEL, D_MODEL)),
                  _const_spec((1, D_MODEL)), _const_spec((PLE_DIM, D_MODEL)), _const_spec((1, D_MODEL))],
        out_specs=row(D_MODEL),
        compiler_params=_cparams(dimension_semantics=("arbitrary",)),
        name="final",
    )(h1, y_pairs, y_pairs, route, p2d, w["g_ple"], w["w_ple_gate"], w["b_ple_gate"], w["w_ple_proj"], w["g_final"])


def _rope_tables(pos):
    half = QK_ROPE // 2
    inv = ROPE_THETA ** (-jnp.arange(half, dtype=F32) / half)
    ang = pos.astype(F32)[:, None] * inv[None, :]
    pad = jnp.zeros((pos.shape[0], ROPE_PAD - QK_ROPE), F32)
    cos = jnp.concatenate([jnp.cos(ang), jnp.cos(ang), pad], axis=1)
    sin = jnp.concatenate([jnp.sin(ang), jnp.sin(ang), pad], axis=1)
    return cos, sin


def _rot_cols(wr):
    half = QK_ROPE // 2
    return jnp.concatenate([-wr[..., half:], wr[..., :half]], axis=-1)


def _pad_cols(wr):
    return jnp.concatenate([wr, jnp.zeros(wr.shape[:-1] + (ROPE_PAD - QK_ROPE,), wr.dtype)], axis=-1)


def _prep_weights(g_mix, w_in, g_q, w_uq, g_kv, w_ukv, w_conv, b_conv, w_rg, b_rg, w_ig, b_ig, lru_lambda,
                  g_out_mla, g_out_lru, w_o, g_ffn, w_group, b_group, w_router, b_router, g_ple, w_ple_gate,
                  b_ple_gate, w_ple_proj, g_final):
    row = lambda v: v.reshape(1, -1).astype(F32)
    s0, s1, s2, s3 = Q_LORA, Q_LORA + KV_LORA, Q_LORA + KV_LORA + QK_ROPE, Q_LORA + KV_LORA + QK_ROPE + LRU_WIDTH
    kr = w_in[:, s1:s2]
    w_in_ext = jnp.concatenate([w_in[:, :s1], _pad_cols(kr), _pad_cols(_rot_cols(kr)), w_in[:, s2:]], axis=1)
    uq = w_uq.reshape(Q_LORA, MLA_HEADS, QK_NOPE + QK_ROPE)
    uq_r = uq[..., QK_NOPE:]
    w_uq_ext = jnp.concatenate([uq[..., :QK_NOPE].reshape(Q_LORA, -1), _pad_cols(uq_r).reshape(Q_LORA, -1),
                                _pad_cols(_rot_cols(uq_r)).reshape(Q_LORA, -1)], axis=1)
    w_uk = jnp.transpose(w_ukv[..., :QK_NOPE], (1, 2, 0))
    w_uv = jnp.transpose(w_ukv[..., QK_NOPE:], (1, 0, 2))
    w_route = jnp.concatenate([w_group, w_router, jnp.zeros((D_MODEL, ROUTE_LANES - N_GROUPS - N_EXPERTS), F32)], axis=1)
    b_route = jnp.concatenate([b_group, b_router, jnp.zeros((ROUTE_LANES - N_GROUPS - N_EXPERTS,), F32)])
    return dict(
        g_mix=row(g_mix), w_in=w_in_ext.astype(BF16), g_q=row(g_q), w_uq=w_uq_ext.astype(BF16), g_kv=row(g_kv),
        w_uk=w_uk.astype(BF16), w_uv=w_uv.astype(BF16), w_conv=w_conv.astype(F32), b_conv=row(b_conv),
        w_rg=w_rg.astype(BF16), b_rg=row(b_rg), w_ig=w_ig.astype(BF16), b_ig=row(b_ig), lru_lambda=row(lru_lambda),
        g_out_mla=row(g_out_mla), g_out_lru=row(g_out_lru), w_o=w_o.astype(BF16), g_ffn=row(g_ffn),
        w_route=w_route.astype(BF16), b_route=row(b_route), g_ple=row(g_ple), w_ple_gate=w_ple_gate.astype(BF16),
        b_ple_gate=row(b_ple_gate), w_ple_proj=w_ple_proj.astype(BF16), g_final=row(g_final))


TM_PROJ = 512
TM_FINAL = 256
TQ_ATTN = 256
ATTN_HEAD_GROUP = 4
TC_MIX = 256
TM_MOE = 256
PAGES_PER_STEP = 32
SAMPLE_BUFFERS = 4
SAMPLE_SUB_BLOCKS = 4


def kernel(x_prompt, x_sample, p_prompt, p_sample, cache_latent, cache_krope, state_lru, state_conv, page_table, g_mix, w_in, g_q, w_uq, g_kv, w_ukv, w_conv, b_conv, w_rg, b_rg, w_ig, b_ig, lru_lambda, g_out_mla, g_out_lru, w_o, g_ffn, w_group, b_group, w_router, b_router, w_gate, w_up, w_down, g_ple, w_ple_gate, b_ple_gate, w_ple_proj, g_final):
    assert w_in.shape[0] == 1, "single trunk layer"
    batch, seq, _ = x_prompt.shape
    db, t_new, _ = x_sample.shape
    page = cache_latent.shape[2]
    past_len = page_table.shape[1] * page
    n_p, n_s = batch * seq, db * t_new
    n_all = n_p + n_s

    w = _prep_weights(g_mix[0], w_in[0], g_q[0], w_uq[0], g_kv[0], w_ukv[0], w_conv[0], b_conv[0], w_rg[0], b_rg[0],
                      w_ig[0], b_ig[0], lru_lambda[0], g_out_mla[0], g_out_lru[0], w_o[0], g_ffn[0], w_group[0],
                      b_group[0], w_router[0], b_router[0], g_ple[0], w_ple_gate[0], b_ple_gate[0], w_ple_proj[0],
                      g_final)

    xp = x_prompt.reshape(n_p, D_MODEL)
    cos_p, sin_p = _rope_tables(jnp.arange(seq))
    qcat_p, kcat_p, lat_p, krope_p, xbr_p, ybr_p = _proj(xp, cos_p, sin_p, seq // TM_PROJ, w, TM_PROJ)
    omla_p = _prompt_attn(qcat_p, kcat_p, w["w_uv"], batch, seq, TQ_ATTN)
    h1_p, t_p, route_p, lru_p = _mix_prompt(xbr_p, ybr_p, omla_p, xp, w, batch, seq, TC_MIX)

    xs = x_sample.reshape(n_s, D_MODEL)
    tm_s = min(TM_PROJ, n_s)
    cos_s, sin_s = _rope_tables(past_len + (jnp.arange(tm_s) % t_new))
    qcat_s, kcat_s, lat_s, krope_s, xbr_s, ybr_s = _proj(xs, cos_s, sin_s, 1, w, tm_s)
    q_s = qcat_s.reshape(db, t_new, MLA_HEADS, QK_CAT).transpose(0, 2, 1, 3).reshape(db, MLA_HEADS * t_new, QK_CAT)
    knew = jnp.pad(kcat_s.reshape(db, t_new, QK_CAT), ((0, 0), (0, 8 - t_new), (0, 0)))
    cache_ropet = jnp.swapaxes(cache_krope[0], 1, 2)
    omla_s = _sample_attn(page_table, q_s, knew, cache_latent[0], cache_ropet, w["w_uv"], PAGES_PER_STEP)
    h1_s, t_s, route_s, lru_s = _mix_sample(
        xbr_s.reshape(db, t_new * LRU_WIDTH), ybr_s.reshape(db, t_new * LRU_WIDTH),
        omla_s.reshape(db, t_new * MLA_WIDTH), x_sample.reshape(db, t_new * D_MODEL),
        state_conv[0].reshape(db, (CONV_W - 1) * LRU_WIDTH), state_lru[0], w, t_new)
    h1_s = h1_s.reshape(n_s, D_MODEL)
    route_s = route_s.reshape(n_s, ROUTE_LANES)

    t_all = jnp.concatenate([t_p, t_s], axis=0)
    route_all = jnp.concatenate([route_p, route_s], axis=0)
    te, n_used, tok_of_slot, pair_of_slot = _moe_plan(route_all, TM_MOE)
    y_pairs = _moe(t_all, te, n_used, tok_of_slot, pair_of_slot, w_gate[0], w_up[0], w_down[0], TM_MOE)

    y_p = _final(h1_p, y_pairs, route_p, p_prompt[0].reshape(n_p, PLE_DIM), w, 0, n_all, TM_FINAL)
    y_s = _final(h1_s, y_pairs, route_s, p_sample[0].reshape(n_s, PLE_DIM), w, n_p // TM_FINAL, n_all, TM_FINAL)

    new_conv_p = xbr_p.reshape(batch, seq, LRU_WIDTH)[:, seq - (CONV_W - 1):]
    hist = jnp.concatenate([state_conv[0], xbr_s.reshape(db, t_new, LRU_WIDTH)], axis=1)
    new_conv_s = hist[:, hist.shape[1] - (CONV_W - 1):]
    return (y_p.reshape(batch, seq, D_MODEL), y_s.reshape(db, t_new, D_MODEL),
            lat_p.reshape(1, batch, seq, KV_LORA), krope_p.reshape(1, batch, seq, QK_ROPE),
            lru_p.reshape(1, batch, LRU_WIDTH), new_conv_p[None],
            lat_s.reshape(1, db, t_new, KV_LORA), krope_s.reshape(1, db, t_new, QK_ROPE),
            lru_s[None], new_conv_s[None])
```

```python
import functools

import jax
import jax.numpy as jnp
from jax import lax
from jax.experimental import pallas as pl
from jax.experimental.pallas import tpu as pltpu

F32 = jnp.float32
BF16 = jnp.bfloat16

D_MODEL = 2048
MLA_HEADS = 8
V_HEAD = 128
MLA_WIDTH = MLA_HEADS * V_HEAD
LRU_WIDTH = D_MODEL - MLA_WIDTH
QK_NOPE = 128
QK_ROPE = 64
Q_LORA = 512
KV_LORA = 256
ROPE_THETA = 10000.0
SM_SCALE = (QK_NOPE + QK_ROPE) ** -0.5
NEG_INF = -1e30
LRU_BLOCKS = 8
LRU_BLOCK = LRU_WIDTH // LRU_BLOCKS
CONV_W = 4
LRU_C = 8.0
N_GROUPS = 4
EXPERTS_PER_GROUP = 8
N_EXPERTS = N_GROUPS * EXPERTS_PER_GROUP
D_EXPERT = 512
PLE_DIM = 256
EPS = 1e-6

LANES = 128
ROPE_PAD = LANES
QK_CAT = KV_LORA + ROPE_PAD
C_Q0, C_KV0, C_KRA0, C_KRB0, C_X0, C_Y0, IN_EXT = 0, 512, 768, 896, 1024, 2048, 3072
ROUTE_LANES = LANES
VMEM_LIMIT = 56 * 1024 * 1024


def _cparams(**kw):
    return pltpu.CompilerParams(vmem_limit_bytes=VMEM_LIMIT, **kw)


def _rms(x, g):
    return x * lax.rsqrt(jnp.mean(x * x, axis=-1, keepdims=True) + EPS) * g


def _dot(a, b):
    return jnp.dot(a, b, preferred_element_type=F32)


def _dot_nt(a, b):
    return lax.dot_general(a, b, (((1,), (1,)), ((), ())), preferred_element_type=F32)


def _const_spec(shape):
    nd = len(shape)
    return pl.BlockSpec(shape, lambda *_: (0,) * nd, pipeline_mode=pl.Buffered(1))


def _proj_kernel(x_ref, cos_ref, sin_ref, gmix_ref, win_ref, gq_ref, wuq_ref, gkv_ref, wuk_ref,
                 qcat_ref, kcat_ref, lat_ref, krope_ref, xbr_ref, ybr_ref):
    u = _rms(x_ref[...], gmix_ref[...]).astype(BF16)
    z = _dot(u, win_ref[...])
    xbr_ref[...] = z[:, C_X0:C_Y0]
    ybr_ref[...] = z[:, C_Y0:IN_EXT]
    cos = cos_ref[...]
    sin = sin_ref[...]
    lat = _rms(z[:, C_KV0:C_KRA0], gkv_ref[...])
    kr = z[:, C_KRA0:C_KRB0] * cos + z[:, C_KRB0:C_X0] * sin
    lat_ref[...] = lat
    krope_ref[...] = kr[:, :QK_ROPE]
    kcat_ref[:, 0:KV_LORA] = lat.astype(BF16)
    kcat_ref[:, KV_LORA:QK_CAT] = kr.astype(BF16)
    qn = _rms(z[:, C_Q0:C_KV0], gq_ref[...]).astype(BF16)
    q = _dot(qn, wuq_ref[...])
    ra0 = MLA_HEADS * QK_NOPE
    rb0 = ra0 + MLA_HEADS * ROPE_PAD
    for h in range(MLA_HEADS):
        ql = _dot(q[:, h * QK_NOPE:(h + 1) * QK_NOPE].astype(BF16), wuk_ref[h])
        qr = (q[:, ra0 + h * ROPE_PAD:ra0 + (h + 1) * ROPE_PAD] * cos
              + q[:, rb0 + h * ROPE_PAD:rb0 + (h + 1) * ROPE_PAD] * sin)
        qcat_ref[:, h * QK_CAT:h * QK_CAT + KV_LORA] = ql.astype(BF16)
        qcat_ref[:, h * QK_CAT + KV_LORA:(h + 1) * QK_CAT] = qr.astype(BF16)


def _proj(x2d, cos, sin, pos_blocks, w, tm):
    n = x2d.shape[0]
    row = lambda width: pl.BlockSpec((tm, width), lambda i: (i, 0))
    pos = pl.BlockSpec((tm, ROPE_PAD), lambda i: (i % pos_blocks, 0))
    out_shape = (
        jax.ShapeDtypeStruct((n, MLA_HEADS * QK_CAT), BF16),
        jax.ShapeDtypeStruct((n, QK_CAT), BF16),
        jax.ShapeDtypeStruct((n, KV_LORA), F32),
        jax.ShapeDtypeStruct((n, QK_ROPE), F32),
        jax.ShapeDtypeStruct((n, LRU_WIDTH), F32),
        jax.ShapeDtypeStruct((n, LRU_WIDTH), F32),
    )
    return pl.pallas_call(
        _proj_kernel,
        out_shape=out_shape,
        grid=(n // tm,),
        in_specs=[row(D_MODEL), pos, pos, _const_spec((1, D_MODEL)), _const_spec((D_MODEL, IN_EXT)),
                  _const_spec((1, Q_LORA)), _const_spec((Q_LORA, 3 * MLA_HEADS * LANES)),
                  _const_spec((1, KV_LORA)), _const_spec((MLA_HEADS, QK_NOPE, KV_LORA))],
        out_specs=(row(MLA_HEADS * QK_CAT), row(QK_CAT), row(KV_LORA), row(QK_ROPE), row(LRU_WIDTH), row(LRU_WIDTH)),
        compiler_params=_cparams(dimension_semantics=("arbitrary",)),
        name="proj",
    )(x2d, cos, sin, w["g_mix"], w["w_in"], w["g_q"], w["w_uq"], w["g_kv"], w["w_uk"])


def _prompt_attn_kernel(q_ref, k_ref, wuv_ref, o_ref, *scratch, tq, group):
    m_refs, l_refs, acc_refs = scratch[:MLA_HEADS], scratch[MLA_HEADS:2 * MLA_HEADS], scratch[2 * MLA_HEADS:]
    qi = pl.program_id(1)
    for h in range(MLA_HEADS):
        m_refs[h][...] = jnp.full(m_refs[h].shape, NEG_INF, F32)
        l_refs[h][...] = jnp.zeros(l_refs[h].shape, F32)
        acc_refs[h][...] = jnp.zeros(acc_refs[h].shape, F32)

    def step(kb, masked):
        k = k_ref[pl.ds(pl.multiple_of(kb * tq, tq), tq), :]
        v = k[:, :KV_LORA]
        if masked:
            row = lax.broadcasted_iota(jnp.int32, (tq, tq), 0)
            col = lax.broadcasted_iota(jnp.int32, (tq, tq), 1)
            keep = col <= row
        for h0 in range(0, MLA_HEADS, group):
            heads = range(h0, h0 + group)
            scores = [_dot_nt(q_ref[:, h * QK_CAT:(h + 1) * QK_CAT], k) * SM_SCALE for h in heads]
            probs = []
            for h, s in zip(heads, scores):
                if masked:
                    s = jnp.where(keep, s, NEG_INF)
                m_prev = m_refs[h][...]
                m_new = jnp.maximum(m_prev, jnp.max(s, axis=1, keepdims=True))
                alpha = jnp.exp(m_prev - m_new)
                p = jnp.exp(s - jnp.concatenate([m_new] * (tq // LANES), axis=1))
                l_refs[h][...] = alpha * l_refs[h][...] + jnp.sum(p, axis=1, keepdims=True)
                m_refs[h][...] = m_new
                probs.append((alpha, p.astype(BF16)))
            for h, (alpha, p) in zip(heads, probs):
                acc_refs[h][...] = jnp.concatenate([alpha] * (KV_LORA // LANES), axis=1) * acc_refs[h][...] + _dot(p, v)

    def body(kb, carry):
        step(kb, False)
        return carry

    lax.fori_loop(0, qi, body, 0)
    step(qi, True)
    for h in range(MLA_HEADS):
        o = acc_refs[h][...] / jnp.concatenate([l_refs[h][...]] * (KV_LORA // LANES), axis=1)
        o_ref[:, h * V_HEAD:(h + 1) * V_HEAD] = _dot(o.astype(BF16), wuv_ref[h])


def _prompt_attn(qcat, kcat, wuv, batch, seq, tq):
    nq = seq // tq
    return pl.pallas_call(
        functools.partial(_prompt_attn_kernel, tq=tq, group=ATTN_HEAD_GROUP),
        out_shape=jax.ShapeDtypeStruct((batch * seq, MLA_WIDTH), F32),
        grid=(batch, nq),
        in_specs=[pl.BlockSpec((tq, MLA_HEADS * QK_CAT), lambda b, i: (b * nq + i, 0)),
                  pl.BlockSpec((seq, QK_CAT), lambda b, i: (b, 0)),
                  _const_spec((MLA_HEADS, KV_LORA, V_HEAD))],
        out_specs=pl.BlockSpec((tq, MLA_WIDTH), lambda b, i: (b * nq + i, 0)),
        scratch_shapes=([pltpu.VMEM((tq, LANES), F32)] * (2 * MLA_HEADS) + [pltpu.VMEM((tq, KV_LORA), F32)] * MLA_HEADS),
        compiler_params=_cparams(dimension_semantics=("arbitrary", "arbitrary")),
        name="prompt_attn",
    )(qcat, kcat, wuv)


def _sample_attn_kernel(pt_ref, q_ref, knew_ref, lat_hbm, ropet_hbm, wuv_ref, o_ref, *scratch,
                        pages, page, n_groups, n_buf, t_new, n_sub):
    lat_bufs, rope_bufs = scratch[:n_buf], scratch[n_buf:2 * n_buf]
    sem, m_ref, l_ref, acc_ref = scratch[2 * n_buf:]
    c = pl.program_id(1)
    step = pl.program_id(0) * n_groups + c
    n_steps = pl.num_programs(0) * n_groups
    bufs = tuple(zip(lat_bufs, rope_bufs))

    def start_chunk(chunk, k):
        for g in range(pages):
            pid = pt_ref[chunk * pages + g]
            pltpu.make_async_copy(lat_hbm.at[pid], bufs[k][0].at[g], sem.at[0, k]).start()
            pltpu.make_async_copy(ropet_hbm.at[pid], bufs[k][1].at[g], sem.at[1, k]).start()

    def wait_chunk(k):
        pltpu.make_async_copy(lat_hbm.at[pl.ds(0, pages)], bufs[k][0], sem.at[0, k]).wait()
        pltpu.make_async_copy(ropet_hbm.at[pl.ds(0, pages)], bufs[k][1], sem.at[1, k]).wait()

    q = q_ref[0]
    q_lat = q[:, :KV_LORA]
    q_rope = q[:, KV_LORA:KV_LORA + QK_ROPE]
    ps = pages // n_sub

    def attend(k):
        latbuf, ropebuf = bufs[k]
        lats, scores, parts = [], [], []
        for sb in range(n_sub):
            lat = latbuf[sb * ps:(sb + 1) * ps].reshape(ps * page, KV_LORA).astype(BF16)
            s_rope = jnp.concatenate([_dot(q_rope, ropebuf[sb * ps + g].astype(BF16)) for g in range(ps)], axis=1)
            lats.append(lat)
            scores.append((_dot_nt(q_lat, lat) + s_rope) * SM_SCALE)
        for sb in range(n_sub):
            m_i = jnp.max(scores[sb], axis=1, keepdims=True)
            p = jnp.exp(scores[sb] - m_i)
            parts.append((m_i, jnp.sum(p, axis=1, keepdims=True), p.astype(BF16)))
        parts = [(m_i, l_i, _dot(p, lats[sb])) for sb, (m_i, l_i, p) in enumerate(parts)]
        m_prev = m_ref[...]
        m_new = m_prev
        for m_i, _, _ in parts:
            m_new = jnp.maximum(m_new, m_i)
        alpha = jnp.exp(m_prev - m_new)
        l = alpha * l_ref[...]
        acc = alpha * acc_ref[...]
        for m_i, l_i, o_i in parts:
            w_i = jnp.exp(m_i - m_new)
            l = l + w_i * l_i
            acc = acc + w_i * o_i
        l_ref[...] = l
        acc_ref[...] = acc
        m_ref[...] = m_new

    @pl.when(step == 0)
    def _():
        for k in range(n_buf):
            start_chunk(k, k)

    @pl.when(c == 0)
    def _():
        m_ref[...] = jnp.full(m_ref.shape, NEG_INF, F32)
        l_ref[...] = jnp.zeros(l_ref.shape, F32)
        acc_ref[...] = jnp.zeros(acc_ref.shape, F32)

    last_chunk = n_buf * n_steps - 1
    for k in range(n_buf):
        wait_chunk(k)
        attend(k)
        start_chunk(jnp.minimum(n_buf * (step + 1) + k, last_chunk), k)

    @pl.when(step == n_steps - 1)
    def _():
        for k in range(n_buf):
            wait_chunk(k)

    @pl.when(c == n_groups - 1)
    def _():
        qf = q.astype(F32)
        kn = knew_ref[0].astype(F32)
        tok = lax.broadcasted_iota(jnp.int32, (q.shape[0], 1), 0) & (t_new - 1)
        cols = []
        for j in range(t_new):
            sj = jnp.sum(qf * kn[j:j + 1, :], axis=1, keepdims=True) * SM_SCALE
            cols.append(jnp.where(tok >= j, sj, NEG_INF))
        m_prev = m_ref[...]
        m_new = m_prev
        for sj in cols:
            m_new = jnp.maximum(m_new, sj)
        alpha = jnp.exp(m_prev - m_new)
        l = alpha * l_ref[...]
        acc = alpha * acc_ref[...]
        for j, sj in enumerate(cols):
            pj = jnp.exp(sj - m_new)
            l = l + pj
            acc = acc + pj * kn[j:j + 1, :KV_LORA]
        o = (acc / l).astype(BF16)
        for h in range(MLA_HEADS):
            oh = _dot(o, wuv_ref[h])
            o_ref[0, :, h * V_HEAD:(h + 1) * V_HEAD] = oh[h * t_new:(h + 1) * t_new]


def _sample_attn(page_table, q_s, knew, cache_lat, cache_ropet, wuv, pages):
    db, n_pages = page_table.shape
    page = cache_lat.shape[1]
    n_buf = SAMPLE_BUFFERS
    n_groups = n_pages // (n_buf * pages)
    assert n_groups * n_buf * pages == n_pages
    rows = q_s.shape[1]
    t_new = rows // MLA_HEADS
    in_specs = [pl.BlockSpec((1, rows, QK_CAT), lambda b, c, pt: (b, 0, 0)),
                pl.BlockSpec((1, 8, QK_CAT), lambda b, c, pt: (b, 0, 0)),
                pl.BlockSpec(memory_space=pl.ANY), pl.BlockSpec(memory_space=pl.ANY),
                pl.BlockSpec((MLA_HEADS, KV_LORA, V_HEAD), lambda b, c, pt: (0, 0, 0))]
    grid_spec = pltpu.PrefetchScalarGridSpec(
        num_scalar_prefetch=1, grid=(db, n_groups), in_specs=in_specs,
        out_specs=pl.BlockSpec((1, t_new, MLA_WIDTH), lambda b, c, pt: (b, 0, 0)),
        scratch_shapes=([pltpu.VMEM((pages, page, KV_LORA), F32)] * n_buf
                        + [pltpu.VMEM((pages, QK_ROPE, page), F32)] * n_buf
                        + [pltpu.SemaphoreType.DMA((2, n_buf)), pltpu.VMEM((rows, 1), F32),
                           pltpu.VMEM((rows, 1), F32), pltpu.VMEM((rows, KV_LORA), F32)]))
    return pl.pallas_call(
        functools.partial(_sample_attn_kernel, pages=pages, page=page, n_groups=n_groups, n_buf=n_buf, t_new=t_new,
                          n_sub=SAMPLE_SUB_BLOCKS),
        out_shape=jax.ShapeDtypeStruct((db, t_new, MLA_WIDTH), F32),
        grid_spec=grid_spec,
        compiler_params=_cparams(dimension_semantics=("arbitrary", "arbitrary")),
        name="sample_attn",
    )(page_table.reshape(-1), q_s, knew, cache_lat, cache_ropet, wuv)


def _gate_block(xc, n, wrg_ref, brg_ref, wig_ref, big_ref, lam_ref):
    cs = slice(n * LRU_BLOCK, (n + 1) * LRU_BLOCK)
    xb = xc.astype(BF16)
    r = jax.nn.sigmoid(_dot(xb, wrg_ref[n]) + brg_ref[:, cs])
    i = jax.nn.sigmoid(_dot(xb, wig_ref[n]) + big_ref[:, cs])
    neg_lam = -lam_ref[:, cs]
    softplus = jnp.maximum(neg_lam, 0.0) + jnp.log(1.0 + jnp.exp(-jnp.abs(neg_lam)))
    log_a = -LRU_C * r * softplus
    a = jnp.exp(log_a)
    gap = 1.0 - a * a
    root = jnp.where(gap > 0.0, gap * lax.rsqrt(gap), 0.0)
    u = root * i * xc
    return a, u


def _gates(xc, wrg_ref, brg_ref, wig_ref, big_ref, lam_ref):
    blocks = [_gate_block(xc[:, n * LRU_BLOCK:(n + 1) * LRU_BLOCK], n, wrg_ref, brg_ref, wig_ref, big_ref, lam_ref)
              for n in range(LRU_BLOCKS)]
    return (jnp.concatenate([a for a, _ in blocks], axis=1), jnp.concatenate([u for _, u in blocks], axis=1))


def _gelu(y):
    return 0.5 * y * (1.0 + jnp.tanh(0.7978845608028654 * (y + 0.044715 * (y * y * y))))


def _route(logits, counts):
    lane = lax.broadcasted_iota(jnp.int32, logits.shape, 1)
    far = jnp.int32(4 * ROUTE_LANES)
    gmask = lane < N_GROUPS
    gl = jnp.where(gmask, logits, NEG_INF)
    gmax = jnp.max(gl, axis=1, keepdims=True)
    gidx = jnp.min(jnp.where(gl == gmax, lane, far), axis=1, keepdims=True)
    g_w = 1.0 / jnp.sum(jnp.where(gmask, jnp.exp(gl - gmax), 0.0), axis=1, keepdims=True)
    lo = N_GROUPS + gidx * EXPERTS_PER_GROUP
    emask = jnp.logical_and(lane >= lo, lane < lo + EXPERTS_PER_GROUP)
    el = jnp.where(emask, logits, NEG_INF)
    emax = jnp.max(el, axis=1, keepdims=True)
    ex = jnp.where(emask, jnp.exp(el - emax), 0.0)
    prob = jnp.where(emask, ex / jnp.sum(ex, axis=1, keepdims=True), -1.0)
    p1 = jnp.max(prob, axis=1, keepdims=True)
    i1 = jnp.min(jnp.where(prob == p1, lane, far), axis=1, keepdims=True)
    rest = jnp.where(lane == i1, -1.0, prob)
    p2 = jnp.max(rest, axis=1, keepdims=True)
    i2 = jnp.min(jnp.where(rest == p2, lane, far), axis=1, keepdims=True)
    den = p1 + p2
    w1 = g_w * p1 / den
    w2 = g_w * p2 / den
    e1 = (i1 - N_GROUPS).astype(F32)
    e2 = (i2 - N_GROUPS).astype(F32)
    n = logits.shape[0]
    hit1 = lane == i1
    hit2 = lane == i2
    chosen = jnp.where(hit1, 1.0, jnp.where(hit2, 1.0, 0.0))
    earlier = jnp.where(lax.broadcasted_iota(jnp.int32, (n, n), 1) < lax.broadcasted_iota(jnp.int32, (n, n), 0), 1.0, 0.0)
    before = _dot(earlier.astype(BF16), chosen.astype(BF16)) + counts
    r1 = jnp.sum(jnp.where(hit1, before, 0.0), axis=1, keepdims=True)
    r2 = jnp.sum(jnp.where(hit2, before, 0.0), axis=1, keepdims=True)
    out = jnp.where(lane == 0, e1, jnp.where(lane == 1, e2, jnp.where(lane == 2, w1, jnp.where(lane == 3, w2,
          jnp.where(lane == 4, r1, jnp.where(lane == 5, r2, 0.0))))))
    return out, counts + jnp.sum(chosen, axis=0, keepdims=True)


ROW_CHUNKS = D_MODEL // LANES


def _store_token_rows(ref, val, first, count, step):
    for c in range(ROW_CHUNKS):
        ref[pl.ds(first * ROW_CHUNKS + c, count, stride=step * ROW_CHUNKS), :] = val[:, c * LANES:(c + 1) * LANES]


def _load_token_rows(ref, count):
    return jnp.concatenate([ref[pl.ds(c, count, stride=ROW_CHUNKS), :] for c in range(ROW_CHUNKS)], axis=1)


def _mix_tail(o_mla, o_lru, x, counts, gmla_ref, glru_ref, wo_ref, gffn_ref, wr_ref, br_ref):
    mixed = jnp.concatenate([_rms(o_mla, gmla_ref[...]), _rms(o_lru, glru_ref[...])], axis=1).astype(BF16)
    h1 = x + _dot(mixed, wo_ref[...])
    t = _rms(h1, gffn_ref[...])
    logits = _dot(t.astype(BF16), wr_ref[...]) + br_ref[...]
    route, counts = _route(logits, counts)
    return h1, t, route, counts


def _mix_prompt_kernel(xbr_ref, ybr_ref, omla_ref, x_ref, wconv_ref, bconv_ref, wrg_ref, brg_ref, wig_ref, big_ref,
                       lam_ref, gmla_ref, glru_ref, wo_ref, gffn_ref, wr_ref, br_ref,
                       h1_ref, t_ref, route_ref, lru_ref, cnt_ref, hc_ref, tail_ref, olru_ref, *, tc):
    @pl.when(pl.program_id(1) == 0)
    def _():
        hc_ref[...] = jnp.zeros(hc_ref.shape, F32)
        tail_ref[...] = jnp.zeros(tail_ref.shape, F32)

    @pl.when(jnp.logical_and(pl.program_id(0) == 0, pl.program_id(1) == 0))
    def _():
        cnt_ref[...] = jnp.zeros(cnt_ref.shape, F32)

    sub8 = lax.broadcasted_iota(jnp.int32, (tc // 8, 8, LRU_BLOCK), 1)
    row = lax.broadcasted_iota(jnp.int32, (tc, LRU_BLOCK), 0)
    for n in range(LRU_BLOCKS):
        cs = slice(n * LRU_BLOCK, (n + 1) * LRU_BLOCK)
        xb = xbr_ref[:, cs]
        tail = tail_ref[:, cs]
        xc = bconv_ref[:, cs] + xb * wconv_ref[CONV_W - 1:CONV_W, cs]
        x3 = jnp.concatenate([tail, xb], axis=0).reshape(tc // 8 + 1, 8, LRU_BLOCK)
        for j in range(1, CONV_W):
            rot = pltpu.roll(x3, j, axis=1)
            sh = jnp.where(sub8 < j, rot[:-1], rot[1:]).reshape(tc, LRU_BLOCK)
            xc = xc + sh * wconv_ref[CONV_W - 1 - j:CONV_W - j, cs]
        tail_ref[:, cs] = xb[tc - 8:tc]

        a, u = _gate_block(xc, n, wrg_ref, brg_ref, wig_ref, big_ref, lam_ref)
        d = 1
        while d < tc:
            keep = row >= d
            u = jnp.where(keep, a * pltpu.roll(u, d, axis=0) + u, u)
            a = jnp.where(keep, a * pltpu.roll(a, d, axis=0), a)
            d *= 2
        hs = a * hc_ref[:, cs] + u
        hc_ref[:, cs] = hs[tc - 1:tc]
        lru_ref[:, cs] = hs[tc - 1:tc]
        olru_ref[:, cs] = _gelu(ybr_ref[:, cs]) * hs

    h1, t, route, counts = _mix_tail(omla_ref[...], olru_ref[...], x_ref[...], cnt_ref[...],
                                     gmla_ref, glru_ref, wo_ref, gffn_ref, wr_ref, br_ref)
    h1_ref[...] = h1
    _store_token_rows(t_ref, t, 0, tc, 1)
    route_ref[...] = route
    cnt_ref[...] = counts


def _mix_weight_specs():
    return [_const_spec((CONV_W, LRU_WIDTH)), _const_spec((1, LRU_WIDTH)),
            _const_spec((LRU_BLOCKS, LRU_BLOCK, LRU_BLOCK)), _const_spec((1, LRU_WIDTH)),
            _const_spec((LRU_BLOCKS, LRU_BLOCK, LRU_BLOCK)), _const_spec((1, LRU_WIDTH)),
            _const_spec((1, LRU_WIDTH)), _const_spec((1, MLA_WIDTH)), _const_spec((1, LRU_WIDTH)),
            _const_spec((D_MODEL, D_MODEL)), _const_spec((1, D_MODEL)),
            _const_spec((D_MODEL, ROUTE_LANES)), _const_spec((1, ROUTE_LANES))]


def _mix_weights(w):
    return (w["w_conv"], w["b_conv"], w["w_rg"], w["b_rg"], w["w_ig"], w["b_ig"], w["lru_lambda"],
            w["g_out_mla"], w["g_out_lru"], w["w_o"], w["g_ffn"], w["w_route"], w["b_route"])


def _mix_prompt(xbr, ybr, omla, x2d, w, batch, seq, tc):
    nt = seq // tc
    n = batch * seq
    row = lambda width: pl.BlockSpec((tc, width), lambda b, i: (b * nt + i, 0))
    return pl.pallas_call(
        functools.partial(_mix_prompt_kernel, tc=tc),
        out_shape=(jax.ShapeDtypeStruct((n, D_MODEL), F32), jax.ShapeDtypeStruct((n * ROW_CHUNKS, LANES), F32),
                   jax.ShapeDtypeStruct((n, ROUTE_LANES), F32), jax.ShapeDtypeStruct((batch, 1, LRU_WIDTH), F32),
                   jax.ShapeDtypeStruct((1, ROUTE_LANES), F32)),
        grid=(batch, nt),
        in_specs=[row(LRU_WIDTH), row(LRU_WIDTH), row(MLA_WIDTH), row(D_MODEL)] + _mix_weight_specs(),
        out_specs=(row(D_MODEL), pl.BlockSpec((tc * ROW_CHUNKS, LANES), lambda b, i: (b * nt + i, 0)), row(ROUTE_LANES),
                   pl.BlockSpec((None, 1, LRU_WIDTH), lambda b, i: (b, 0, 0)),
                   pl.BlockSpec((1, ROUTE_LANES), lambda b, i: (0, 0))),
        scratch_shapes=[pltpu.VMEM((1, LRU_WIDTH), F32), pltpu.VMEM((8, LRU_WIDTH), F32),
                        pltpu.VMEM((tc, LRU_WIDTH), F32)],
        compiler_params=_cparams(dimension_semantics=("arbitrary", "arbitrary")),
        name="mix_prompt",
    )(xbr, ybr, omla, x2d, *_mix_weights(w))


def _mix_sample_kernel(xbr_ref, ybr_ref, omla_ref, x_ref, sconv_ref, slru_ref, cnt_in_ref, wconv_ref, bconv_ref, wrg_ref,
                       brg_ref, wig_ref, big_ref, lam_ref, gmla_ref, glru_ref, wo_ref, gffn_ref, wr_ref, br_ref,
                       h1_ref, t_ref, route_ref, lru_ref, cnt_ref, *, t_new):
    w_ = LRU_WIDTH
    xs = [sconv_ref[:, k * w_:(k + 1) * w_] for k in range(CONV_W - 1)] + [xbr_ref[:, t * w_:(t + 1) * w_] for t in range(t_new)]
    xc = []
    for t in range(t_new):
        acc = bconv_ref[...] + xs[t] * wconv_ref[0:1, :]
        for k in range(1, CONV_W):
            acc = acc + xs[t + k] * wconv_ref[k:k + 1, :]
        xc.append(acc)
    xc = jnp.concatenate(xc, axis=0)
    a, u = _gates(xc, wrg_ref, brg_ref, wig_ref, big_ref, lam_ref)
    nb = xbr_ref.shape[0]
    h = slru_ref[...]
    hs = []
    for t in range(t_new):
        h = a[t * nb:(t + 1) * nb] * h + u[t * nb:(t + 1) * nb]
        hs.append(h)
    lru_ref[...] = h
    hs = jnp.concatenate(hs, axis=0)
    stack = lambda ref, width: jnp.concatenate([ref[:, t * width:(t + 1) * width] for t in range(t_new)], axis=0)
    o_lru = _gelu(stack(ybr_ref, w_)) * hs
    h1, tt, route, counts = _mix_tail(stack(omla_ref, MLA_WIDTH), o_lru, stack(x_ref, D_MODEL), cnt_in_ref[...],
                                      gmla_ref, glru_ref, wo_ref, gffn_ref, wr_ref, br_ref)
    cnt_ref[...] = counts
    for t in range(t_new):
        h1_ref[:, t * D_MODEL:(t + 1) * D_MODEL] = h1[t * nb:(t + 1) * nb]
        _store_token_rows(t_ref, tt[t * nb:(t + 1) * nb], t, nb, t_new)
        route_ref[:, t * ROUTE_LANES:(t + 1) * ROUTE_LANES] = route[t * nb:(t + 1) * nb]


def _mix_sample(xbr, ybr, omla, x2d, sconv, slru, counts, w, t_new):
    nb = slru.shape[0]
    full = lambda a: pl.BlockSpec(a.shape, lambda i: (0, 0))
    ins = (xbr, ybr, omla, x2d, sconv, slru, counts)
    return pl.pallas_call(
        functools.partial(_mix_sample_kernel, t_new=t_new),
        out_shape=(jax.ShapeDtypeStruct((nb, t_new * D_MODEL), F32),
                   jax.ShapeDtypeStruct((nb * t_new * ROW_CHUNKS, LANES), F32),
                   jax.ShapeDtypeStruct((nb, t_new * ROUTE_LANES), F32), jax.ShapeDtypeStruct((nb, LRU_WIDTH), F32),
                   jax.ShapeDtypeStruct((1, ROUTE_LANES), F32)),
        grid=(1,),
        in_specs=[full(a) for a in ins] + _mix_weight_specs(),
        out_specs=(pl.BlockSpec((nb, t_new * D_MODEL), lambda i: (0, 0)),
                   pl.BlockSpec((nb * t_new * ROW_CHUNKS, LANES), lambda i: (0, 0)),
                   pl.BlockSpec((nb, t_new * ROUTE_LANES), lambda i: (0, 0)), pl.BlockSpec((nb, LRU_WIDTH), lambda i: (0, 0)),
                   pl.BlockSpec((1, ROUTE_LANES), lambda i: (0, 0))),
        compiler_params=_cparams(dimension_semantics=("arbitrary",)),
        name="mix_sample",
    )(*ins, *_mix_weights(w))


MOE_ROW_GROUP = 32


def _moe_kernel(te_ref, nu_ref, nv_ref, tok_ref, pair_ref, t_hbm, wg_ref, wu_ref, wd_ref, y_hbm, xa, xb, oa, ob, gsem, ssem, *, tm):
    del te_ref
    j = pl.program_id(0)
    n_used = nu_ref[0]
    xbufs, obufs = (xa, xb), (oa, ob)
    rc = ROW_CHUNKS
    spare = y_hbm.shape[0] - 2 * tm * rc

    def rows(jj):
        return pl.multiple_of(nv_ref[jj + 2], MOE_ROW_GROUP)

    def row_groups(n, issue):
        for g in range(tm // MOE_ROW_GROUP):
            @pl.when(n > g * MOE_ROW_GROUP)
            def _(g=g):
                for u in range(MOE_ROW_GROUP):
                    issue(g * MOE_ROW_GROUP + u)

    def start_gather(jj, k):
        def issue(r):
            src = pl.multiple_of(tok_ref[jj * tm + r], rc)
            pltpu.make_async_copy(t_hbm.at[pl.ds(src, rc)], xbufs[k].at[pl.ds(r * rc, rc)], gsem.at[k]).start()

        row_groups(rows(jj), issue)

    def wait_gather(jj, k):
        n = pl.multiple_of(rows(jj) * rc, 8)
        pltpu.make_async_copy(t_hbm.at[pl.ds(0, n)], xbufs[k].at[pl.ds(0, n)], gsem.at[k]).wait()

    def start_scatter(jj, k):
        def issue(r):
            dst = pl.multiple_of(pair_ref[(jj + 1) * tm + r], rc)
            pltpu.make_async_copy(obufs[k].at[pl.ds(r * rc, rc)], y_hbm.at[pl.ds(dst, rc)], ssem.at[k]).start()

        row_groups(rows(jj), issue)

    def wait_scatter(jj, k):
        n = pl.multiple_of(rows(jj) * rc, 8)
        pltpu.make_async_copy(obufs[k].at[pl.ds(0, n)], y_hbm.at[pl.ds(0, n)], ssem.at[k]).wait()

    @pl.when(j == 0)
    def _():
        for buf in (xa, xb, oa, ob):
            buf[...] = jnp.zeros(buf.shape, F32)
        pltpu.make_async_copy(oa, y_hbm.at[pl.ds(spare, tm * rc)], ssem.at[0]).start()
        start_gather(0, 0)

    def tile(k):
        wait_gather(j, k)
        wait_scatter(j - 2, k)
        start_gather(jnp.minimum(j + 1, n_used - 1), 1 - k)
        start_scatter(j - 1, 1 - k)
        x = _load_token_rows(xbufs[k], tm).astype(BF16)
        hg = _dot(x, wg_ref[...].astype(BF16))
        hu = _dot(x, wu_ref[...].astype(BF16))
        act = (hg * jax.nn.sigmoid(hg) * hu).astype(BF16)
        _store_token_rows(obufs[k], _dot(act, wd_ref[...].astype(BF16)), 0, tm, 1)

    def drain(k):
        start_scatter(j, k)
        wait_gather(j, 1 - k)
        wait_scatter(j - 1, 1 - k)
        wait_scatter(j, k)

    for k in range(2):
        pl.when(jnp.logical_and(j < n_used, j % 2 == k))(functools.partial(tile, k))
        pl.when(jnp.logical_and(j == n_used - 1, j % 2 == k))(functools.partial(drain, k))


def _moe(t_all, tile_expert, n_used, tile_rows, tok_of_slot, pair_of_slot, w_gate, w_up, w_down, tm):
    n_tiles = tile_expert.shape[0]
    n_tok = t_all.shape[0] // ROW_CHUNKS
    wspec = lambda a, b: pl.BlockSpec((None, a, b), lambda j, te, nu, nv, tk, pr: (te[j], 0, 0))
    grid_spec = pltpu.PrefetchScalarGridSpec(
        num_scalar_prefetch=5, grid=(n_tiles,),
        in_specs=[pl.BlockSpec(memory_space=pl.ANY), wspec(D_MODEL, D_EXPERT), wspec(D_MODEL, D_EXPERT),
                  wspec(D_EXPERT, D_MODEL)],
        out_specs=pl.BlockSpec(memory_space=pl.ANY),
        scratch_shapes=([pltpu.VMEM((tm * ROW_CHUNKS, LANES), F32)] * 4
                        + [pltpu.SemaphoreType.DMA((2,)), pltpu.SemaphoreType.DMA((2,))]))
    return pl.pallas_call(
        functools.partial(_moe_kernel, tm=tm),
        out_shape=jax.ShapeDtypeStruct(((2 * n_tok + 2 * tm) * ROW_CHUNKS, LANES), F32),
        grid_spec=grid_spec,
        compiler_params=_cparams(dimension_semantics=("arbitrary",)),
        name="moe",
    )(tile_expert, n_used, tile_rows, tok_of_slot, pair_of_slot, t_all, w_gate, w_up, w_down)


def _moe_plan(route_all, counts, tm):
    n = route_all.shape[0]
    n_pairs = 2 * n
    eflat = jnp.concatenate([route_all[:, 0], route_all[:, 1]]).astype(jnp.int32)
    rank = jnp.concatenate([route_all[:, 4], route_all[:, 5]]).astype(jnp.int32)
    counts = counts[0, N_GROUPS:N_GROUPS + N_EXPERTS].astype(jnp.int32)
    tiles_e = (counts + tm - 1) // tm
    tile_end = jnp.cumsum(tiles_e)
    tile_start = tile_end - tiles_e
    n_used = tile_end[-1]
    n_tiles = n_pairs // tm + N_EXPERTS
    slot = tile_start[eflat] * tm + rank
    tid = jnp.arange(n_tiles, dtype=jnp.int32)
    te = jnp.minimum(jnp.sum((tile_end[None, :] <= tid[:, None]).astype(jnp.int32), axis=1), N_EXPERTS - 1)
    te = jnp.where(tid < n_used, te, te[jnp.maximum(n_used - 1, 0)])
    rows = jnp.where(tid < n_used, jnp.clip(counts[te] - (tid - tile_start[te]) * tm, 0, tm), 0)
    rows = (rows + MOE_ROW_GROUP - 1) // MOE_ROW_GROUP * MOE_ROW_GROUP
    tile_rows = jnp.concatenate([jnp.full((2,), tm, jnp.int32), rows.astype(jnp.int32)])
    sid = jnp.arange(n_tiles * tm, dtype=jnp.int32)
    pad_pair = n_pairs + sid % (2 * tm)
    pair_of_slot = pad_pair.at[slot].set(jnp.arange(n_pairs, dtype=jnp.int32))
    tok_of_slot = jnp.where(pair_of_slot >= n_pairs, pair_of_slot - n_pairs,
                            jnp.where(pair_of_slot >= n, pair_of_slot - n, pair_of_slot))
    pair_shifted = jnp.concatenate([n_pairs + tm + jnp.arange(tm, dtype=jnp.int32), pair_of_slot])
    return te, n_used.reshape(1).astype(jnp.int32), tile_rows, tok_of_slot * ROW_CHUNKS, pair_shifted * ROW_CHUNKS


def _final_kernel(h1_ref, y0_ref, y1_ref, route_ref, p_ref, gple_ref, wpg_ref, bpg_ref, wpp_ref, gfin_ref, y_ref):
    route = route_ref[...]
    tm = h1_ref.shape[0]
    h2 = h1_ref[...] + (route[:, 2:3] * _load_token_rows(y0_ref, tm) + route[:, 3:4] * _load_token_rows(y1_ref, tm))
    gate = jax.nn.sigmoid(_dot(_rms(h2, gple_ref[...]).astype(BF16), wpg_ref[...]) + bpg_ref[...])
    h3 = h2 + gate * _dot(p_ref[...].astype(BF16), wpp_ref[...])
    y_ref[...] = _rms(h3, gfin_ref[...])


def _final(h1, y_pairs, route, p2d, w, blk0, n_all, tm):
    n = h1.shape[0]
    nb_all = n_all // tm
    row = lambda width: pl.BlockSpec((tm, width), lambda i: (i, 0))
    return pl.pallas_call(
        _final_kernel,
        out_shape=jax.ShapeDtypeStruct((n, D_MODEL), F32),
        grid=(n // tm,),
        in_specs=[row(D_MODEL),
                  pl.BlockSpec((tm * ROW_CHUNKS, LANES), lambda i: (blk0 + i, 0)),
                  pl.BlockSpec((tm * ROW_CHUNKS, LANES), lambda i: (nb_all + blk0 + i, 0)),
                  row(ROUTE_LANES), row(PLE_DIM), _const_spec((1, D_MODEL)), _const_spec((D_MODEL, D_MODEL)),
                  _const_spec((1, D_MODEL)), _const_spec((PLE_DIM, D_MODEL)), _const_spec((1, D_MODEL))],
        out_specs=row(D_MODEL),
        compiler_params=_cparams(dimension_semantics=("arbitrary",)),
        name="final",
    )(h1, y_pairs, y_pairs, route, p2d, w["g_ple"], w["w_ple_gate"], w["b_ple_gate"], w["w_ple_proj"], w["g_final"])


def _rope_tables(pos):
    half = QK_ROPE // 2
    inv = ROPE_THETA ** (-jnp.arange(half, dtype=F32) / half)
    ang = pos.astype(F32)[:, None] * inv[None, :]
    pad = jnp.zeros((pos.shape[0], ROPE_PAD - QK_ROPE), F32)
    cos = jnp.concatenate([jnp.cos(ang), jnp.cos(ang), pad], axis=1)
    sin = jnp.concatenate([jnp.sin(ang), jnp.sin(ang), pad], axis=1)
    return cos, sin


def _rot_cols(wr):
    half = QK_ROPE // 2
    return jnp.concatenate([-wr[..., half:], wr[..., :half]], axis=-1)


def _pad_cols(wr):
    return jnp.concatenate([wr, jnp.zeros(wr.shape[:-1] + (ROPE_PAD - QK_ROPE,), wr.dtype)], axis=-1)


def _prep_weights(g_mix, w_in, g_q, w_uq, g_kv, w_ukv, w_conv, b_conv, w_rg, b_rg, w_ig, b_ig, lru_lambda,
                  g_out_mla, g_out_lru, w_o, g_ffn, w_group, b_group, w_router, b_router, g_ple, w_ple_gate,
                  b_ple_gate, w_ple_proj, g_final):
    row = lambda v: v.reshape(1, -1).astype(F32)
    s0, s1, s2, s3 = Q_LORA, Q_LORA + KV_LORA, Q_LORA + KV_LORA + QK_ROPE, Q_LORA + KV_LORA + QK_ROPE + LRU_WIDTH
    kr = w_in[:, s1:s2]
    w_in_ext = jnp.concatenate([w_in[:, :s1], _pad_cols(kr), _pad_cols(_rot_cols(kr)), w_in[:, s2:]], axis=1)
    uq = w_uq.reshape(Q_LORA, MLA_HEADS, QK_NOPE + QK_ROPE)
    uq_r = uq[..., QK_NOPE:]
    w_uq_ext = jnp.concatenate([uq[..., :QK_NOPE].reshape(Q_LORA, -1), _pad_cols(uq_r).reshape(Q_LORA, -1),
                                _pad_cols(_rot_cols(uq_r)).reshape(Q_LORA, -1)], axis=1)
    w_uk = jnp.transpose(w_ukv[..., :QK_NOPE], (1, 2, 0))
    w_uv = jnp.transpose(w_ukv[..., QK_NOPE:], (1, 0, 2))
    w_route = jnp.concatenate([w_group, w_router, jnp.zeros((D_MODEL, ROUTE_LANES - N_GROUPS - N_EXPERTS), F32)], axis=1)
    b_route = jnp.concatenate([b_group, b_router, jnp.zeros((ROUTE_LANES - N_GROUPS - N_EXPERTS,), F32)])
    return dict(
        g_mix=row(g_mix), w_in=w_in_ext.astype(BF16), g_q=row(g_q), w_uq=w_uq_ext.astype(BF16), g_kv=row(g_kv),
        w_uk=w_uk.astype(BF16), w_uv=w_uv.astype(BF16), w_conv=w_conv.astype(F32), b_conv=row(b_conv),
        w_rg=w_rg.astype(BF16), b_rg=row(b_rg), w_ig=w_ig.astype(BF16), b_ig=row(b_ig), lru_lambda=row(lru_lambda),
        g_out_mla=row(g_out_mla), g_out_lru=row(g_out_lru), w_o=w_o.astype(BF16), g_ffn=row(g_ffn),
        w_route=w_route.astype(BF16), b_route=row(b_route), g_ple=row(g_ple), w_ple_gate=w_ple_gate.astype(BF16),
        b_ple_gate=row(b_ple_gate), w_ple_proj=w_ple_proj.astype(BF16), g_final=row(g_final))


TM_PROJ = 512
TM_FINAL = 256
TQ_ATTN = 256
ATTN_HEAD_GROUP = 4
TC_MIX = 256
TM_MOE = 256
PAGES_PER_STEP = 32
SAMPLE_BUFFERS = 4
SAMPLE_SUB_BLOCKS = 4


def kernel(x_prompt, x_sample, p_prompt, p_sample, cache_latent, cache_krope, state_lru, state_conv, page_table, g_mix, w_in, g_q, w_uq, g_kv, w_ukv, w_conv, b_conv, w_rg, b_rg, w_ig, b_ig, lru_lambda, g_out_mla, g_out_lru, w_o, g_ffn, w_group, b_group, w_router, b_router, w_gate, w_up, w_down, g_ple, w_ple_gate, b_ple_gate, w_ple_proj, g_final):
    assert w_in.shape[0] == 1, "single trunk layer"
    batch, seq, _ = x_prompt.shape
    db, t_new, _ = x_sample.shape
    page = cache_latent.shape[2]
    past_len = page_table.shape[1] * page
    n_p, n_s = batch * seq, db * t_new
    n_all = n_p + n_s

    w = _prep_weights(g_mix[0], w_in[0], g_q[0], w_uq[0], g_kv[0], w_ukv[0], w_conv[0], b_conv[0], w_rg[0], b_rg[0],
                      w_ig[0], b_ig[0], lru_lambda[0], g_out_mla[0], g_out_lru[0], w_o[0], g_ffn[0], w_group[0],
                      b_group[0], w_router[0], b_router[0], g_ple[0], w_ple_gate[0], b_ple_gate[0], w_ple_proj[0],
                      g_final)

    xp = x_prompt.reshape(n_p, D_MODEL)
    cos_p, sin_p = _rope_tables(jnp.arange(seq))
    qcat_p, kcat_p, lat_p, krope_p, xbr_p, ybr_p = _proj(xp, cos_p, sin_p, seq // TM_PROJ, w, TM_PROJ)
    omla_p = _prompt_attn(qcat_p, kcat_p, w["w_uv"], batch, seq, TQ_ATTN)
    h1_p, t_p, route_p, lru_p, counts_p = _mix_prompt(xbr_p, ybr_p, omla_p, xp, w, batch, seq, TC_MIX)

    xs = x_sample.reshape(n_s, D_MODEL)
    tm_s = min(TM_PROJ, n_s)
    cos_s, sin_s = _rope_tables(past_len + (jnp.arange(tm_s) % t_new))
    qcat_s, kcat_s, lat_s, krope_s, xbr_s, ybr_s = _proj(xs, cos_s, sin_s, 1, w, tm_s)
    q_s = qcat_s.reshape(db, t_new, MLA_HEADS, QK_CAT).transpose(0, 2, 1, 3).reshape(db, MLA_HEADS * t_new, QK_CAT)
    knew = jnp.pad(kcat_s.reshape(db, t_new, QK_CAT), ((0, 0), (0, 8 - t_new), (0, 0)))
    cache_ropet = jnp.swapaxes(cache_krope[0], 1, 2)
    omla_s = _sample_attn(page_table, q_s, knew, cache_latent[0], cache_ropet, w["w_uv"], PAGES_PER_STEP)
    h1_s, t_s, route_s, lru_s, counts = _mix_sample(
        xbr_s.reshape(db, t_new * LRU_WIDTH), ybr_s.reshape(db, t_new * LRU_WIDTH),
        omla_s.reshape(db, t_new * MLA_WIDTH), x_sample.reshape(db, t_new * D_MODEL),
        state_conv[0].reshape(db, (CONV_W - 1) * LRU_WIDTH), state_lru[0], counts_p, w, t_new)
    h1_s = h1_s.reshape(n_s, D_MODEL)
    route_s = route_s.reshape(n_s, ROUTE_LANES)

    t_all = jnp.concatenate([t_p, t_s], axis=0)
    route_all = jnp.concatenate([route_p, route_s], axis=0)
    te, n_used, tile_rows, tok_of_slot, pair_of_slot = _moe_plan(route_all, counts, TM_MOE)
    y_pairs = _moe(t_all, te, n_used, tile_rows, tok_of_slot, pair_of_slot, w_gate[0], w_up[0], w_down[0], TM_MOE)

    y_p = _final(h1_p, y_pairs, route_p, p_prompt[0].reshape(n_p, PLE_DIM), w, 0, n_all, TM_FINAL)
    y_s = _final(h1_s, y_pairs, route_s, p_sample[0].reshape(n_s, PLE_DIM), w, n_p // TM_FINAL, n_all, TM_FINAL)

    new_conv_p = xbr_p.reshape(batch, seq, LRU_WIDTH)[:, seq - (CONV_W - 1):]
    hist = jnp.concatenate([state_conv[0], xbr_s.reshape(db, t_new, LRU_WIDTH)], axis=1)
    new_conv_s = hist[:, hist.shape[1] - (CONV_W - 1):]
    return (y_p.reshape(batch, seq, D_MODEL), y_s.reshape(db, t_new, D_MODEL),
            lat_p.reshape(1, batch, seq, KV_LORA), krope_p.reshape(1, batch, seq, QK_ROPE),
            lru_p.reshape(1, batch, LRU_WIDTH), new_conv_p[None],
            lat_s.reshape(1, db, t_new, KV_LORA), krope_s.reshape(1, db, t_new, QK_ROPE),
            lru_s[None], new_conv_s[None])
```

```python
import functools

import jax
import jax.numpy as jnp
from jax import lax
from jax.experimental import pallas as pl
from jax.experimental.pallas import tpu as pltpu

F32 = jnp.float32
BF16 = jnp.bfloat16

D_MODEL = 2048
MLA_HEADS = 8
V_HEAD = 128
MLA_WIDTH = MLA_HEADS * V_HEAD
LRU_WIDTH = D_MODEL - MLA_WIDTH
QK_NOPE = 128
QK_ROPE = 64
Q_LORA = 512
KV_LORA = 256
ROPE_THETA = 10000.0
SM_SCALE = (QK_NOPE + QK_ROPE) ** -0.5
NEG_INF = -1e30
LRU_BLOCKS = 8
LRU_BLOCK = LRU_WIDTH // LRU_BLOCKS
CONV_W = 4
LRU_C = 8.0
N_GROUPS = 4
EXPERTS_PER_GROUP = 8
N_EXPERTS = N_GROUPS * EXPERTS_PER_GROUP
D_EXPERT = 512
PLE_DIM = 256
EPS = 1e-6

LANES = 128
ROPE_PAD = LANES
QK_CAT = KV_LORA + ROPE_PAD
C_Q0, C_KV0, C_KRA0, C_KRB0, C_X0, C_Y0, IN_EXT = 0, 512, 768, 896, 1024, 2048, 3072
ROUTE_LANES = LANES
VMEM_LIMIT = 56 * 1024 * 1024


def _cparams(**kw):
    return pltpu.CompilerParams(vmem_limit_bytes=VMEM_LIMIT, **kw)


def _rms(x, g):
    return x * lax.rsqrt(jnp.mean(x * x, axis=-1, keepdims=True) + EPS) * g


def _dot(a, b):
    return jnp.dot(a, b, preferred_element_type=F32)


def _dot_nt(a, b):
    return lax.dot_general(a, b, (((1,), (1,)), ((), ())), preferred_element_type=F32)


def _const_spec(shape):
    nd = len(shape)
    return pl.BlockSpec(shape, lambda *_: (0,) * nd, pipeline_mode=pl.Buffered(1))


def _proj_kernel(x_ref, cos_ref, sin_ref, gmix_ref, win_ref, gq_ref, wuq_ref, gkv_ref, wuk_ref,
                 qcat_ref, kcat_ref, lat_ref, krope_ref, xbr_ref, ybr_ref):
    u = _rms(x_ref[...], gmix_ref[...]).astype(BF16)
    z = _dot(u, win_ref[...])
    xbr_ref[...] = z[:, C_X0:C_Y0]
    ybr_ref[...] = z[:, C_Y0:IN_EXT]
    cos = cos_ref[...]
    sin = sin_ref[...]
    lat = _rms(z[:, C_KV0:C_KRA0], gkv_ref[...])
    kr = z[:, C_KRA0:C_KRB0] * cos + z[:, C_KRB0:C_X0] * sin
    lat_ref[...] = lat
    krope_ref[...] = kr[:, :QK_ROPE]
    kcat_ref[:, 0:KV_LORA] = lat.astype(BF16)
    kcat_ref[:, KV_LORA:QK_CAT] = kr.astype(BF16)
    qn = _rms(z[:, C_Q0:C_KV0], gq_ref[...]).astype(BF16)
    q = _dot(qn, wuq_ref[...])
    ra0 = MLA_HEADS * QK_NOPE
    rb0 = ra0 + MLA_HEADS * ROPE_PAD
    for h in range(MLA_HEADS):
        ql = _dot(q[:, h * QK_NOPE:(h + 1) * QK_NOPE].astype(BF16), wuk_ref[h])
        qr = (q[:, ra0 + h * ROPE_PAD:ra0 + (h + 1) * ROPE_PAD] * cos
              + q[:, rb0 + h * ROPE_PAD:rb0 + (h + 1) * ROPE_PAD] * sin)
        qcat_ref[:, h * QK_CAT:h * QK_CAT + KV_LORA] = ql.astype(BF16)
        qcat_ref[:, h * QK_CAT + KV_LORA:(h + 1) * QK_CAT] = qr.astype(BF16)


def _proj(x2d, cos, sin, pos_blocks, w, tm):
    n = x2d.shape[0]
    row = lambda width: pl.BlockSpec((tm, width), lambda i: (i, 0))
    pos = pl.BlockSpec((tm, ROPE_PAD), lambda i: (i % pos_blocks, 0))
    out_shape = (
        jax.ShapeDtypeStruct((n, MLA_HEADS * QK_CAT), BF16),
        jax.ShapeDtypeStruct((n, QK_CAT), BF16),
        jax.ShapeDtypeStruct((n, KV_LORA), F32),
        jax.ShapeDtypeStruct((n, QK_ROPE), F32),
        jax.ShapeDtypeStruct((n, LRU_WIDTH), F32),
        jax.ShapeDtypeStruct((n, LRU_WIDTH), F32),
    )
    return pl.pallas_call(
        _proj_kernel,
        out_shape=out_shape,
        grid=(n // tm,),
        in_specs=[row(D_MODEL), pos, pos, _const_spec((1, D_MODEL)), _const_spec((D_MODEL, IN_EXT)),
                  _const_spec((1, Q_LORA)), _const_spec((Q_LORA, 3 * MLA_HEADS * LANES)),
                  _const_spec((1, KV_LORA)), _const_spec((MLA_HEADS, QK_NOPE, KV_LORA))],
        out_specs=(row(MLA_HEADS * QK_CAT), row(QK_CAT), row(KV_LORA), row(QK_ROPE), row(LRU_WIDTH), row(LRU_WIDTH)),
        compiler_params=_cparams(dimension_semantics=("arbitrary",)),
        name="proj",
    )(x2d, cos, sin, w["g_mix"], w["w_in"], w["g_q"], w["w_uq"], w["g_kv"], w["w_uk"])


def _prompt_attn_kernel(q_ref, k_ref, wuv_ref, o_ref, *scratch, tq, group):
    m_refs, l_refs, acc_refs = scratch[:MLA_HEADS], scratch[MLA_HEADS:2 * MLA_HEADS], scratch[2 * MLA_HEADS:]
    qi = pl.program_id(1)
    for h in range(MLA_HEADS):
        m_refs[h][...] = jnp.full(m_refs[h].shape, NEG_INF, F32)
        l_refs[h][...] = jnp.zeros(l_refs[h].shape, F32)
        acc_refs[h][...] = jnp.zeros(acc_refs[h].shape, F32)

    def step(kb, masked):
        k = k_ref[pl.ds(pl.multiple_of(kb * tq, tq), tq), :]
        v = k[:, :KV_LORA]
        if masked:
            row = lax.broadcasted_iota(jnp.int32, (tq, tq), 0)
            col = lax.broadcasted_iota(jnp.int32, (tq, tq), 1)
            keep = col <= row
        for h0 in range(0, MLA_HEADS, group):
            heads = range(h0, h0 + group)
            scores = [_dot_nt(q_ref[:, h * QK_CAT:(h + 1) * QK_CAT], k) * SM_SCALE for h in heads]
            probs = []
            for h, s in zip(heads, scores):
                if masked:
                    s = jnp.where(keep, s, NEG_INF)
                m_prev = m_refs[h][...]
                m_new = jnp.maximum(m_prev, jnp.max(s, axis=1, keepdims=True))
                alpha = jnp.exp(m_prev - m_new)
                p = jnp.exp(s - jnp.concatenate([m_new] * (tq // LANES), axis=1))
                l_refs[h][...] = alpha * l_refs[h][...] + jnp.sum(p, axis=1, keepdims=True)
                m_refs[h][...] = m_new
                probs.append((alpha, p.astype(BF16)))
            for h, (alpha, p) in zip(heads, probs):
                acc_refs[h][...] = jnp.concatenate([alpha] * (KV_LORA // LANES), axis=1) * acc_refs[h][...] + _dot(p, v)

    def body(kb, carry):
        step(kb, False)
        return carry

    lax.fori_loop(0, qi, body, 0)
    step(qi, True)
    for h in range(MLA_HEADS):
        o = acc_refs[h][...] / jnp.concatenate([l_refs[h][...]] * (KV_LORA // LANES), axis=1)
        o_ref[:, h * V_HEAD:(h + 1) * V_HEAD] = _dot(o.astype(BF16), wuv_ref[h])


def _prompt_attn(qcat, kcat, wuv, batch, seq, tq):
    nq = seq // tq
    return pl.pallas_call(
        functools.partial(_prompt_attn_kernel, tq=tq, group=ATTN_HEAD_GROUP),
        out_shape=jax.ShapeDtypeStruct((batch * seq, MLA_WIDTH), F32),
        grid=(batch, nq),
        in_specs=[pl.BlockSpec((tq, MLA_HEADS * QK_CAT), lambda b, i: (b * nq + i, 0)),
                  pl.BlockSpec((seq, QK_CAT), lambda b, i: (b, 0)),
                  _const_spec((MLA_HEADS, KV_LORA, V_HEAD))],
        out_specs=pl.BlockSpec((tq, MLA_WIDTH), lambda b, i: (b * nq + i, 0)),
        scratch_shapes=([pltpu.VMEM((tq, LANES), F32)] * (2 * MLA_HEADS) + [pltpu.VMEM((tq, KV_LORA), F32)] * MLA_HEADS),
        compiler_params=_cparams(dimension_semantics=("arbitrary", "arbitrary")),
        name="prompt_attn",
    )(qcat, kcat, wuv)


def _sample_attn_kernel(pt_ref, q_ref, knew_ref, lat_hbm, ropet_hbm, wuv_ref, o_ref, *scratch,
                        pages, page, n_groups, n_buf, t_new, n_sub):
    lat_bufs, rope_bufs = scratch[:n_buf], scratch[n_buf:2 * n_buf]
    sem, m_ref, l_ref, acc_ref = scratch[2 * n_buf:]
    c = pl.program_id(1)
    step = pl.program_id(0) * n_groups + c
    n_steps = pl.num_programs(0) * n_groups
    bufs = tuple(zip(lat_bufs, rope_bufs))

    def start_chunk(chunk, k):
        for g in range(pages):
            pid = pt_ref[chunk * pages + g]
            pltpu.make_async_copy(lat_hbm.at[pid], bufs[k][0].at[g], sem.at[0, k]).start()
            pltpu.make_async_copy(ropet_hbm.at[pid], bufs[k][1].at[g], sem.at[1, k]).start()

    def wait_chunk(k):
        pltpu.make_async_copy(lat_hbm.at[pl.ds(0, pages)], bufs[k][0], sem.at[0, k]).wait()
        pltpu.make_async_copy(ropet_hbm.at[pl.ds(0, pages)], bufs[k][1], sem.at[1, k]).wait()

    q = q_ref[0]
    q_lat = q[:, :KV_LORA]
    q_rope = q[:, KV_LORA:KV_LORA + QK_ROPE]
    ps = pages // n_sub

    def attend(k):
        latbuf, ropebuf = bufs[k]
        lats, scores, parts = [], [], []
        for sb in range(n_sub):
            lat = latbuf[sb * ps:(sb + 1) * ps].reshape(ps * page, KV_LORA).astype(BF16)
            s_rope = jnp.concatenate([_dot(q_rope, ropebuf[sb * ps + g].astype(BF16)) for g in range(ps)], axis=1)
            lats.append(lat)
            scores.append((_dot_nt(q_lat, lat) + s_rope) * SM_SCALE)
        for sb in range(n_sub):
            m_i = jnp.max(scores[sb], axis=1, keepdims=True)
            p = jnp.exp(scores[sb] - m_i)
            parts.append((m_i, jnp.sum(p, axis=1, keepdims=True), p.astype(BF16)))
        parts = [(m_i, l_i, _dot(p, lats[sb])) for sb, (m_i, l_i, p) in enumerate(parts)]
        m_prev = m_ref[...]
        m_new = m_prev
        for m_i, _, _ in parts:
            m_new = jnp.maximum(m_new, m_i)
        alpha = jnp.exp(m_prev - m_new)
        l = alpha * l_ref[...]
        acc = alpha * acc_ref[...]
        for m_i, l_i, o_i in parts:
            w_i = jnp.exp(m_i - m_new)
            l = l + w_i * l_i
            acc = acc + w_i * o_i
        l_ref[...] = l
        acc_ref[...] = acc
        m_ref[...] = m_new

    @pl.when(step == 0)
    def _():
        for k in range(n_buf):
            start_chunk(k, k)

    @pl.when(c == 0)
    def _():
        m_ref[...] = jnp.full(m_ref.shape, NEG_INF, F32)
        l_ref[...] = jnp.zeros(l_ref.shape, F32)
        acc_ref[...] = jnp.zeros(acc_ref.shape, F32)

    last_chunk = n_buf * n_steps - 1
    for k in range(n_buf):
        wait_chunk(k)
        attend(k)
        start_chunk(jnp.minimum(n_buf * (step + 1) + k, last_chunk), k)

    @pl.when(step == n_steps - 1)
    def _():
        for k in range(n_buf):
            wait_chunk(k)

    @pl.when(c == n_groups - 1)
    def _():
        qf = q.astype(F32)
        kn = knew_ref[0].astype(F32)
        tok = lax.broadcasted_iota(jnp.int32, (q.shape[0], 1), 0) & (t_new - 1)
        cols = []
        for j in range(t_new):
            sj = jnp.sum(qf * kn[j:j + 1, :], axis=1, keepdims=True) * SM_SCALE
            cols.append(jnp.where(tok >= j, sj, NEG_INF))
        m_prev = m_ref[...]
        m_new = m_prev
        for sj in cols:
            m_new = jnp.maximum(m_new, sj)
        alpha = jnp.exp(m_prev - m_new)
        l = alpha * l_ref[...]
        acc = alpha * acc_ref[...]
        for j, sj in enumerate(cols):
            pj = jnp.exp(sj - m_new)
            l = l + pj
            acc = acc + pj * kn[j:j + 1, :KV_LORA]
        o = (acc / l).astype(BF16)
        for h in range(MLA_HEADS):
            oh = _dot(o, wuv_ref[h])
            o_ref[0, :, h * V_HEAD:(h + 1) * V_HEAD] = oh[h * t_new:(h + 1) * t_new]


def _sample_attn(page_table, q_s, knew, cache_lat, cache_ropet, wuv, pages):
    db, n_pages = page_table.shape
    page = cache_lat.shape[1]
    n_buf = SAMPLE_BUFFERS
    n_groups = n_pages // (n_buf * pages)
    assert n_groups * n_buf * pages == n_pages
    rows = q_s.shape[1]
    t_new = rows // MLA_HEADS
    in_specs = [pl.BlockSpec((1, rows, QK_CAT), lambda b, c, pt: (b, 0, 0)),
                pl.BlockSpec((1, 8, QK_CAT), lambda b, c, pt: (b, 0, 0)),
                pl.BlockSpec(memory_space=pl.ANY), pl.BlockSpec(memory_space=pl.ANY),
                pl.BlockSpec((MLA_HEADS, KV_LORA, V_HEAD), lambda b, c, pt: (0, 0, 0))]
    grid_spec = pltpu.PrefetchScalarGridSpec(
        num_scalar_prefetch=1, grid=(db, n_groups), in_specs=in_specs,
        out_specs=pl.BlockSpec((1, t_new, MLA_WIDTH), lambda b, c, pt: (b, 0, 0)),
        scratch_shapes=([pltpu.VMEM((pages, page, KV_LORA), F32)] * n_buf
                        + [pltpu.VMEM((pages, QK_ROPE, page), F32)] * n_buf
                        + [pltpu.SemaphoreType.DMA((2, n_buf)), pltpu.VMEM((rows, 1), F32),
                           pltpu.VMEM((rows, 1), F32), pltpu.VMEM((rows, KV_LORA), F32)]))
    return pl.pallas_call(
        functools.partial(_sample_attn_kernel, pages=pages, page=page, n_groups=n_groups, n_buf=n_buf, t_new=t_new,
                          n_sub=SAMPLE_SUB_BLOCKS),
        out_shape=jax.ShapeDtypeStruct((db, t_new, MLA_WIDTH), F32),
        grid_spec=grid_spec,
        compiler_params=_cparams(dimension_semantics=("arbitrary", "arbitrary")),
        name="sample_attn",
    )(page_table.reshape(-1), q_s, knew, cache_lat, cache_ropet, wuv)


def _gate_block(xc, n, wrg_ref, brg_ref, wig_ref, big_ref, lam_ref):
    cs = slice(n * LRU_BLOCK, (n + 1) * LRU_BLOCK)
    xb = xc.astype(BF16)
    r = jax.nn.sigmoid(_dot(xb, wrg_ref[n]) + brg_ref[:, cs])
    i = jax.nn.sigmoid(_dot(xb, wig_ref[n]) + big_ref[:, cs])
    neg_lam = -lam_ref[:, cs]
    softplus = jnp.maximum(neg_lam, 0.0) + jnp.log(1.0 + jnp.exp(-jnp.abs(neg_lam)))
    log_a = -LRU_C * r * softplus
    a = jnp.exp(log_a)
    gap = 1.0 - a * a
    root = jnp.where(gap > 0.0, gap * lax.rsqrt(gap), 0.0)
    u = root * i * xc
    return a, u


def _gates(xc, wrg_ref, brg_ref, wig_ref, big_ref, lam_ref):
    blocks = [_gate_block(xc[:, n * LRU_BLOCK:(n + 1) * LRU_BLOCK], n, wrg_ref, brg_ref, wig_ref, big_ref, lam_ref)
              for n in range(LRU_BLOCKS)]
    return (jnp.concatenate([a for a, _ in blocks], axis=1), jnp.concatenate([u for _, u in blocks], axis=1))


def _gelu(y):
    return 0.5 * y * (1.0 + jnp.tanh(0.7978845608028654 * (y + 0.044715 * (y * y * y))))


def _route(logits, counts):
    lane = lax.broadcasted_iota(jnp.int32, logits.shape, 1)
    far = jnp.int32(4 * ROUTE_LANES)
    gmask = lane < N_GROUPS
    gl = jnp.where(gmask, logits, NEG_INF)
    gmax = jnp.max(gl, axis=1, keepdims=True)
    gidx = jnp.min(jnp.where(gl == gmax, lane, far), axis=1, keepdims=True)
    g_w = 1.0 / jnp.sum(jnp.where(gmask, jnp.exp(gl - gmax), 0.0), axis=1, keepdims=True)
    lo = N_GROUPS + gidx * EXPERTS_PER_GROUP
    emask = jnp.logical_and(lane >= lo, lane < lo + EXPERTS_PER_GROUP)
    el = jnp.where(emask, logits, NEG_INF)
    emax = jnp.max(el, axis=1, keepdims=True)
    ex = jnp.where(emask, jnp.exp(el - emax), 0.0)
    prob = jnp.where(emask, ex / jnp.sum(ex, axis=1, keepdims=True), -1.0)
    p1 = jnp.max(prob, axis=1, keepdims=True)
    i1 = jnp.min(jnp.where(prob == p1, lane, far), axis=1, keepdims=True)
    rest = jnp.where(lane == i1, -1.0, prob)
    p2 = jnp.max(rest, axis=1, keepdims=True)
    i2 = jnp.min(jnp.where(rest == p2, lane, far), axis=1, keepdims=True)
    den = p1 + p2
    w1 = g_w * p1 / den
    w2 = g_w * p2 / den
    e1 = (i1 - N_GROUPS).astype(F32)
    e2 = (i2 - N_GROUPS).astype(F32)
    n = logits.shape[0]
    hit1 = lane == i1
    hit2 = lane == i2
    chosen = jnp.where(hit1, 1.0, jnp.where(hit2, 1.0, 0.0))
    earlier = jnp.where(lax.broadcasted_iota(jnp.int32, (n, n), 1) < lax.broadcasted_iota(jnp.int32, (n, n), 0), 1.0, 0.0)
    before = _dot(earlier.astype(BF16), chosen.astype(BF16)) + counts
    r1 = jnp.sum(jnp.where(hit1, before, 0.0), axis=1, keepdims=True)
    r2 = jnp.sum(jnp.where(hit2, before, 0.0), axis=1, keepdims=True)
    out = jnp.where(lane == 0, e1, jnp.where(lane == 1, e2, jnp.where(lane == 2, w1, jnp.where(lane == 3, w2,
          jnp.where(lane == 4, r1, jnp.where(lane == 5, r2, 0.0))))))
    return out, counts + jnp.sum(chosen, axis=0, keepdims=True)


ROW_CHUNKS = D_MODEL // LANES


def _store_token_rows(ref, val, first, count, step):
    for c in range(ROW_CHUNKS):
        ref[pl.ds(first * ROW_CHUNKS + c, count, stride=step * ROW_CHUNKS), :] = val[:, c * LANES:(c + 1) * LANES]


def _load_token_rows(ref, count):
    return jnp.concatenate([ref[pl.ds(c, count, stride=ROW_CHUNKS), :] for c in range(ROW_CHUNKS)], axis=1)


def _mix_tail(o_mla, o_lru, x, counts, gmla_ref, glru_ref, wo_ref, gffn_ref, wr_ref, br_ref):
    mixed = jnp.concatenate([_rms(o_mla, gmla_ref[...]), _rms(o_lru, glru_ref[...])], axis=1).astype(BF16)
    h1 = x + _dot(mixed, wo_ref[...])
    t = _rms(h1, gffn_ref[...])
    logits = _dot(t.astype(BF16), wr_ref[...]) + br_ref[...]
    route, counts = _route(logits, counts)
    return h1, t, route, counts


def _mix_prompt_kernel(xbr_ref, ybr_ref, omla_ref, x_ref, wconv_ref, bconv_ref, wrg_ref, brg_ref, wig_ref, big_ref,
                       lam_ref, gmla_ref, glru_ref, wo_ref, gffn_ref, wr_ref, br_ref,
                       h1_ref, t_ref, route_ref, lru_ref, cnt_ref, hc_ref, tail_ref, olru_ref, *, tc):
    @pl.when(pl.program_id(1) == 0)
    def _():
        hc_ref[...] = jnp.zeros(hc_ref.shape, F32)
        tail_ref[...] = jnp.zeros(tail_ref.shape, F32)

    @pl.when(jnp.logical_and(pl.program_id(0) == 0, pl.program_id(1) == 0))
    def _():
        cnt_ref[...] = jnp.zeros(cnt_ref.shape, F32)

    sub8 = lax.broadcasted_iota(jnp.int32, (tc // 8, 8, LRU_BLOCK), 1)
    row = lax.broadcasted_iota(jnp.int32, (tc, LRU_BLOCK), 0)
    for n in range(LRU_BLOCKS):
        cs = slice(n * LRU_BLOCK, (n + 1) * LRU_BLOCK)
        xb = xbr_ref[:, cs]
        tail = tail_ref[:, cs]
        xc = bconv_ref[:, cs] + xb * wconv_ref[CONV_W - 1:CONV_W, cs]
        x3 = jnp.concatenate([tail, xb], axis=0).reshape(tc // 8 + 1, 8, LRU_BLOCK)
        for j in range(1, CONV_W):
            rot = pltpu.roll(x3, j, axis=1)
            sh = jnp.where(sub8 < j, rot[:-1], rot[1:]).reshape(tc, LRU_BLOCK)
            xc = xc + sh * wconv_ref[CONV_W - 1 - j:CONV_W - j, cs]
        tail_ref[:, cs] = xb[tc - 8:tc]

        a, u = _gate_block(xc, n, wrg_ref, brg_ref, wig_ref, big_ref, lam_ref)
        d = 1
        while d < tc:
            keep = row >= d
            u = jnp.where(keep, a * pltpu.roll(u, d, axis=0) + u, u)
            a = jnp.where(keep, a * pltpu.roll(a, d, axis=0), a)
            d *= 2
        hs = a * hc_ref[:, cs] + u
        hc_ref[:, cs] = hs[tc - 1:tc]
        lru_ref[:, cs] = hs[tc - 1:tc]
        olru_ref[:, cs] = _gelu(ybr_ref[:, cs]) * hs

    h1, t, route, counts = _mix_tail(omla_ref[...], olru_ref[...], x_ref[...], cnt_ref[...],
                                     gmla_ref, glru_ref, wo_ref, gffn_ref, wr_ref, br_ref)
    h1_ref[...] = h1
    _store_token_rows(t_ref, t, 0, tc, 1)
    route_ref[...] = route
    cnt_ref[...] = counts


def _mix_weight_specs():
    return [_const_spec((CONV_W, LRU_WIDTH)), _const_spec((1, LRU_WIDTH)),
            _const_spec((LRU_BLOCKS, LRU_BLOCK, LRU_BLOCK)), _const_spec((1, LRU_WIDTH)),
            _const_spec((LRU_BLOCKS, LRU_BLOCK, LRU_BLOCK)), _const_spec((1, LRU_WIDTH)),
            _const_spec((1, LRU_WIDTH)), _const_spec((1, MLA_WIDTH)), _const_spec((1, LRU_WIDTH)),
            _const_spec((D_MODEL, D_MODEL)), _const_spec((1, D_MODEL)),
            _const_spec((D_MODEL, ROUTE_LANES)), _const_spec((1, ROUTE_LANES))]


def _mix_weights(w):
    return (w["w_conv"], w["b_conv"], w["w_rg"], w["b_rg"], w["w_ig"], w["b_ig"], w["lru_lambda"],
            w["g_out_mla"], w["g_out_lru"], w["w_o"], w["g_ffn"], w["w_route"], w["b_route"])


def _mix_prompt(xbr, ybr, omla, x2d, w, batch, seq, tc, n_all):
    nt = seq // tc
    n = batch * seq
    row = lambda width: pl.BlockSpec((tc, width), lambda b, i: (b * nt + i, 0))
    n_w = len(_mix_weight_specs())

    def entry(*refs):
        _mix_prompt_kernel(*refs[:4 + n_w], *refs[5 + n_w:], tc=tc)

    return pl.pallas_call(
        entry,
        out_shape=(jax.ShapeDtypeStruct((n, D_MODEL), F32), jax.ShapeDtypeStruct((n_all * ROW_CHUNKS, LANES), F32),
                   jax.ShapeDtypeStruct((n, ROUTE_LANES), F32), jax.ShapeDtypeStruct((batch, 1, LRU_WIDTH), F32),
                   jax.ShapeDtypeStruct((1, ROUTE_LANES), F32)),
        grid=(batch, nt),
        in_specs=([row(LRU_WIDTH), row(LRU_WIDTH), row(MLA_WIDTH), row(D_MODEL)] + _mix_weight_specs()
                  + [pl.BlockSpec(memory_space=pl.ANY)]),
        out_specs=(row(D_MODEL), pl.BlockSpec((tc * ROW_CHUNKS, LANES), lambda b, i: (b * nt + i, 0)), row(ROUTE_LANES),
                   pl.BlockSpec((None, 1, LRU_WIDTH), lambda b, i: (b, 0, 0)),
                   pl.BlockSpec((1, ROUTE_LANES), lambda b, i: (0, 0))),
        scratch_shapes=[pltpu.VMEM((1, LRU_WIDTH), F32), pltpu.VMEM((8, LRU_WIDTH), F32),
                        pltpu.VMEM((tc, LRU_WIDTH), F32)],
        input_output_aliases={4 + n_w: 1},
        compiler_params=_cparams(dimension_semantics=("arbitrary", "arbitrary")),
        name="mix_prompt",
    )(xbr, ybr, omla, x2d, *_mix_weights(w), jnp.zeros((n_all * ROW_CHUNKS, LANES), F32))


def _mix_sample_kernel(xbr_ref, ybr_ref, omla_ref, x_ref, sconv_ref, slru_ref, cnt_in_ref, wconv_ref, bconv_ref, wrg_ref,
                       brg_ref, wig_ref, big_ref, lam_ref, gmla_ref, glru_ref, wo_ref, gffn_ref, wr_ref, br_ref,
                       h1_ref, t_ref, route_ref, lru_ref, cnt_ref, *, t_new):
    w_ = LRU_WIDTH
    xs = [sconv_ref[:, k * w_:(k + 1) * w_] for k in range(CONV_W - 1)] + [xbr_ref[:, t * w_:(t + 1) * w_] for t in range(t_new)]
    xc = []
    for t in range(t_new):
        acc = bconv_ref[...] + xs[t] * wconv_ref[0:1, :]
        for k in range(1, CONV_W):
            acc = acc + xs[t + k] * wconv_ref[k:k + 1, :]
        xc.append(acc)
    xc = jnp.concatenate(xc, axis=0)
    a, u = _gates(xc, wrg_ref, brg_ref, wig_ref, big_ref, lam_ref)
    nb = xbr_ref.shape[0]
    h = slru_ref[...]
    hs = []
    for t in range(t_new):
        h = a[t * nb:(t + 1) * nb] * h + u[t * nb:(t + 1) * nb]
        hs.append(h)
    lru_ref[...] = h
    hs = jnp.concatenate(hs, axis=0)
    stack = lambda ref, width: jnp.concatenate([ref[:, t * width:(t + 1) * width] for t in range(t_new)], axis=0)
    o_lru = _gelu(stack(ybr_ref, w_)) * hs
    h1, tt, route, counts = _mix_tail(stack(omla_ref, MLA_WIDTH), o_lru, stack(x_ref, D_MODEL), cnt_in_ref[...],
                                      gmla_ref, glru_ref, wo_ref, gffn_ref, wr_ref, br_ref)
    cnt_ref[...] = counts
    for t in range(t_new):
        h1_ref[:, t * D_MODEL:(t + 1) * D_MODEL] = h1[t * nb:(t + 1) * nb]
        _store_token_rows(t_ref, tt[t * nb:(t + 1) * nb], t, nb, t_new)
        route_ref[:, t * ROUTE_LANES:(t + 1) * ROUTE_LANES] = route[t * nb:(t + 1) * nb]


def _mix_sample(t_all, xbr, ybr, omla, x2d, sconv, slru, counts, w, t_new):
    nb = slru.shape[0]
    n_s = nb * t_new
    assert t_all.shape[0] % (n_s * ROW_CHUNKS) == 0
    t_block = t_all.shape[0] // (n_s * ROW_CHUNKS) - 1
    full = lambda a: pl.BlockSpec(a.shape, lambda i: (0, 0))
    ins = (xbr, ybr, omla, x2d, sconv, slru, counts)
    n_w = len(_mix_weight_specs())
    return pl.pallas_call(
        _mix_sample_entry(t_new),
        out_shape=(jax.ShapeDtypeStruct((nb, t_new * D_MODEL), F32),
                   jax.ShapeDtypeStruct(t_all.shape, F32),
                   jax.ShapeDtypeStruct((nb, t_new * ROUTE_LANES), F32), jax.ShapeDtypeStruct((nb, LRU_WIDTH), F32),
                   jax.ShapeDtypeStruct((1, ROUTE_LANES), F32)),
        grid=(1,),
        in_specs=[full(a) for a in ins] + _mix_weight_specs() + [pl.BlockSpec(memory_space=pl.ANY)],
        out_specs=(pl.BlockSpec((nb, t_new * D_MODEL), lambda i: (0, 0)),
                   pl.BlockSpec((n_s * ROW_CHUNKS, LANES), lambda i: (t_block, 0)),
                   pl.BlockSpec((nb, t_new * ROUTE_LANES), lambda i: (0, 0)), pl.BlockSpec((nb, LRU_WIDTH), lambda i: (0, 0)),
                   pl.BlockSpec((1, ROUTE_LANES), lambda i: (0, 0))),
        input_output_aliases={len(ins) + n_w: 1},
        compiler_params=_cparams(dimension_semantics=("arbitrary",)),
        name="mix_sample",
    )(*ins, *_mix_weights(w), t_all)


def _mix_sample_entry(t_new):
    n_in = 7 + len(_mix_weight_specs())

    def entry(*refs):
        _mix_sample_kernel(*refs[:n_in], *refs[n_in + 1:], t_new=t_new)

    return entry


MOE_ROW_GROUP = 32


def _moe_kernel(te_ref, nu_ref, nv_ref, tok_ref, pair_ref, t_hbm, wg_ref, wu_ref, wd_ref, y_hbm, xa, xb, oa, ob, gsem, ssem, *, tm):
    del te_ref
    j = pl.program_id(0)
    n_used = nu_ref[0]
    xbufs, obufs = (xa, xb), (oa, ob)
    rc = ROW_CHUNKS
    spare = y_hbm.shape[0] - 2 * tm * rc

    def rows(jj):
        return pl.multiple_of(nv_ref[jj + 2], MOE_ROW_GROUP)

    def row_groups(n, issue):
        for g in range(tm // MOE_ROW_GROUP):
            @pl.when(n > g * MOE_ROW_GROUP)
            def _(g=g):
                for u in range(MOE_ROW_GROUP):
                    issue(g * MOE_ROW_GROUP + u)

    def start_gather(jj, k):
        def issue(r):
            src = pl.multiple_of(tok_ref[jj * tm + r], rc)
            pltpu.make_async_copy(t_hbm.at[pl.ds(src, rc)], xbufs[k].at[pl.ds(r * rc, rc)], gsem.at[k]).start()

        row_groups(rows(jj), issue)

    def wait_gather(jj, k):
        n = pl.multiple_of(rows(jj) * rc, 8)
        pltpu.make_async_copy(t_hbm.at[pl.ds(0, n)], xbufs[k].at[pl.ds(0, n)], gsem.at[k]).wait()

    def start_scatter(jj, k):
        def issue(r):
            dst = pl.multiple_of(pair_ref[(jj + 1) * tm + r], rc)
            pltpu.make_async_copy(obufs[k].at[pl.ds(r * rc, rc)], y_hbm.at[pl.ds(dst, rc)], ssem.at[k]).start()

        row_groups(rows(jj), issue)

    def wait_scatter(jj, k):
        n = pl.multiple_of(rows(jj) * rc, 8)
        pltpu.make_async_copy(obufs[k].at[pl.ds(0, n)], y_hbm.at[pl.ds(0, n)], ssem.at[k]).wait()

    @pl.when(j == 0)
    def _():
        for buf in (xa, xb, oa, ob):
            buf[...] = jnp.zeros(buf.shape, F32)
        pltpu.make_async_copy(oa, y_hbm.at[pl.ds(spare, tm * rc)], ssem.at[0]).start()
        start_gather(0, 0)

    def tile(k):
        wait_gather(j, k)
        wait_scatter(j - 2, k)
        start_gather(jnp.minimum(j + 1, n_used - 1), 1 - k)
        start_scatter(j - 1, 1 - k)
        x = _load_token_rows(xbufs[k], tm).astype(BF16)
        hg = _dot(x, wg_ref[...].astype(BF16))
        hu = _dot(x, wu_ref[...].astype(BF16))
        act = (hg * jax.nn.sigmoid(hg) * hu).astype(BF16)
        _store_token_rows(obufs[k], _dot(act, wd_ref[...].astype(BF16)), 0, tm, 1)

    def drain(k):
        start_scatter(j, k)
        wait_gather(j, 1 - k)
        wait_scatter(j - 1, 1 - k)
        wait_scatter(j, k)

    for k in range(2):
        pl.when(jnp.logical_and(j < n_used, j % 2 == k))(functools.partial(tile, k))
        pl.when(jnp.logical_and(j == n_used - 1, j % 2 == k))(functools.partial(drain, k))


def _moe(t_all, tile_expert, n_used, tile_rows, tok_of_slot, pair_of_slot, w_gate, w_up, w_down, tm):
    n_tiles = tile_expert.shape[0]
    n_tok = t_all.shape[0] // ROW_CHUNKS
    wspec = lambda a, b: pl.BlockSpec((None, a, b), lambda j, te, nu, nv, tk, pr: (te[j], 0, 0))
    grid_spec = pltpu.PrefetchScalarGridSpec(
        num_scalar_prefetch=5, grid=(n_tiles,),
        in_specs=[pl.BlockSpec(memory_space=pl.ANY), wspec(D_MODEL, D_EXPERT), wspec(D_MODEL, D_EXPERT),
                  wspec(D_EXPERT, D_MODEL)],
        out_specs=pl.BlockSpec(memory_space=pl.ANY),
        scratch_shapes=([pltpu.VMEM((tm * ROW_CHUNKS, LANES), F32)] * 4
                        + [pltpu.SemaphoreType.DMA((2,)), pltpu.SemaphoreType.DMA((2,))]))
    return pl.pallas_call(
        functools.partial(_moe_kernel, tm=tm),
        out_shape=jax.ShapeDtypeStruct(((2 * n_tok + 2 * tm) * ROW_CHUNKS, LANES), F32),
        grid_spec=grid_spec,
        compiler_params=_cparams(dimension_semantics=("arbitrary",)),
        name="moe",
    )(tile_expert, n_used, tile_rows, tok_of_slot, pair_of_slot, t_all, w_gate, w_up, w_down)


def _moe_plan(route_all, counts, tm):
    n = route_all.shape[0]
    n_pairs = 2 * n
    eflat = jnp.concatenate([route_all[:, 0], route_all[:, 1]]).astype(jnp.int32)
    rank = jnp.concatenate([route_all[:, 4], route_all[:, 5]]).astype(jnp.int32)
    counts = counts[0, N_GROUPS:N_GROUPS + N_EXPERTS].astype(jnp.int32)
    tiles_e = (counts + tm - 1) // tm
    tile_end = jnp.cumsum(tiles_e)
    tile_start = tile_end - tiles_e
    n_used = tile_end[-1]
    n_tiles = n_pairs // tm + N_EXPERTS
    slot = tile_start[eflat] * tm + rank
    tid = jnp.arange(n_tiles, dtype=jnp.int32)
    te = jnp.minimum(jnp.sum((tile_end[None, :] <= tid[:, None]).astype(jnp.int32), axis=1), N_EXPERTS - 1)
    te = jnp.where(tid < n_used, te, te[jnp.maximum(n_used - 1, 0)])
    rows = jnp.where(tid < n_used, jnp.clip(counts[te] - (tid - tile_start[te]) * tm, 0, tm), 0)
    rows = (rows + MOE_ROW_GROUP - 1) // MOE_ROW_GROUP * MOE_ROW_GROUP
    tile_rows = jnp.concatenate([jnp.full((2,), tm, jnp.int32), rows.astype(jnp.int32)])
    sid = jnp.arange(n_tiles * tm, dtype=jnp.int32)
    pad_pair = n_pairs + sid % (2 * tm)
    pair_of_slot = pad_pair.at[slot].set(jnp.arange(n_pairs, dtype=jnp.int32))
    tok_of_slot = jnp.where(pair_of_slot >= n_pairs, pair_of_slot - n_pairs,
                            jnp.where(pair_of_slot >= n, pair_of_slot - n, pair_of_slot))
    pair_shifted = jnp.concatenate([n_pairs + tm + jnp.arange(tm, dtype=jnp.int32), pair_of_slot])
    return te, n_used.reshape(1).astype(jnp.int32), tile_rows, tok_of_slot * ROW_CHUNKS, pair_shifted * ROW_CHUNKS


def _final_kernel(h1_ref, y0_ref, y1_ref, route_ref, p_ref, gple_ref, wpg_ref, bpg_ref, wpp_ref, gfin_ref, y_ref):
    route = route_ref[...]
    tm = h1_ref.shape[0]
    h2 = h1_ref[...] + (route[:, 2:3] * _load_token_rows(y0_ref, tm) + route[:, 3:4] * _load_token_rows(y1_ref, tm))
    gate = jax.nn.sigmoid(_dot(_rms(h2, gple_ref[...]).astype(BF16), wpg_ref[...]) + bpg_ref[...])
    h3 = h2 + gate * _dot(p_ref[...].astype(BF16), wpp_ref[...])
    y_ref[...] = _rms(h3, gfin_ref[...])


def _final(h1, y_pairs, route, p2d, w, blk0, n_all, tm):
    n = h1.shape[0]
    nb_all = n_all // tm
    row = lambda width: pl.BlockSpec((tm, width), lambda i: (i, 0))
    return pl.pallas_call(
        _final_kernel,
        out_shape=jax.ShapeDtypeStruct((n, D_MODEL), F32),
        grid=(n // tm,),
        in_specs=[row(D_MODEL),
                  pl.BlockSpec((tm * ROW_CHUNKS, LANES), lambda i: (blk0 + i, 0)),
                  pl.BlockSpec((tm * ROW_CHUNKS, LANES), lambda i: (nb_all + blk0 + i, 0)),
                  row(ROUTE_LANES), row(PLE_DIM), _const_spec((1, D_MODEL)), _const_spec((D_MODEL, D_MODEL)),
                  _const_spec((1, D_MODEL)), _const_spec((PLE_DIM, D_MODEL)), _const_spec((1, D_MODEL))],
        out_specs=row(D_MODEL),
        compiler_params=_cparams(dimension_semantics=("arbitrary",)),
        name="final",
    )(h1, y_pairs, y_pairs, route, p2d, w["g_ple"], w["w_ple_gate"], w["b_ple_gate"], w["w_ple_proj"], w["g_final"])


def _rope_tables(pos):
    half = QK_ROPE // 2
    inv = ROPE_THETA ** (-jnp.arange(half, dtype=F32) / half)
    ang = pos.astype(F32)[:, None] * inv[None, :]
    pad = jnp.zeros((pos.shape[0], ROPE_PAD - QK_ROPE), F32)
    cos = jnp.concatenate([jnp.cos(ang), jnp.cos(ang), pad], axis=1)
    sin = jnp.concatenate([jnp.sin(ang), jnp.sin(ang), pad], axis=1)
    return cos, sin


def _rot_cols(wr):
    half = QK_ROPE // 2
    return jnp.concatenate([-wr[..., half:], wr[..., :half]], axis=-1)


def _pad_cols(wr):
    return jnp.concatenate([wr, jnp.zeros(wr.shape[:-1] + (ROPE_PAD - QK_ROPE,), wr.dtype)], axis=-1)


def _prep_weights(g_mix, w_in, g_q, w_uq, g_kv, w_ukv, w_conv, b_conv, w_rg, b_rg, w_ig, b_ig, lru_lambda,
                  g_out_mla, g_out_lru, w_o, g_ffn, w_group, b_group, w_router, b_router, g_ple, w_ple_gate,
                  b_ple_gate, w_ple_proj, g_final):
    row = lambda v: v.reshape(1, -1).astype(F32)
    s0, s1, s2, s3 = Q_LORA, Q_LORA + KV_LORA, Q_LORA + KV_LORA + QK_ROPE, Q_LORA + KV_LORA + QK_ROPE + LRU_WIDTH
    kr = w_in[:, s1:s2]
    w_in_ext = jnp.concatenate([w_in[:, :s1], _pad_cols(kr), _pad_cols(_rot_cols(kr)), w_in[:, s2:]], axis=1)
    uq = w_uq.reshape(Q_LORA, MLA_HEADS, QK_NOPE + QK_ROPE)
    uq_r = uq[..., QK_NOPE:]
    w_uq_ext = jnp.concatenate([uq[..., :QK_NOPE].reshape(Q_LORA, -1), _pad_cols(uq_r).reshape(Q_LORA, -1),
                                _pad_cols(_rot_cols(uq_r)).reshape(Q_LORA, -1)], axis=1)
    w_uk = jnp.transpose(w_ukv[..., :QK_NOPE], (1, 2, 0))
    w_uv = jnp.transpose(w_ukv[..., QK_NOPE:], (1, 0, 2))
    w_route = jnp.concatenate([w_group, w_router, jnp.zeros((D_MODEL, ROUTE_LANES - N_GROUPS - N_EXPERTS), F32)], axis=1)
    b_route = jnp.concatenate([b_group, b_router, jnp.zeros((ROUTE_LANES - N_GROUPS - N_EXPERTS,), F32)])
    return dict(
        g_mix=row(g_mix), w_in=w_in_ext.astype(BF16), g_q=row(g_q), w_uq=w_uq_ext.astype(BF16), g_kv=row(g_kv),
        w_uk=w_uk.astype(BF16), w_uv=w_uv.astype(BF16), w_conv=w_conv.astype(F32), b_conv=row(b_conv),
        w_rg=w_rg.astype(BF16), b_rg=row(b_rg), w_ig=w_ig.astype(BF16), b_ig=row(b_ig), lru_lambda=row(lru_lambda),
        g_out_mla=row(g_out_mla), g_out_lru=row(g_out_lru), w_o=w_o.astype(BF16), g_ffn=row(g_ffn),
        w_route=w_route.astype(BF16), b_route=row(b_route), g_ple=row(g_ple), w_ple_gate=w_ple_gate.astype(BF16),
        b_ple_gate=row(b_ple_gate), w_ple_proj=w_ple_proj.astype(BF16), g_final=row(g_final))


TM_PROJ = 512
TM_FINAL = 256
TQ_ATTN = 256
ATTN_HEAD_GROUP = 4
TC_MIX = 256
TM_MOE = 256
PAGES_PER_STEP = 32
SAMPLE_BUFFERS = 4
SAMPLE_SUB_BLOCKS = 4


def kernel(x_prompt, x_sample, p_prompt, p_sample, cache_latent, cache_krope, state_lru, state_conv, page_table, g_mix, w_in, g_q, w_uq, g_kv, w_ukv, w_conv, b_conv, w_rg, b_rg, w_ig, b_ig, lru_lambda, g_out_mla, g_out_lru, w_o, g_ffn, w_group, b_group, w_router, b_router, w_gate, w_up, w_down, g_ple, w_ple_gate, b_ple_gate, w_ple_proj, g_final):
    assert w_in.shape[0] == 1, "single trunk layer"
    batch, seq, _ = x_prompt.shape
    db, t_new, _ = x_sample.shape
    page = cache_latent.shape[2]
    past_len = page_table.shape[1] * page
    n_p, n_s = batch * seq, db * t_new
    n_all = n_p + n_s

    w = _prep_weights(g_mix[0], w_in[0], g_q[0], w_uq[0], g_kv[0], w_ukv[0], w_conv[0], b_conv[0], w_rg[0], b_rg[0],
                      w_ig[0], b_ig[0], lru_lambda[0], g_out_mla[0], g_out_lru[0], w_o[0], g_ffn[0], w_group[0],
                      b_group[0], w_router[0], b_router[0], g_ple[0], w_ple_gate[0], b_ple_gate[0], w_ple_proj[0],
                      g_final)

    xp = x_prompt.reshape(n_p, D_MODEL)
    cos_p, sin_p = _rope_tables(jnp.arange(seq))
    qcat_p, kcat_p, lat_p, krope_p, xbr_p, ybr_p = _proj(xp, cos_p, sin_p, seq // TM_PROJ, w, TM_PROJ)
    omla_p = _prompt_attn(qcat_p, kcat_p, w["w_uv"], batch, seq, TQ_ATTN)
    h1_p, t_p, route_p, lru_p, counts_p = _mix_prompt(xbr_p, ybr_p, omla_p, xp, w, batch, seq, TC_MIX, n_all)

    xs = x_sample.reshape(n_s, D_MODEL)
    tm_s = min(TM_PROJ, n_s)
    cos_s, sin_s = _rope_tables(past_len + (jnp.arange(tm_s) % t_new))
    qcat_s, kcat_s, lat_s, krope_s, xbr_s, ybr_s = _proj(xs, cos_s, sin_s, 1, w, tm_s)
    q_s = qcat_s.reshape(db, t_new, MLA_HEADS, QK_CAT).transpose(0, 2, 1, 3).reshape(db, MLA_HEADS * t_new, QK_CAT)
    knew = jnp.pad(kcat_s.reshape(db, t_new, QK_CAT), ((0, 0), (0, 8 - t_new), (0, 0)))
    cache_ropet = jnp.swapaxes(cache_krope[0], 1, 2)
    omla_s = _sample_attn(page_table, q_s, knew, cache_latent[0], cache_ropet, w["w_uv"], PAGES_PER_STEP)
    h1_s, t_all, route_s, lru_s, counts = _mix_sample(
        t_p, xbr_s.reshape(db, t_new * LRU_WIDTH), ybr_s.reshape(db, t_new * LRU_WIDTH),
        omla_s.reshape(db, t_new * MLA_WIDTH), x_sample.reshape(db, t_new * D_MODEL),
        state_conv[0].reshape(db, (CONV_W - 1) * LRU_WIDTH), state_lru[0], counts_p, w, t_new)
    h1_s = h1_s.reshape(n_s, D_MODEL)
    route_s = route_s.reshape(n_s, ROUTE_LANES)

    route_all = jnp.concatenate([route_p, route_s], axis=0)
    te, n_used, tile_rows, tok_of_slot, pair_of_slot = _moe_plan(route_all, counts, TM_MOE)
    y_pairs = _moe(t_all, te, n_used, tile_rows, tok_of_slot, pair_of_slot, w_gate[0], w_up[0], w_down[0], TM_MOE)

    y_p = _final(h1_p, y_pairs, route_p, p_prompt[0].reshape(n_p, PLE_DIM), w, 0, n_all, TM_FINAL)
    y_s = _final(h1_s, y_pairs, route_s, p_sample[0].reshape(n_s, PLE_DIM), w, n_p // TM_FINAL, n_all, TM_FINAL)

    new_conv_p = xbr_p.reshape(batch, seq, LRU_WIDTH)[:, seq - (CONV_W - 1):]
    hist = jnp.concatenate([state_conv[0], xbr_s.reshape(db, t_new, LRU_WIDTH)], axis=1)
    new_conv_s = hist[:, hist.shape[1] - (CONV_W - 1):]
    return (y_p.reshape(batch, seq, D_MODEL), y_s.reshape(db, t_new, D_MODEL),
            lat_p.reshape(1, batch, seq, KV_LORA), krope_p.reshape(1, batch, seq, QK_ROPE),
            lru_p.reshape(1, batch, LRU_WIDTH), new_conv_p[None],
            lat_s.reshape(1, db, t_new, KV_LORA), krope_s.reshape(1, db, t_new, QK_ROPE),
            lru_s[None], new_conv_s[None])
```

```python
import functools

import jax
import jax.numpy as jnp
from jax import lax
from jax.experimental import pallas as pl
from jax.experimental.pallas import tpu as pltpu

F32 = jnp.float32
BF16 = jnp.bfloat16

D_MODEL = 2048
MLA_HEADS = 8
V_HEAD = 128
MLA_WIDTH = MLA_HEADS * V_HEAD
LRU_WIDTH = D_MODEL - MLA_WIDTH
QK_NOPE = 128
QK_ROPE = 64
Q_LORA = 512
KV_LORA = 256
ROPE_THETA = 10000.0
SM_SCALE = (QK_NOPE + QK_ROPE) ** -0.5
NEG_INF = -1e30
LRU_BLOCKS = 8
LRU_BLOCK = LRU_WIDTH // LRU_BLOCKS
CONV_W = 4
LRU_C = 8.0
N_GROUPS = 4
EXPERTS_PER_GROUP = 8
N_EXPERTS = N_GROUPS * EXPERTS_PER_GROUP
D_EXPERT = 512
PLE_DIM = 256
EPS = 1e-6

LANES = 128
ROPE_PAD = LANES
QK_CAT = KV_LORA + ROPE_PAD
C_Q0, C_KV0, C_KRA0, C_KRB0, C_X0, C_Y0, IN_EXT = 0, 512, 768, 896, 1024, 2048, 3072
ROUTE_LANES = LANES
VMEM_LIMIT = 56 * 1024 * 1024


def _cparams(**kw):
    return pltpu.CompilerParams(vmem_limit_bytes=VMEM_LIMIT, **kw)


def _rms(x, g):
    return x * lax.rsqrt(jnp.mean(x * x, axis=-1, keepdims=True) + EPS) * g


def _dot(a, b):
    return jnp.dot(a, b, preferred_element_type=F32)


def _dot_nt(a, b):
    return lax.dot_general(a, b, (((1,), (1,)), ((), ())), preferred_element_type=F32)


def _const_spec(shape):
    nd = len(shape)
    return pl.BlockSpec(shape, lambda *_: (0,) * nd, pipeline_mode=pl.Buffered(1))


def _proj_kernel(x_ref, cos_ref, sin_ref, gmix_ref, win_ref, gq_ref, wuq_ref, gkv_ref, wuk_ref,
                 qcat_ref, kcat_ref, lat_ref, krope_ref, xbr_ref, ybr_ref):
    u = _rms(x_ref[...], gmix_ref[...]).astype(BF16)
    z = _dot(u, win_ref[...])
    xbr_ref[...] = z[:, C_X0:C_Y0]
    ybr_ref[...] = z[:, C_Y0:IN_EXT]
    cos = cos_ref[...]
    sin = sin_ref[...]
    lat = _rms(z[:, C_KV0:C_KRA0], gkv_ref[...])
    kr = z[:, C_KRA0:C_KRB0] * cos + z[:, C_KRB0:C_X0] * sin
    lat_ref[...] = lat
    krope_ref[...] = kr[:, :QK_ROPE]
    kcat_ref[:, 0:KV_LORA] = lat.astype(BF16)
    kcat_ref[:, KV_LORA:QK_CAT] = kr.astype(BF16)
    qn = _rms(z[:, C_Q0:C_KV0], gq_ref[...]).astype(BF16)
    q = _dot(qn, wuq_ref[...])
    ra0 = MLA_HEADS * QK_NOPE
    rb0 = ra0 + MLA_HEADS * ROPE_PAD
    for h in range(MLA_HEADS):
        ql = _dot(q[:, h * QK_NOPE:(h + 1) * QK_NOPE].astype(BF16), wuk_ref[h])
        qr = (q[:, ra0 + h * ROPE_PAD:ra0 + (h + 1) * ROPE_PAD] * cos
              + q[:, rb0 + h * ROPE_PAD:rb0 + (h + 1) * ROPE_PAD] * sin)
        qcat_ref[:, h * QK_CAT:h * QK_CAT + KV_LORA] = ql.astype(BF16)
        qcat_ref[:, h * QK_CAT + KV_LORA:(h + 1) * QK_CAT] = qr.astype(BF16)


def _proj(x2d, cos, sin, pos_blocks, w, tm):
    n = x2d.shape[0]
    row = lambda width: pl.BlockSpec((tm, width), lambda i: (i, 0))
    pos = pl.BlockSpec((tm, ROPE_PAD), lambda i: (i % pos_blocks, 0))
    out_shape = (
        jax.ShapeDtypeStruct((n, MLA_HEADS * QK_CAT), BF16),
        jax.ShapeDtypeStruct((n, QK_CAT), BF16),
        jax.ShapeDtypeStruct((n, KV_LORA), F32),
        jax.ShapeDtypeStruct((n, QK_ROPE), F32),
        jax.ShapeDtypeStruct((n, LRU_WIDTH), F32),
        jax.ShapeDtypeStruct((n, LRU_WIDTH), F32),
    )
    return pl.pallas_call(
        _proj_kernel,
        out_shape=out_shape,
        grid=(n // tm,),
        in_specs=[row(D_MODEL), pos, pos, _const_spec((1, D_MODEL)), _const_spec((D_MODEL, IN_EXT)),
                  _const_spec((1, Q_LORA)), _const_spec((Q_LORA, 3 * MLA_HEADS * LANES)),
                  _const_spec((1, KV_LORA)), _const_spec((MLA_HEADS, QK_NOPE, KV_LORA))],
        out_specs=(row(MLA_HEADS * QK_CAT), row(QK_CAT), row(KV_LORA), row(QK_ROPE), row(LRU_WIDTH), row(LRU_WIDTH)),
        compiler_params=_cparams(dimension_semantics=("arbitrary",)),
        name="proj",
    )(x2d, cos, sin, w["g_mix"], w["w_in"], w["g_q"], w["w_uq"], w["g_kv"], w["w_uk"])


def _prompt_attn_kernel(q_ref, k_ref, wuv_ref, o_ref, *scratch, tq, group):
    m_refs, l_refs, acc_refs = scratch[:MLA_HEADS], scratch[MLA_HEADS:2 * MLA_HEADS], scratch[2 * MLA_HEADS:]
    qi = pl.program_id(1)
    for h in range(MLA_HEADS):
        m_refs[h][...] = jnp.full(m_refs[h].shape, NEG_INF, F32)
        l_refs[h][...] = jnp.zeros(l_refs[h].shape, F32)
        acc_refs[h][...] = jnp.zeros(acc_refs[h].shape, F32)

    def step(kb, masked):
        k = k_ref[pl.ds(pl.multiple_of(kb * tq, tq), tq), :]
        v = k[:, :KV_LORA]
        if masked:
            row = lax.broadcasted_iota(jnp.int32, (tq, tq), 0)
            col = lax.broadcasted_iota(jnp.int32, (tq, tq), 1)
            keep = col <= row
        for h0 in range(0, MLA_HEADS, group):
            heads = range(h0, h0 + group)
            scores = [_dot_nt(q_ref[:, h * QK_CAT:(h + 1) * QK_CAT], k) * SM_SCALE for h in heads]
            probs = []
            for h, s in zip(heads, scores):
                if masked:
                    s = jnp.where(keep, s, NEG_INF)
                m_prev = m_refs[h][...]
                m_new = jnp.maximum(m_prev, jnp.max(s, axis=1, keepdims=True))
                alpha = jnp.exp(m_prev - m_new)
                p = jnp.exp(s - jnp.concatenate([m_new] * (tq // LANES), axis=1))
                l_refs[h][...] = alpha * l_refs[h][...] + jnp.sum(p, axis=1, keepdims=True)
                m_refs[h][...] = m_new
                probs.append((alpha, p.astype(BF16)))
            for h, (alpha, p) in zip(heads, probs):
                acc_refs[h][...] = jnp.concatenate([alpha] * (KV_LORA // LANES), axis=1) * acc_refs[h][...] + _dot(p, v)

    def body(kb, carry):
        step(kb, False)
        return carry

    lax.fori_loop(0, qi, body, 0)
    step(qi, True)
    for h in range(MLA_HEADS):
        o = acc_refs[h][...] / jnp.concatenate([l_refs[h][...]] * (KV_LORA // LANES), axis=1)
        o_ref[:, h * V_HEAD:(h + 1) * V_HEAD] = _dot(o.astype(BF16), wuv_ref[h])


def _prompt_attn(qcat, kcat, wuv, batch, seq, tq):
    nq = seq // tq
    return pl.pallas_call(
        functools.partial(_prompt_attn_kernel, tq=tq, group=ATTN_HEAD_GROUP),
        out_shape=jax.ShapeDtypeStruct((batch * seq, MLA_WIDTH), F32),
        grid=(batch, nq),
        in_specs=[pl.BlockSpec((tq, MLA_HEADS * QK_CAT), lambda b, i: (b * nq + i, 0)),
                  pl.BlockSpec((seq, QK_CAT), lambda b, i: (b, 0)),
                  _const_spec((MLA_HEADS, KV_LORA, V_HEAD))],
        out_specs=pl.BlockSpec((tq, MLA_WIDTH), lambda b, i: (b * nq + i, 0)),
        scratch_shapes=([pltpu.VMEM((tq, LANES), F32)] * (2 * MLA_HEADS) + [pltpu.VMEM((tq, KV_LORA), F32)] * MLA_HEADS),
        compiler_params=_cparams(dimension_semantics=("arbitrary", "arbitrary")),
        name="prompt_attn",
    )(qcat, kcat, wuv)


def _sample_attn_kernel(pt_ref, q_ref, knew_ref, lat_hbm, ropet_hbm, wuv_ref, o_ref, *scratch,
                        pages, page, n_groups, n_buf, t_new, n_sub):
    lat_bufs, rope_bufs = scratch[:n_buf], scratch[n_buf:2 * n_buf]
    sem, m_ref, l_ref, acc_ref = scratch[2 * n_buf:]
    c = pl.program_id(1)
    step = pl.program_id(0) * n_groups + c
    n_steps = pl.num_programs(0) * n_groups
    bufs = tuple(zip(lat_bufs, rope_bufs))

    def start_chunk(chunk, k):
        for g in range(pages):
            pid = pt_ref[chunk * pages + g]
            pltpu.make_async_copy(lat_hbm.at[pid], bufs[k][0].at[g], sem.at[0, k]).start()
            pltpu.make_async_copy(ropet_hbm.at[pid], bufs[k][1].at[g], sem.at[1, k]).start()

    def wait_chunk(k):
        pltpu.make_async_copy(lat_hbm.at[pl.ds(0, pages)], bufs[k][0], sem.at[0, k]).wait()
        pltpu.make_async_copy(ropet_hbm.at[pl.ds(0, pages)], bufs[k][1], sem.at[1, k]).wait()

    q = q_ref[0]
    q_lat = q[:, :KV_LORA]
    q_rope = q[:, KV_LORA:KV_LORA + QK_ROPE]
    ps = pages // n_sub

    def attend(k):
        latbuf, ropebuf = bufs[k]
        lats, scores, parts = [], [], []
        for sb in range(n_sub):
            lat = latbuf[sb * ps:(sb + 1) * ps].reshape(ps * page, KV_LORA).astype(BF16)
            s_rope = jnp.concatenate([_dot(q_rope, ropebuf[sb * ps + g].astype(BF16)) for g in range(ps)], axis=1)
            lats.append(lat)
            scores.append((_dot_nt(q_lat, lat) + s_rope) * SM_SCALE)
        for sb in range(n_sub):
            m_i = jnp.max(scores[sb], axis=1, keepdims=True)
            p = jnp.exp(scores[sb] - m_i)
            parts.append((m_i, jnp.sum(p, axis=1, keepdims=True), p.astype(BF16)))
        parts = [(m_i, l_i, _dot(p, lats[sb])) for sb, (m_i, l_i, p) in enumerate(parts)]
        m_prev = m_ref[...]
        m_new = m_prev
        for m_i, _, _ in parts:
            m_new = jnp.maximum(m_new, m_i)
        alpha = jnp.exp(m_prev - m_new)
        l = alpha * l_ref[...]
        acc = alpha * acc_ref[...]
        for m_i, l_i, o_i in parts:
            w_i = jnp.exp(m_i - m_new)
            l = l + w_i * l_i
            acc = acc + w_i * o_i
        l_ref[...] = l
        acc_ref[...] = acc
        m_ref[...] = m_new

    @pl.when(step == 0)
    def _():
        for k in range(n_buf):
            start_chunk(k, k)

    @pl.when(c == 0)
    def _():
        m_ref[...] = jnp.full(m_ref.shape, NEG_INF, F32)
        l_ref[...] = jnp.zeros(l_ref.shape, F32)
        acc_ref[...] = jnp.zeros(acc_ref.shape, F32)

    last_chunk = n_buf * n_steps - 1
    for k in range(n_buf):
        wait_chunk(k)
        attend(k)
        start_chunk(jnp.minimum(n_buf * (step + 1) + k, last_chunk), k)

    @pl.when(step == n_steps - 1)
    def _():
        for k in range(n_buf):
            wait_chunk(k)

    @pl.when(c == n_groups - 1)
    def _():
        qf = q.astype(F32)
        kn = knew_ref[0].astype(F32)
        tok = lax.broadcasted_iota(jnp.int32, (q.shape[0], 1), 0) & (t_new - 1)
        cols = []
        for j in range(t_new):
            sj = jnp.sum(qf * kn[j:j + 1, :], axis=1, keepdims=True) * SM_SCALE
            cols.append(jnp.where(tok >= j, sj, NEG_INF))
        m_prev = m_ref[...]
        m_new = m_prev
        for sj in cols:
            m_new = jnp.maximum(m_new, sj)
        alpha = jnp.exp(m_prev - m_new)
        l = alpha * l_ref[...]
        acc = alpha * acc_ref[...]
        for j, sj in enumerate(cols):
            pj = jnp.exp(sj - m_new)
            l = l + pj
            acc = acc + pj * kn[j:j + 1, :KV_LORA]
        o = (acc / l).astype(BF16)
        for h in range(MLA_HEADS):
            oh = _dot(o, wuv_ref[h])
            o_ref[0, :, h * V_HEAD:(h + 1) * V_HEAD] = oh[h * t_new:(h + 1) * t_new]


def _sample_attn(page_table, q_s, knew, cache_lat, cache_ropet, wuv, pages):
    db, n_pages = page_table.shape
    page = cache_lat.shape[1]
    n_buf = SAMPLE_BUFFERS
    n_groups = n_pages // (n_buf * pages)
    assert n_groups * n_buf * pages == n_pages
    rows = q_s.shape[1]
    t_new = rows // MLA_HEADS
    in_specs = [pl.BlockSpec((1, rows, QK_CAT), lambda b, c, pt: (b, 0, 0)),
                pl.BlockSpec((1, 8, QK_CAT), lambda b, c, pt: (b, 0, 0)),
                pl.BlockSpec(memory_space=pl.ANY), pl.BlockSpec(memory_space=pl.ANY),
                pl.BlockSpec((MLA_HEADS, KV_LORA, V_HEAD), lambda b, c, pt: (0, 0, 0))]
    grid_spec = pltpu.PrefetchScalarGridSpec(
        num_scalar_prefetch=1, grid=(db, n_groups), in_specs=in_specs,
        out_specs=pl.BlockSpec((1, t_new, MLA_WIDTH), lambda b, c, pt: (b, 0, 0)),
        scratch_shapes=([pltpu.VMEM((pages, page, KV_LORA), F32)] * n_buf
                        + [pltpu.VMEM((pages, QK_ROPE, page), F32)] * n_buf
                        + [pltpu.SemaphoreType.DMA((2, n_buf)), pltpu.VMEM((rows, 1), F32),
                           pltpu.VMEM((rows, 1), F32), pltpu.VMEM((rows, KV_LORA), F32)]))
    return pl.pallas_call(
        functools.partial(_sample_attn_kernel, pages=pages, page=page, n_groups=n_groups, n_buf=n_buf, t_new=t_new,
                          n_sub=SAMPLE_SUB_BLOCKS),
        out_shape=jax.ShapeDtypeStruct((db, t_new, MLA_WIDTH), F32),
        grid_spec=grid_spec,
        compiler_params=_cparams(dimension_semantics=("arbitrary", "arbitrary")),
        name="sample_attn",
    )(page_table.reshape(-1), q_s, knew, cache_lat, cache_ropet, wuv)


def _gate_block(xc, n, wrg_ref, brg_ref, wig_ref, big_ref, lam_ref):
    cs = slice(n * LRU_BLOCK, (n + 1) * LRU_BLOCK)
    xb = xc.astype(BF16)
    r = jax.nn.sigmoid(_dot(xb, wrg_ref[n]) + brg_ref[:, cs])
    i = jax.nn.sigmoid(_dot(xb, wig_ref[n]) + big_ref[:, cs])
    neg_lam = -lam_ref[:, cs]
    softplus = jnp.maximum(neg_lam, 0.0) + jnp.log(1.0 + jnp.exp(-jnp.abs(neg_lam)))
    log_a = -LRU_C * r * softplus
    a = jnp.exp(log_a)
    gap = 1.0 - a * a
    root = jnp.where(gap > 0.0, gap * lax.rsqrt(gap), 0.0)
    u = root * i * xc
    return a, u


def _gates(xc, wrg_ref, brg_ref, wig_ref, big_ref, lam_ref):
    blocks = [_gate_block(xc[:, n * LRU_BLOCK:(n + 1) * LRU_BLOCK], n, wrg_ref, brg_ref, wig_ref, big_ref, lam_ref)
              for n in range(LRU_BLOCKS)]
    return (jnp.concatenate([a for a, _ in blocks], axis=1), jnp.concatenate([u for _, u in blocks], axis=1))


def _gelu(y):
    return 0.5 * y * (1.0 + jnp.tanh(0.7978845608028654 * (y + 0.044715 * (y * y * y))))


def _route(logits, counts):
    lane = lax.broadcasted_iota(jnp.int32, logits.shape, 1)
    far = jnp.int32(4 * ROUTE_LANES)
    gmask = lane < N_GROUPS
    gl = jnp.where(gmask, logits, NEG_INF)
    gmax = jnp.max(gl, axis=1, keepdims=True)
    gidx = jnp.min(jnp.where(gl == gmax, lane, far), axis=1, keepdims=True)
    g_w = 1.0 / jnp.sum(jnp.where(gmask, jnp.exp(gl - gmax), 0.0), axis=1, keepdims=True)
    lo = N_GROUPS + gidx * EXPERTS_PER_GROUP
    emask = jnp.logical_and(lane >= lo, lane < lo + EXPERTS_PER_GROUP)
    el = jnp.where(emask, logits, NEG_INF)
    emax = jnp.max(el, axis=1, keepdims=True)
    ex = jnp.where(emask, jnp.exp(el - emax), 0.0)
    prob = jnp.where(emask, ex / jnp.sum(ex, axis=1, keepdims=True), -1.0)
    p1 = jnp.max(prob, axis=1, keepdims=True)
    i1 = jnp.min(jnp.where(prob == p1, lane, far), axis=1, keepdims=True)
    rest = jnp.where(lane == i1, -1.0, prob)
    p2 = jnp.max(rest, axis=1, keepdims=True)
    i2 = jnp.min(jnp.where(rest == p2, lane, far), axis=1, keepdims=True)
    den = p1 + p2
    w1 = g_w * p1 / den
    w2 = g_w * p2 / den
    e1 = (i1 - N_GROUPS).astype(F32)
    e2 = (i2 - N_GROUPS).astype(F32)
    n = logits.shape[0]
    hit1 = lane == i1
    hit2 = lane == i2
    chosen = jnp.where(hit1, 1.0, jnp.where(hit2, 1.0, 0.0))
    earlier = jnp.where(lax.broadcasted_iota(jnp.int32, (n, n), 1) < lax.broadcasted_iota(jnp.int32, (n, n), 0), 1.0, 0.0)
    before = _dot(earlier.astype(BF16), chosen.astype(BF16)) + counts
    r1 = jnp.sum(jnp.where(hit1, before, 0.0), axis=1, keepdims=True)
    r2 = jnp.sum(jnp.where(hit2, before, 0.0), axis=1, keepdims=True)
    out = jnp.where(lane == 0, e1, jnp.where(lane == 1, e2, jnp.where(lane == 2, w1, jnp.where(lane == 3, w2,
          jnp.where(lane == 4, r1, jnp.where(lane == 5, r2, 0.0))))))
    return out, counts + jnp.sum(chosen, axis=0, keepdims=True)


ROW_CHUNKS = D_MODEL // LANES


def _store_token_rows(ref, val, first, count, step):
    for c in range(ROW_CHUNKS):
        ref[pl.ds(first * ROW_CHUNKS + c, count, stride=step * ROW_CHUNKS), :] = val[:, c * LANES:(c + 1) * LANES]


def _load_token_rows(ref, count):
    return jnp.concatenate([ref[pl.ds(c, count, stride=ROW_CHUNKS), :] for c in range(ROW_CHUNKS)], axis=1)


def _mix_tail(o_mla, o_lru, x, counts, gmla_ref, glru_ref, wo_ref, gffn_ref, wr_ref, br_ref):
    mixed = jnp.concatenate([_rms(o_mla, gmla_ref[...]), _rms(o_lru, glru_ref[...])], axis=1).astype(BF16)
    h1 = x + _dot(mixed, wo_ref[...])
    t = _rms(h1, gffn_ref[...])
    logits = _dot(t.astype(BF16), wr_ref[...]) + br_ref[...]
    route, counts = _route(logits, counts)
    return h1, t, route, counts


def _mix_prompt_kernel(xbr_ref, ybr_ref, omla_ref, x_ref, wconv_ref, bconv_ref, wrg_ref, brg_ref, wig_ref, big_ref,
                       lam_ref, gmla_ref, glru_ref, wo_ref, gffn_ref, wr_ref, br_ref,
                       h1_ref, t_ref, route_ref, lru_ref, cnt_ref, hc_ref, tail_ref, olru_ref, *, tc):
    @pl.when(pl.program_id(1) == 0)
    def _():
        hc_ref[...] = jnp.zeros(hc_ref.shape, F32)
        tail_ref[...] = jnp.zeros(tail_ref.shape, F32)

    @pl.when(jnp.logical_and(pl.program_id(0) == 0, pl.program_id(1) == 0))
    def _():
        cnt_ref[...] = jnp.zeros(cnt_ref.shape, F32)

    sub8 = lax.broadcasted_iota(jnp.int32, (tc // 8, 8, LRU_BLOCK), 1)
    row = lax.broadcasted_iota(jnp.int32, (tc, LRU_BLOCK), 0)
    for n in range(LRU_BLOCKS):
        cs = slice(n * LRU_BLOCK, (n + 1) * LRU_BLOCK)
        xb = xbr_ref[:, cs]
        tail = tail_ref[:, cs]
        xc = bconv_ref[:, cs] + xb * wconv_ref[CONV_W - 1:CONV_W, cs]
        x3 = jnp.concatenate([tail, xb], axis=0).reshape(tc // 8 + 1, 8, LRU_BLOCK)
        for j in range(1, CONV_W):
            rot = pltpu.roll(x3, j, axis=1)
            sh = jnp.where(sub8 < j, rot[:-1], rot[1:]).reshape(tc, LRU_BLOCK)
            xc = xc + sh * wconv_ref[CONV_W - 1 - j:CONV_W - j, cs]
        tail_ref[:, cs] = xb[tc - 8:tc]

        a, u = _gate_block(xc, n, wrg_ref, brg_ref, wig_ref, big_ref, lam_ref)
        d = 1
        while d < tc:
            keep = row >= d
            u = jnp.where(keep, a * pltpu.roll(u, d, axis=0) + u, u)
            a = jnp.where(keep, a * pltpu.roll(a, d, axis=0), a)
            d *= 2
        hs = a * hc_ref[:, cs] + u
        hc_ref[:, cs] = hs[tc - 1:tc]
        lru_ref[:, cs] = hs[tc - 1:tc]
        olru_ref[:, cs] = _gelu(ybr_ref[:, cs]) * hs

    h1, t, route, counts = _mix_tail(omla_ref[...], olru_ref[...], x_ref[...], cnt_ref[...],
                                     gmla_ref, glru_ref, wo_ref, gffn_ref, wr_ref, br_ref)
    h1_ref[...] = h1
    _store_token_rows(t_ref, t, 0, tc, 1)
    route_ref[...] = route
    cnt_ref[...] = counts


def _mix_weight_specs():
    return [_const_spec((CONV_W, LRU_WIDTH)), _const_spec((1, LRU_WIDTH)),
            _const_spec((LRU_BLOCKS, LRU_BLOCK, LRU_BLOCK)), _const_spec((1, LRU_WIDTH)),
            _const_spec((LRU_BLOCKS, LRU_BLOCK, LRU_BLOCK)), _const_spec((1, LRU_WIDTH)),
            _const_spec((1, LRU_WIDTH)), _const_spec((1, MLA_WIDTH)), _const_spec((1, LRU_WIDTH)),
            _const_spec((D_MODEL, D_MODEL)), _const_spec((1, D_MODEL)),
            _const_spec((D_MODEL, ROUTE_LANES)), _const_spec((1, ROUTE_LANES))]


def _mix_weights(w):
    return (w["w_conv"], w["b_conv"], w["w_rg"], w["b_rg"], w["w_ig"], w["b_ig"], w["lru_lambda"],
            w["g_out_mla"], w["g_out_lru"], w["w_o"], w["g_ffn"], w["w_route"], w["b_route"])


def _mix_prompt(xbr, ybr, omla, x2d, w, batch, seq, tc, n_all):
    nt = seq // tc
    n = batch * seq
    row = lambda width: pl.BlockSpec((tc, width), lambda b, i: (b * nt + i, 0))
    n_w = len(_mix_weight_specs())

    def entry(*refs):
        _mix_prompt_kernel(*refs[:4 + n_w], *refs[5 + n_w:], tc=tc)

    return pl.pallas_call(
        entry,
        out_shape=(jax.ShapeDtypeStruct((n, D_MODEL), F32), jax.ShapeDtypeStruct((n_all * ROW_CHUNKS, LANES), F32),
                   jax.ShapeDtypeStruct((n, ROUTE_LANES), F32), jax.ShapeDtypeStruct((batch, 1, LRU_WIDTH), F32),
                   jax.ShapeDtypeStruct((1, ROUTE_LANES), F32)),
        grid=(batch, nt),
        in_specs=([row(LRU_WIDTH), row(LRU_WIDTH), row(MLA_WIDTH), row(D_MODEL)] + _mix_weight_specs()
                  + [pl.BlockSpec(memory_space=pl.ANY)]),
        out_specs=(row(D_MODEL), pl.BlockSpec((tc * ROW_CHUNKS, LANES), lambda b, i: (b * nt + i, 0)), row(ROUTE_LANES),
                   pl.BlockSpec((None, 1, LRU_WIDTH), lambda b, i: (b, 0, 0)),
                   pl.BlockSpec((1, ROUTE_LANES), lambda b, i: (0, 0))),
        scratch_shapes=[pltpu.VMEM((1, LRU_WIDTH), F32), pltpu.VMEM((8, LRU_WIDTH), F32),
                        pltpu.VMEM((tc, LRU_WIDTH), F32)],
        input_output_aliases={4 + n_w: 1},
        compiler_params=_cparams(dimension_semantics=("arbitrary", "arbitrary")),
        name="mix_prompt",
    )(xbr, ybr, omla, x2d, *_mix_weights(w), jnp.zeros((n_all * ROW_CHUNKS, LANES), F32))


def _mix_sample_kernel(xbr_ref, ybr_ref, omla_ref, x_ref, sconv_ref, slru_ref, cnt_in_ref, wconv_ref, bconv_ref, wrg_ref,
                       brg_ref, wig_ref, big_ref, lam_ref, gmla_ref, glru_ref, wo_ref, gffn_ref, wr_ref, br_ref,
                       h1_ref, t_ref, route_ref, lru_ref, cnt_ref, *, t_new):
    w_ = LRU_WIDTH
    xs = [sconv_ref[:, k * w_:(k + 1) * w_] for k in range(CONV_W - 1)] + [xbr_ref[:, t * w_:(t + 1) * w_] for t in range(t_new)]
    xc = []
    for t in range(t_new):
        acc = bconv_ref[...] + xs[t] * wconv_ref[0:1, :]
        for k in range(1, CONV_W):
            acc = acc + xs[t + k] * wconv_ref[k:k + 1, :]
        xc.append(acc)
    xc = jnp.concatenate(xc, axis=0)
    a, u = _gates(xc, wrg_ref, brg_ref, wig_ref, big_ref, lam_ref)
    nb = xbr_ref.shape[0]
    h = slru_ref[...]
    hs = []
    for t in range(t_new):
        h = a[t * nb:(t + 1) * nb] * h + u[t * nb:(t + 1) * nb]
        hs.append(h)
    lru_ref[...] = h
    hs = jnp.concatenate(hs, axis=0)
    stack = lambda ref, width: jnp.concatenate([ref[:, t * width:(t + 1) * width] for t in range(t_new)], axis=0)
    o_lru = _gelu(stack(ybr_ref, w_)) * hs
    h1, tt, route, counts = _mix_tail(stack(omla_ref, MLA_WIDTH), o_lru, stack(x_ref, D_MODEL), cnt_in_ref[...],
                                      gmla_ref, glru_ref, wo_ref, gffn_ref, wr_ref, br_ref)
    cnt_ref[...] = counts
    for t in range(t_new):
        h1_ref[:, t * D_MODEL:(t + 1) * D_MODEL] = h1[t * nb:(t + 1) * nb]
        _store_token_rows(t_ref, tt[t * nb:(t + 1) * nb], t, nb, t_new)
        route_ref[:, t * ROUTE_LANES:(t + 1) * ROUTE_LANES] = route[t * nb:(t + 1) * nb]


def _mix_sample(t_all, xbr, ybr, omla, x2d, sconv, slru, counts, w, t_new):
    nb = slru.shape[0]
    n_s = nb * t_new
    assert t_all.shape[0] % (n_s * ROW_CHUNKS) == 0
    t_block = t_all.shape[0] // (n_s * ROW_CHUNKS) - 1
    full = lambda a: pl.BlockSpec(a.shape, lambda i: (0, 0))
    ins = (xbr, ybr, omla, x2d, sconv, slru, counts)
    n_w = len(_mix_weight_specs())
    return pl.pallas_call(
        _mix_sample_entry(t_new),
        out_shape=(jax.ShapeDtypeStruct((nb, t_new * D_MODEL), F32),
                   jax.ShapeDtypeStruct(t_all.shape, F32),
                   jax.ShapeDtypeStruct((nb, t_new * ROUTE_LANES), F32), jax.ShapeDtypeStruct((nb, LRU_WIDTH), F32),
                   jax.ShapeDtypeStruct((1, ROUTE_LANES), F32)),
        grid=(1,),
        in_specs=[full(a) for a in ins] + _mix_weight_specs() + [pl.BlockSpec(memory_space=pl.ANY)],
        out_specs=(pl.BlockSpec((nb, t_new * D_MODEL), lambda i: (0, 0)),
                   pl.BlockSpec((n_s * ROW_CHUNKS, LANES), lambda i: (t_block, 0)),
                   pl.BlockSpec((nb, t_new * ROUTE_LANES), lambda i: (0, 0)), pl.BlockSpec((nb, LRU_WIDTH), lambda i: (0, 0)),
                   pl.BlockSpec((1, ROUTE_LANES), lambda i: (0, 0))),
        input_output_aliases={len(ins) + n_w: 1},
        compiler_params=_cparams(dimension_semantics=("arbitrary",)),
        name="mix_sample",
    )(*ins, *_mix_weights(w), t_all)


def _mix_sample_entry(t_new):
    n_in = 7 + len(_mix_weight_specs())

    def entry(*refs):
        _mix_sample_kernel(*refs[:n_in], *refs[n_in + 1:], t_new=t_new)

    return entry


MOE_ROW_GROUP = 32


def _moe_kernel(te_ref, nu_ref, nv_ref, slot_ref, padlo_ref, padhi_ref, t_hbm, wg_ref, wu_ref, wd_ref, y_hbm,
                xa, xb, oa, ob, pair_ref, gsem, ssem, *, tm):
    del te_ref
    j = pl.program_id(0)
    n_used = nu_ref[0]
    xbufs, obufs = (xa, xb), (oa, ob)
    rc = ROW_CHUNKS
    spare = y_hbm.shape[0] - 2 * tm * rc
    n_pairs = slot_ref.shape[0]
    n_tok = n_pairs // 2

    def build_tables():
        pad_mask = 2 * tm - 1

        def spare_fill(r, carry):
            pair_ref[r] = (n_pairs + tm + r) * rc
            return carry

        lax.fori_loop(0, tm, spare_fill, 0)

        def expert_pads(e, carry):
            def pad(s, c):
                pair_ref[s + tm] = (n_pairs + jnp.bitwise_and(s, pad_mask)) * rc
                return c

            return lax.fori_loop(padlo_ref[e], padhi_ref[e], pad, carry)

        lax.fori_loop(0, N_EXPERTS, expert_pads, 0)

        def pairs(i, carry):
            for u in range(16):
                p = i * 16 + u
                pair_ref[slot_ref[p] + tm] = p * rc
            return carry

        lax.fori_loop(0, n_pairs // 16, pairs, 0)

    def rows(jj):
        return pl.multiple_of(nv_ref[jj + 2], MOE_ROW_GROUP)

    def row_groups(n, issue):
        for g in range(tm // MOE_ROW_GROUP):
            @pl.when(n > g * MOE_ROW_GROUP)
            def _(g=g):
                for u in range(MOE_ROW_GROUP):
                    issue(g * MOE_ROW_GROUP + u)

    def start_gather(jj, k):
        def issue(r):
            pr = pair_ref[(jj + 1) * tm + r]
            src = jnp.where(pr >= n_pairs * rc, pr - n_pairs * rc, jnp.where(pr >= n_tok * rc, pr - n_tok * rc, pr))
            pltpu.make_async_copy(t_hbm.at[pl.ds(pl.multiple_of(src, rc), rc)], xbufs[k].at[pl.ds(r * rc, rc)],
                                  gsem.at[k]).start()

        row_groups(rows(jj), issue)

    def wait_gather(jj, k):
        n = pl.multiple_of(rows(jj) * rc, 8)
        pltpu.make_async_copy(t_hbm.at[pl.ds(0, n)], xbufs[k].at[pl.ds(0, n)], gsem.at[k]).wait()

    def start_scatter(jj, k):
        def issue(r):
            dst = pl.multiple_of(pair_ref[(jj + 1) * tm + r], rc)
            pltpu.make_async_copy(obufs[k].at[pl.ds(r * rc, rc)], y_hbm.at[pl.ds(dst, rc)], ssem.at[k]).start()

        row_groups(rows(jj), issue)

    def wait_scatter(jj, k):
        n = pl.multiple_of(rows(jj) * rc, 8)
        pltpu.make_async_copy(obufs[k].at[pl.ds(0, n)], y_hbm.at[pl.ds(0, n)], ssem.at[k]).wait()

    @pl.when(j == 0)
    def _():
        for buf in (xa, xb, oa, ob):
            buf[...] = jnp.zeros(buf.shape, F32)
        pltpu.make_async_copy(oa, y_hbm.at[pl.ds(spare, tm * rc)], ssem.at[0]).start()
        build_tables()
        start_gather(0, 0)

    def tile(k):
        wait_gather(j, k)
        wait_scatter(j - 2, k)
        start_gather(jnp.minimum(j + 1, n_used - 1), 1 - k)
        start_scatter(j - 1, 1 - k)
        x = _load_token_rows(xbufs[k], tm).astype(BF16)
        hg = _dot(x, wg_ref[...].astype(BF16))
        hu = _dot(x, wu_ref[...].astype(BF16))
        act = (hg * jax.nn.sigmoid(hg) * hu).astype(BF16)
        _store_token_rows(obufs[k], _dot(act, wd_ref[...].astype(BF16)), 0, tm, 1)

    def drain(k):
        start_scatter(j, k)
        wait_gather(j, 1 - k)
        wait_scatter(j - 1, 1 - k)
        wait_scatter(j, k)

    for k in range(2):
        pl.when(jnp.logical_and(j < n_used, j % 2 == k))(functools.partial(tile, k))
        pl.when(jnp.logical_and(j == n_used - 1, j % 2 == k))(functools.partial(drain, k))


def _moe(t_all, tile_expert, n_used, tile_rows, slot, pad_lo, pad_hi, w_gate, w_up, w_down, tm):
    n_tiles = tile_expert.shape[0]
    n_tok = t_all.shape[0] // ROW_CHUNKS
    wspec = lambda a, b: pl.BlockSpec((None, a, b), lambda j, te, nu, nv, sl, lo, hi: (te[j], 0, 0))
    grid_spec = pltpu.PrefetchScalarGridSpec(
        num_scalar_prefetch=6, grid=(n_tiles,),
        in_specs=[pl.BlockSpec(memory_space=pl.ANY), wspec(D_MODEL, D_EXPERT), wspec(D_MODEL, D_EXPERT),
                  wspec(D_EXPERT, D_MODEL)],
        out_specs=pl.BlockSpec(memory_space=pl.ANY),
        scratch_shapes=([pltpu.VMEM((tm * ROW_CHUNKS, LANES), F32)] * 4
                        + [pltpu.SMEM(((n_tiles + 1) * tm,), jnp.int32)]
                        + [pltpu.SemaphoreType.DMA((2,)), pltpu.SemaphoreType.DMA((2,))]))
    return pl.pallas_call(
        functools.partial(_moe_kernel, tm=tm),
        out_shape=jax.ShapeDtypeStruct(((2 * n_tok + 2 * tm) * ROW_CHUNKS, LANES), F32),
        grid_spec=grid_spec,
        compiler_params=_cparams(dimension_semantics=("arbitrary",)),
        name="moe",
    )(tile_expert, n_used, tile_rows, slot, pad_lo, pad_hi, t_all, w_gate, w_up, w_down)


def _moe_plan(route_all, counts, tm):
    n = route_all.shape[0]
    n_pairs = 2 * n
    eflat = jnp.concatenate([route_all[:, 0], route_all[:, 1]]).astype(jnp.int32)
    rank = jnp.concatenate([route_all[:, 4], route_all[:, 5]]).astype(jnp.int32)
    counts = counts[0, N_GROUPS:N_GROUPS + N_EXPERTS].astype(jnp.int32)
    tiles_e = (counts + tm - 1) // tm
    tile_end = jnp.cumsum(tiles_e)
    tile_start = tile_end - tiles_e
    n_used = tile_end[-1]
    n_tiles = n_pairs // tm + N_EXPERTS
    slot = tile_start[eflat] * tm + rank
    tid = jnp.arange(n_tiles, dtype=jnp.int32)
    te = jnp.minimum(jnp.sum((tile_end[None, :] <= tid[:, None]).astype(jnp.int32), axis=1), N_EXPERTS - 1)
    te = jnp.where(tid < n_used, te, te[jnp.maximum(n_used - 1, 0)])
    rows = jnp.where(tid < n_used, jnp.clip(counts[te] - (tid - tile_start[te]) * tm, 0, tm), 0)
    rows = (rows + MOE_ROW_GROUP - 1) // MOE_ROW_GROUP * MOE_ROW_GROUP
    tile_rows = jnp.concatenate([jnp.full((2,), tm, jnp.int32), rows.astype(jnp.int32)])
    pad_lo = (tile_start * tm + counts).astype(jnp.int32)
    pad_hi = (tile_end * tm).astype(jnp.int32)
    return te, n_used.reshape(1).astype(jnp.int32), tile_rows, slot.astype(jnp.int32), pad_lo, pad_hi


def _final_kernel(h1_ref, y0_ref, y1_ref, route_ref, p_ref, gple_ref, wpg_ref, bpg_ref, wpp_ref, gfin_ref, y_ref):
    route = route_ref[...]
    tm = h1_ref.shape[0]
    h2 = h1_ref[...] + (route[:, 2:3] * _load_token_rows(y0_ref, tm) + route[:, 3:4] * _load_token_rows(y1_ref, tm))
    gate = jax.nn.sigmoid(_dot(_rms(h2, gple_ref[...]).astype(BF16), wpg_ref[...]) + bpg_ref[...])
    h3 = h2 + gate * _dot(p_ref[...].astype(BF16), wpp_ref[...])
    y_ref[...] = _rms(h3, gfin_ref[...])


def _final(h1, y_pairs, route, p2d, w, blk0, n_all, tm):
    n = h1.shape[0]
    nb_all = n_all // tm
    row = lambda width: pl.BlockSpec((tm, width), lambda i: (i, 0))
    return pl.pallas_call(
        _final_kernel,
        out_shape=jax.ShapeDtypeStruct((n, D_MODEL), F32),
        grid=(n // tm,),
        in_specs=[row(D_MODEL),
                  pl.BlockSpec((tm * ROW_CHUNKS, LANES), lambda i: (blk0 + i, 0)),
                  pl.BlockSpec((tm * ROW_CHUNKS, LANES), lambda i: (nb_all + blk0 + i, 0)),
                  row(ROUTE_LANES), row(PLE_DIM), _const_spec((1, D_MODEL)), _const_spec((D_MODEL, D_MODEL)),
                  _const_spec((1, D_MODEL)), _const_spec((PLE_DIM, D_MODEL)), _const_spec((1, D_MODEL))],
        out_specs=row(D_MODEL),
        compiler_params=_cparams(dimension_semantics=("arbitrary",)),
        name="final",
    )(h1, y_pairs, y_pairs, route, p2d, w["g_ple"], w["w_ple_gate"], w["b_ple_gate"], w["w_ple_proj"], w["g_final"])


def _rope_tables(pos):
    half = QK_ROPE // 2
    inv = ROPE_THETA ** (-jnp.arange(half, dtype=F32) / half)
    ang = pos.astype(F32)[:, None] * inv[None, :]
    pad = jnp.zeros((pos.shape[0], ROPE_PAD - QK_ROPE), F32)
    cos = jnp.concatenate([jnp.cos(ang), jnp.cos(ang), pad], axis=1)
    sin = jnp.concatenate([jnp.sin(ang), jnp.sin(ang), pad], axis=1)
    return cos, sin


def _rot_cols(wr):
    half = QK_ROPE // 2
    return jnp.concatenate([-wr[..., half:], wr[..., :half]], axis=-1)


def _pad_cols(wr):
    return jnp.concatenate([wr, jnp.zeros(wr.shape[:-1] + (ROPE_PAD - QK_ROPE,), wr.dtype)], axis=-1)


def _prep_weights(g_mix, w_in, g_q, w_uq, g_kv, w_ukv, w_conv, b_conv, w_rg, b_rg, w_ig, b_ig, lru_lambda,
                  g_out_mla, g_out_lru, w_o, g_ffn, w_group, b_group, w_router, b_router, g_ple, w_ple_gate,
                  b_ple_gate, w_ple_proj, g_final):
    row = lambda v: v.reshape(1, -1).astype(F32)
    s0, s1, s2, s3 = Q_LORA, Q_LORA + KV_LORA, Q_LORA + KV_LORA + QK_ROPE, Q_LORA + KV_LORA + QK_ROPE + LRU_WIDTH
    w_in, w_uq, w_ukv = w_in.astype(BF16), w_uq.astype(BF16), w_ukv.astype(BF16)
    kr = w_in[:, s1:s2]
    w_in_ext = jnp.concatenate([w_in[:, :s1], _pad_cols(kr), _pad_cols(_rot_cols(kr)), w_in[:, s2:]], axis=1)
    uq = w_uq.reshape(Q_LORA, MLA_HEADS, QK_NOPE + QK_ROPE)
    uq_r = uq[..., QK_NOPE:]
    w_uq_ext = jnp.concatenate([uq[..., :QK_NOPE].reshape(Q_LORA, -1), _pad_cols(uq_r).reshape(Q_LORA, -1),
                                _pad_cols(_rot_cols(uq_r)).reshape(Q_LORA, -1)], axis=1)
    w_uk = jnp.transpose(w_ukv[..., :QK_NOPE], (1, 2, 0))
    w_uv = jnp.transpose(w_ukv[..., QK_NOPE:], (1, 0, 2))
    w_route = jnp.concatenate([w_group, w_router, jnp.zeros((D_MODEL, ROUTE_LANES - N_GROUPS - N_EXPERTS), F32)], axis=1)
    b_route = jnp.concatenate([b_group, b_router, jnp.zeros((ROUTE_LANES - N_GROUPS - N_EXPERTS,), F32)])
    return dict(
        g_mix=row(g_mix), w_in=w_in_ext.astype(BF16), g_q=row(g_q), w_uq=w_uq_ext.astype(BF16), g_kv=row(g_kv),
        w_uk=w_uk.astype(BF16), w_uv=w_uv.astype(BF16), w_conv=w_conv.astype(F32), b_conv=row(b_conv),
        w_rg=w_rg.astype(BF16), b_rg=row(b_rg), w_ig=w_ig.astype(BF16), b_ig=row(b_ig), lru_lambda=row(lru_lambda),
        g_out_mla=row(g_out_mla), g_out_lru=row(g_out_lru), w_o=w_o.astype(BF16), g_ffn=row(g_ffn),
        w_route=w_route.astype(BF16), b_route=row(b_route), g_ple=row(g_ple), w_ple_gate=w_ple_gate.astype(BF16),
        b_ple_gate=row(b_ple_gate), w_ple_proj=w_ple_proj.astype(BF16), g_final=row(g_final))


TM_PROJ = 512
TM_FINAL = 256
TQ_ATTN = 256
ATTN_HEAD_GROUP = 4
TC_MIX = 256
TM_MOE = 256
PAGES_PER_STEP = 32
SAMPLE_BUFFERS = 4
SAMPLE_SUB_BLOCKS = 4


def kernel(x_prompt, x_sample, p_prompt, p_sample, cache_latent, cache_krope, state_lru, state_conv, page_table, g_mix, w_in, g_q, w_uq, g_kv, w_ukv, w_conv, b_conv, w_rg, b_rg, w_ig, b_ig, lru_lambda, g_out_mla, g_out_lru, w_o, g_ffn, w_group, b_group, w_router, b_router, w_gate, w_up, w_down, g_ple, w_ple_gate, b_ple_gate, w_ple_proj, g_final):
    assert w_in.shape[0] == 1, "single trunk layer"
    batch, seq, _ = x_prompt.shape
    db, t_new, _ = x_sample.shape
    page = cache_latent.shape[2]
    past_len = page_table.shape[1] * page
    n_p, n_s = batch * seq, db * t_new
    n_all = n_p + n_s

    w = _prep_weights(g_mix[0], w_in[0], g_q[0], w_uq[0], g_kv[0], w_ukv[0], w_conv[0], b_conv[0], w_rg[0], b_rg[0],
                      w_ig[0], b_ig[0], lru_lambda[0], g_out_mla[0], g_out_lru[0], w_o[0], g_ffn[0], w_group[0],
                      b_group[0], w_router[0], b_router[0], g_ple[0], w_ple_gate[0], b_ple_gate[0], w_ple_proj[0],
                      g_final)

    xp = x_prompt.reshape(n_p, D_MODEL)
    cos_p, sin_p = _rope_tables(jnp.arange(seq))
    qcat_p, kcat_p, lat_p, krope_p, xbr_p, ybr_p = _proj(xp, cos_p, sin_p, seq // TM_PROJ, w, TM_PROJ)
    omla_p = _prompt_attn(qcat_p, kcat_p, w["w_uv"], batch, seq, TQ_ATTN)
    h1_p, t_p, route_p, lru_p, counts_p = _mix_prompt(xbr_p, ybr_p, omla_p, xp, w, batch, seq, TC_MIX, n_all)

    xs = x_sample.reshape(n_s, D_MODEL)
    tm_s = min(TM_PROJ, n_s)
    cos_s, sin_s = _rope_tables(past_len + (jnp.arange(tm_s) % t_new))
    qcat_s, kcat_s, lat_s, krope_s, xbr_s, ybr_s = _proj(xs, cos_s, sin_s, 1, w, tm_s)
    q_s = qcat_s.reshape(db, t_new, MLA_HEADS, QK_CAT).transpose(0, 2, 1, 3).reshape(db, MLA_HEADS * t_new, QK_CAT)
    knew = jnp.pad(kcat_s.reshape(db, t_new, QK_CAT), ((0, 0), (0, 8 - t_new), (0, 0)))
    cache_ropet = jnp.swapaxes(cache_krope[0], 1, 2)
    omla_s = _sample_attn(page_table, q_s, knew, cache_latent[0], cache_ropet, w["w_uv"], PAGES_PER_STEP)
    h1_s, t_all, route_s, lru_s, counts = _mix_sample(
        t_p, xbr_s.reshape(db, t_new * LRU_WIDTH), ybr_s.reshape(db, t_new * LRU_WIDTH),
        omla_s.reshape(db, t_new * MLA_WIDTH), x_sample.reshape(db, t_new * D_MODEL),
        state_conv[0].reshape(db, (CONV_W - 1) * LRU_WIDTH), state_lru[0], counts_p, w, t_new)
    h1_s = h1_s.reshape(n_s, D_MODEL)
    route_s = route_s.reshape(n_s, ROUTE_LANES)

    route_all = jnp.concatenate([route_p, route_s], axis=0)
    te, n_used, tile_rows, slot, pad_lo, pad_hi = _moe_plan(route_all, counts, TM_MOE)
    y_pairs = _moe(t_all, te, n_used, tile_rows, slot, pad_lo, pad_hi, w_gate[0], w_up[0], w_down[0], TM_MOE)

    y_p = _final(h1_p, y_pairs, route_p, p_prompt[0].reshape(n_p, PLE_DIM), w, 0, n_all, TM_FINAL)
    y_s = _final(h1_s, y_pairs, route_s, p_sample[0].reshape(n_s, PLE_DIM), w, n_p // TM_FINAL, n_all, TM_FINAL)

    new_conv_p = xbr_p.reshape(batch, seq, LRU_WIDTH)[:, seq - (CONV_W - 1):]
    hist = jnp.concatenate([state_conv[0], xbr_s.reshape(db, t_new, LRU_WIDTH)], axis=1)
    new_conv_s = hist[:, hist.shape[1] - (CONV_W - 1):]
    return (y_p.reshape(batch, seq, D_MODEL), y_s.reshape(db, t_new, D_MODEL),
            lat_p.reshape(1, batch, seq, KV_LORA), krope_p.reshape(1, batch, seq, QK_ROPE),
            lru_p.reshape(1, batch, LRU_WIDTH), new_conv_p[None],
            lat_s.reshape(1, db, t_new, KV_LORA), krope_s.reshape(1, db, t_new, QK_ROPE),
            lru_s[None], new_conv_s[None])
```

```python
import functools

import jax
import jax.numpy as jnp
from jax import lax
from jax.experimental import pallas as pl
from jax.experimental.pallas import tpu as pltpu

F32 = jnp.float32
BF16 = jnp.bfloat16

D_MODEL = 2048
MLA_HEADS = 8
V_HEAD = 128
MLA_WIDTH = MLA_HEADS * V_HEAD
LRU_WIDTH = D_MODEL - MLA_WIDTH
QK_NOPE = 128
QK_ROPE = 64
Q_LORA = 512
KV_LORA = 256
ROPE_THETA = 10000.0
SM_SCALE = (QK_NOPE + QK_ROPE) ** -0.5
NEG_INF = -1e30
LRU_BLOCKS = 8
LRU_BLOCK = LRU_WIDTH // LRU_BLOCKS
CONV_W = 4
LRU_C = 8.0
N_GROUPS = 4
EXPERTS_PER_GROUP = 8
N_EXPERTS = N_GROUPS * EXPERTS_PER_GROUP
D_EXPERT = 512
PLE_DIM = 256
EPS = 1e-6

LANES = 128
ROPE_PAD = LANES
QK_CAT = KV_LORA + ROPE_PAD
C_Q0, C_KV0, C_KRA0, C_KRB0, C_X0, C_Y0, IN_EXT = 0, 512, 768, 896, 1024, 2048, 3072
ROUTE_LANES = LANES
VMEM_LIMIT = 56 * 1024 * 1024


def _cparams(**kw):
    return pltpu.CompilerParams(vmem_limit_bytes=VMEM_LIMIT, **kw)


def _rms(x, g):
    return x * lax.rsqrt(jnp.mean(x * x, axis=-1, keepdims=True) + EPS) * g


def _dot(a, b):
    return jnp.dot(a, b, preferred_element_type=F32)


def _dot_nt(a, b):
    return lax.dot_general(a, b, (((1,), (1,)), ((), ())), preferred_element_type=F32)


def _const_spec(shape):
    nd = len(shape)
    return pl.BlockSpec(shape, lambda *_: (0,) * nd, pipeline_mode=pl.Buffered(1))


def _proj_kernel(x_ref, cos_ref, sin_ref, gmix_ref, win_ref, gq_ref, wuq_ref, gkv_ref, wuk_ref,
                 qcat_ref, kcat_ref, lat_ref, krope_ref, xbr_ref, ybr_ref):
    u = _rms(x_ref[...], gmix_ref[...]).astype(BF16)
    z = _dot(u, win_ref[...])
    xbr_ref[...] = z[:, C_X0:C_Y0]
    ybr_ref[...] = z[:, C_Y0:IN_EXT]
    cos = cos_ref[...]
    sin = sin_ref[...]
    lat = _rms(z[:, C_KV0:C_KRA0], gkv_ref[...])
    kr = z[:, C_KRA0:C_KRB0] * cos + z[:, C_KRB0:C_X0] * sin
    lat_ref[...] = lat
    krope_ref[...] = kr[:, :QK_ROPE]
    kcat_ref[:, 0:KV_LORA] = lat.astype(BF16)
    kcat_ref[:, KV_LORA:QK_CAT] = kr.astype(BF16)
    qn = _rms(z[:, C_Q0:C_KV0], gq_ref[...]).astype(BF16)
    q = _dot(qn, wuq_ref[...])
    ra0 = MLA_HEADS * QK_NOPE
    rb0 = ra0 + MLA_HEADS * ROPE_PAD
    for h in range(MLA_HEADS):
        ql = _dot(q[:, h * QK_NOPE:(h + 1) * QK_NOPE].astype(BF16), wuk_ref[h])
        qr = (q[:, ra0 + h * ROPE_PAD:ra0 + (h + 1) * ROPE_PAD] * cos
              + q[:, rb0 + h * ROPE_PAD:rb0 + (h + 1) * ROPE_PAD] * sin)
        qcat_ref[:, h * QK_CAT:h * QK_CAT + KV_LORA] = ql.astype(BF16)
        qcat_ref[:, h * QK_CAT + KV_LORA:(h + 1) * QK_CAT] = qr.astype(BF16)


def _proj(x2d, cos, sin, pos_blocks, w, tm):
    n = x2d.shape[0]
    row = lambda width: pl.BlockSpec((tm, width), lambda i: (i, 0))
    pos = pl.BlockSpec((tm, ROPE_PAD), lambda i: (i % pos_blocks, 0))
    out_shape = (
        jax.ShapeDtypeStruct((n, MLA_HEADS * QK_CAT), BF16),
        jax.ShapeDtypeStruct((n, QK_CAT), BF16),
        jax.ShapeDtypeStruct((n, KV_LORA), F32),
        jax.ShapeDtypeStruct((n, QK_ROPE), F32),
        jax.ShapeDtypeStruct((n, LRU_WIDTH), F32),
        jax.ShapeDtypeStruct((n, LRU_WIDTH), F32),
    )
    return pl.pallas_call(
        _proj_kernel,
        out_shape=out_shape,
        grid=(n // tm,),
        in_specs=[row(D_MODEL), pos, pos, _const_spec((1, D_MODEL)), _const_spec((D_MODEL, IN_EXT)),
                  _const_spec((1, Q_LORA)), _const_spec((Q_LORA, 3 * MLA_HEADS * LANES)),
                  _const_spec((1, KV_LORA)), _const_spec((MLA_HEADS, QK_NOPE, KV_LORA))],
        out_specs=(row(MLA_HEADS * QK_CAT), row(QK_CAT), row(KV_LORA), row(QK_ROPE), row(LRU_WIDTH), row(LRU_WIDTH)),
        compiler_params=_cparams(dimension_semantics=("arbitrary",)),
        name="proj",
    )(x2d, cos, sin, w["g_mix"], w["w_in"], w["g_q"], w["w_uq"], w["g_kv"], w["w_uk"])


def _prompt_attn_kernel(q_ref, k_ref, wuv_ref, o_ref, *scratch, tq, group):
    m_refs, l_refs, acc_refs = scratch[:MLA_HEADS], scratch[MLA_HEADS:2 * MLA_HEADS], scratch[2 * MLA_HEADS:]
    qi = pl.program_id(1)
    for h in range(MLA_HEADS):
        m_refs[h][...] = jnp.full(m_refs[h].shape, NEG_INF, F32)
        l_refs[h][...] = jnp.zeros(l_refs[h].shape, F32)
        acc_refs[h][...] = jnp.zeros(acc_refs[h].shape, F32)

    def step(kb, masked):
        k = k_ref[pl.ds(pl.multiple_of(kb * tq, tq), tq), :]
        v = k[:, :KV_LORA]
        if masked:
            row = lax.broadcasted_iota(jnp.int32, (tq, tq), 0)
            col = lax.broadcasted_iota(jnp.int32, (tq, tq), 1)
            keep = col <= row
        for h0 in range(0, MLA_HEADS, group):
            heads = range(h0, h0 + group)
            scores = [_dot_nt(q_ref[:, h * QK_CAT:(h + 1) * QK_CAT], k) * SM_SCALE for h in heads]
            probs = []
            for h, s in zip(heads, scores):
                if masked:
                    s = jnp.where(keep, s, NEG_INF)
                m_prev = m_refs[h][...]
                m_new = jnp.maximum(m_prev, jnp.max(s, axis=1, keepdims=True))
                alpha = jnp.exp(m_prev - m_new)
                p = jnp.exp(s - jnp.concatenate([m_new] * (tq // LANES), axis=1))
                l_refs[h][...] = alpha * l_refs[h][...] + jnp.sum(p, axis=1, keepdims=True)
                m_refs[h][...] = m_new
                probs.append((alpha, p.astype(BF16)))
            for h, (alpha, p) in zip(heads, probs):
                acc_refs[h][...] = jnp.concatenate([alpha] * (KV_LORA // LANES), axis=1) * acc_refs[h][...] + _dot(p, v)

    def body(kb, carry):
        step(kb, False)
        return carry

    lax.fori_loop(0, qi, body, 0)
    step(qi, True)
    for h in range(MLA_HEADS):
        o = acc_refs[h][...] / jnp.concatenate([l_refs[h][...]] * (KV_LORA // LANES), axis=1)
        o_ref[:, h * V_HEAD:(h + 1) * V_HEAD] = _dot(o.astype(BF16), wuv_ref[h])


def _prompt_attn(qcat, kcat, wuv, batch, seq, tq):
    nq = seq // tq
    return pl.pallas_call(
        functools.partial(_prompt_attn_kernel, tq=tq, group=ATTN_HEAD_GROUP),
        out_shape=jax.ShapeDtypeStruct((batch * seq, MLA_WIDTH), F32),
        grid=(batch, nq),
        in_specs=[pl.BlockSpec((tq, MLA_HEADS * QK_CAT), lambda b, i: (b * nq + i, 0)),
                  pl.BlockSpec((seq, QK_CAT), lambda b, i: (b, 0)),
                  _const_spec((MLA_HEADS, KV_LORA, V_HEAD))],
        out_specs=pl.BlockSpec((tq, MLA_WIDTH), lambda b, i: (b * nq + i, 0)),
        scratch_shapes=([pltpu.VMEM((tq, LANES), F32)] * (2 * MLA_HEADS) + [pltpu.VMEM((tq, KV_LORA), F32)] * MLA_HEADS),
        compiler_params=_cparams(dimension_semantics=("arbitrary", "arbitrary")),
        name="prompt_attn",
    )(qcat, kcat, wuv)


def _sample_attn_kernel(pt_ref, q_ref, knew_ref, lat_hbm, ropet_hbm, wuv_ref, o_ref, *scratch,
                        pages, page, n_groups, n_buf, t_new, n_sub):
    lat_bufs, rope_bufs = scratch[:n_buf], scratch[n_buf:2 * n_buf]
    sem, m_ref, l_ref, acc_ref = scratch[2 * n_buf:]
    c = pl.program_id(1)
    step = pl.program_id(0) * n_groups + c
    n_steps = pl.num_programs(0) * n_groups
    bufs = tuple(zip(lat_bufs, rope_bufs))

    def start_chunk(chunk, k):
        for g in range(pages):
            pid = pt_ref[chunk * pages + g]
            pltpu.make_async_copy(lat_hbm.at[pid], bufs[k][0].at[g], sem.at[0, k]).start()
            pltpu.make_async_copy(ropet_hbm.at[pid], bufs[k][1].at[g], sem.at[1, k]).start()

    def wait_chunk(k):
        pltpu.make_async_copy(lat_hbm.at[pl.ds(0, pages)], bufs[k][0], sem.at[0, k]).wait()
        pltpu.make_async_copy(ropet_hbm.at[pl.ds(0, pages)], bufs[k][1], sem.at[1, k]).wait()

    q = q_ref[0]
    q_lat = q[:, :KV_LORA]
    q_rope = q[:, KV_LORA:KV_LORA + QK_ROPE]
    ps = pages // n_sub

    def attend(k):
        latbuf, ropebuf = bufs[k]
        lats, scores, parts = [], [], []
        for sb in range(n_sub):
            lat = latbuf[sb * ps:(sb + 1) * ps].reshape(ps * page, KV_LORA).astype(BF16)
            s_rope = jnp.concatenate([_dot(q_rope, ropebuf[sb * ps + g].astype(BF16)) for g in range(ps)], axis=1)
            lats.append(lat)
            scores.append((_dot_nt(q_lat, lat) + s_rope) * SM_SCALE)
        for sb in range(n_sub):
            m_i = jnp.max(scores[sb], axis=1, keepdims=True)
            p = jnp.exp(scores[sb] - m_i)
            parts.append((m_i, jnp.sum(p, axis=1, keepdims=True), p.astype(BF16)))
        parts = [(m_i, l_i, _dot(p, lats[sb])) for sb, (m_i, l_i, p) in enumerate(parts)]
        m_prev = m_ref[...]
        m_new = m_prev
        for m_i, _, _ in parts:
            m_new = jnp.maximum(m_new, m_i)
        alpha = jnp.exp(m_prev - m_new)
        l = alpha * l_ref[...]
        acc = alpha * acc_ref[...]
        for m_i, l_i, o_i in parts:
            w_i = jnp.exp(m_i - m_new)
            l = l + w_i * l_i
            acc = acc + w_i * o_i
        l_ref[...] = l
        acc_ref[...] = acc
        m_ref[...] = m_new

    @pl.when(step == 0)
    def _():
        for k in range(n_buf):
            start_chunk(k, k)

    @pl.when(c == 0)
    def _():
        m_ref[...] = jnp.full(m_ref.shape, NEG_INF, F32)
        l_ref[...] = jnp.zeros(l_ref.shape, F32)
        acc_ref[...] = jnp.zeros(acc_ref.shape, F32)

    last_chunk = n_buf * n_steps - 1
    for k in range(n_buf):
        wait_chunk(k)
        attend(k)
        start_chunk(jnp.minimum(n_buf * (step + 1) + k, last_chunk), k)

    @pl.when(step == n_steps - 1)
    def _():
        for k in range(n_buf):
            wait_chunk(k)

    @pl.when(c == n_groups - 1)
    def _():
        qf = q.astype(F32)
        kn = knew_ref[0].astype(F32)
        tok = lax.broadcasted_iota(jnp.int32, (q.shape[0], 1), 0) & (t_new - 1)
        cols = []
        for j in range(t_new):
            sj = jnp.sum(qf * kn[j:j + 1, :], axis=1, keepdims=True) * SM_SCALE
            cols.append(jnp.where(tok >= j, sj, NEG_INF))
        m_prev = m_ref[...]
        m_new = m_prev
        for sj in cols:
            m_new = jnp.maximum(m_new, sj)
        alpha = jnp.exp(m_prev - m_new)
        l = alpha * l_ref[...]
        acc = alpha * acc_ref[...]
        for j, sj in enumerate(cols):
            pj = jnp.exp(sj - m_new)
            l = l + pj
            acc = acc + pj * kn[j:j + 1, :KV_LORA]
        o = (acc / l).astype(BF16)
        for h in range(MLA_HEADS):
            oh = _dot(o, wuv_ref[h])
            o_ref[0, :, h * V_HEAD:(h + 1) * V_HEAD] = oh[h * t_new:(h + 1) * t_new]


def _sample_attn(page_table, q_s, knew, cache_lat, cache_ropet, wuv, pages):
    db, n_pages = page_table.shape
    page = cache_lat.shape[1]
    n_buf = SAMPLE_BUFFERS
    n_groups = n_pages // (n_buf * pages)
    assert n_groups * n_buf * pages == n_pages
    rows = q_s.shape[1]
    t_new = rows // MLA_HEADS
    in_specs = [pl.BlockSpec((1, rows, QK_CAT), lambda b, c, pt: (b, 0, 0)),
                pl.BlockSpec((1, 8, QK_CAT), lambda b, c, pt: (b, 0, 0)),
                pl.BlockSpec(memory_space=pl.ANY), pl.BlockSpec(memory_space=pl.ANY),
                pl.BlockSpec((MLA_HEADS, KV_LORA, V_HEAD), lambda b, c, pt: (0, 0, 0))]
    grid_spec = pltpu.PrefetchScalarGridSpec(
        num_scalar_prefetch=1, grid=(db, n_groups), in_specs=in_specs,
        out_specs=pl.BlockSpec((1, t_new, MLA_WIDTH), lambda b, c, pt: (b, 0, 0)),
        scratch_shapes=([pltpu.VMEM((pages, page, KV_LORA), F32)] * n_buf
                        + [pltpu.VMEM((pages, QK_ROPE, page), F32)] * n_buf
                        + [pltpu.SemaphoreType.DMA((2, n_buf)), pltpu.VMEM((rows, 1), F32),
                           pltpu.VMEM((rows, 1), F32), pltpu.VMEM((rows, KV_LORA), F32)]))
    return pl.pallas_call(
        functools.partial(_sample_attn_kernel, pages=pages, page=page, n_groups=n_groups, n_buf=n_buf, t_new=t_new,
                          n_sub=SAMPLE_SUB_BLOCKS),
        out_shape=jax.ShapeDtypeStruct((db, t_new, MLA_WIDTH), F32),
        grid_spec=grid_spec,
        compiler_params=_cparams(dimension_semantics=("arbitrary", "arbitrary")),
        name="sample_attn",
    )(page_table.reshape(-1), q_s, knew, cache_lat, cache_ropet, wuv)


def _gate_block(xc, n, wrg_ref, brg_ref, wig_ref, big_ref, lam_ref):
    cs = slice(n * LRU_BLOCK, (n + 1) * LRU_BLOCK)
    xb = xc.astype(BF16)
    r = jax.nn.sigmoid(_dot(xb, wrg_ref[n]) + brg_ref[:, cs])
    i = jax.nn.sigmoid(_dot(xb, wig_ref[n]) + big_ref[:, cs])
    neg_lam = -lam_ref[:, cs]
    softplus = jnp.maximum(neg_lam, 0.0) + jnp.log(1.0 + jnp.exp(-jnp.abs(neg_lam)))
    log_a = -LRU_C * r * softplus
    a = jnp.exp(log_a)
    gap = 1.0 - a * a
    root = jnp.where(gap > 0.0, gap * lax.rsqrt(gap), 0.0)
    u = root * i * xc
    return a, u


def _gates(xc, wrg_ref, brg_ref, wig_ref, big_ref, lam_ref):
    blocks = [_gate_block(xc[:, n * LRU_BLOCK:(n + 1) * LRU_BLOCK], n, wrg_ref, brg_ref, wig_ref, big_ref, lam_ref)
              for n in range(LRU_BLOCKS)]
    return (jnp.concatenate([a for a, _ in blocks], axis=1), jnp.concatenate([u for _, u in blocks], axis=1))


def _gelu(y):
    return 0.5 * y * (1.0 + jnp.tanh(0.7978845608028654 * (y + 0.044715 * (y * y * y))))


def _route(logits, counts):
    lane = lax.broadcasted_iota(jnp.int32, logits.shape, 1)
    far = jnp.int32(4 * ROUTE_LANES)
    gmask = lane < N_GROUPS
    gl = jnp.where(gmask, logits, NEG_INF)
    gmax = jnp.max(gl, axis=1, keepdims=True)
    gidx = jnp.min(jnp.where(gl == gmax, lane, far), axis=1, keepdims=True)
    g_w = 1.0 / jnp.sum(jnp.where(gmask, jnp.exp(gl - gmax), 0.0), axis=1, keepdims=True)
    lo = N_GROUPS + gidx * EXPERTS_PER_GROUP
    emask = jnp.logical_and(lane >= lo, lane < lo + EXPERTS_PER_GROUP)
    el = jnp.where(emask, logits, NEG_INF)
    emax = jnp.max(el, axis=1, keepdims=True)
    ex = jnp.where(emask, jnp.exp(el - emax), 0.0)
    prob = jnp.where(emask, ex / jnp.sum(ex, axis=1, keepdims=True), -1.0)
    p1 = jnp.max(prob, axis=1, keepdims=True)
    i1 = jnp.min(jnp.where(prob == p1, lane, far), axis=1, keepdims=True)
    rest = jnp.where(lane == i1, -1.0, prob)
    p2 = jnp.max(rest, axis=1, keepdims=True)
    i2 = jnp.min(jnp.where(rest == p2, lane, far), axis=1, keepdims=True)
    den = p1 + p2
    w1 = g_w * p1 / den
    w2 = g_w * p2 / den
    e1 = (i1 - N_GROUPS).astype(F32)
    e2 = (i2 - N_GROUPS).astype(F32)
    n = logits.shape[0]
    hit1 = lane == i1
    hit2 = lane == i2
    chosen = jnp.where(hit1, 1.0, jnp.where(hit2, 1.0, 0.0))
    earlier = jnp.where(lax.broadcasted_iota(jnp.int32, (n, n), 1) < lax.broadcasted_iota(jnp.int32, (n, n), 0), 1.0, 0.0)
    before = _dot(earlier.astype(BF16), chosen.astype(BF16)) + counts
    r1 = jnp.sum(jnp.where(hit1, before, 0.0), axis=1, keepdims=True)
    r2 = jnp.sum(jnp.where(hit2, before, 0.0), axis=1, keepdims=True)
    out = jnp.where(lane == 0, e1, jnp.where(lane == 1, e2, jnp.where(lane == 2, w1, jnp.where(lane == 3, w2,
          jnp.where(lane == 4, r1, jnp.where(lane == 5, r2, 0.0))))))
    return out, counts + jnp.sum(chosen, axis=0, keepdims=True)


ROW_CHUNKS = D_MODEL // LANES


def _store_token_rows(ref, val, first, count, step):
    for c in range(ROW_CHUNKS):
        ref[pl.ds(first * ROW_CHUNKS + c, count, stride=step * ROW_CHUNKS), :] = val[:, c * LANES:(c + 1) * LANES]


def _load_token_rows(ref, count):
    return jnp.concatenate([ref[pl.ds(c, count, stride=ROW_CHUNKS), :] for c in range(ROW_CHUNKS)], axis=1)


def _mix_tail(o_mla, o_lru, x, counts, gmla_ref, glru_ref, wo_ref, gffn_ref, wr_ref, br_ref):
    mixed = jnp.concatenate([_rms(o_mla, gmla_ref[...]), _rms(o_lru, glru_ref[...])], axis=1).astype(BF16)
    h1 = x + _dot(mixed, wo_ref[...])
    t = _rms(h1, gffn_ref[...])
    logits = _dot(t.astype(BF16), wr_ref[...]) + br_ref[...]
    route, counts = _route(logits, counts)
    return h1, t, route, counts


def _mix_prompt_kernel(xbr_ref, ybr_ref, omla_ref, x_ref, wconv_ref, bconv_ref, wrg_ref, brg_ref, wig_ref, big_ref,
                       lam_ref, gmla_ref, glru_ref, wo_ref, gffn_ref, wr_ref, br_ref,
                       h1_ref, t_ref, route_ref, lru_ref, cnt_ref, hc_ref, tail_ref, olru_ref, *, tc):
    @pl.when(pl.program_id(1) == 0)
    def _():
        hc_ref[...] = jnp.zeros(hc_ref.shape, F32)
        tail_ref[...] = jnp.zeros(tail_ref.shape, F32)

    @pl.when(jnp.logical_and(pl.program_id(0) == 0, pl.program_id(1) == 0))
    def _():
        cnt_ref[...] = jnp.zeros(cnt_ref.shape, F32)

    sub8 = lax.broadcasted_iota(jnp.int32, (tc // 8, 8, LRU_BLOCK), 1)
    row = lax.broadcasted_iota(jnp.int32, (tc, LRU_BLOCK), 0)
    for n in range(LRU_BLOCKS):
        cs = slice(n * LRU_BLOCK, (n + 1) * LRU_BLOCK)
        xb = xbr_ref[:, cs]
        tail = tail_ref[:, cs]
        xc = bconv_ref[:, cs] + xb * wconv_ref[CONV_W - 1:CONV_W, cs]
        x3 = jnp.concatenate([tail, xb], axis=0).reshape(tc // 8 + 1, 8, LRU_BLOCK)
        for j in range(1, CONV_W):
            rot = pltpu.roll(x3, j, axis=1)
            sh = jnp.where(sub8 < j, rot[:-1], rot[1:]).reshape(tc, LRU_BLOCK)
            xc = xc + sh * wconv_ref[CONV_W - 1 - j:CONV_W - j, cs]
        tail_ref[:, cs] = xb[tc - 8:tc]

        a, u = _gate_block(xc, n, wrg_ref, brg_ref, wig_ref, big_ref, lam_ref)
        d = 1
        while d < tc:
            keep = row >= d
            u = jnp.where(keep, a * pltpu.roll(u, d, axis=0) + u, u)
            a = jnp.where(keep, a * pltpu.roll(a, d, axis=0), a)
            d *= 2
        hs = a * hc_ref[:, cs] + u
        hc_ref[:, cs] = hs[tc - 1:tc]
        lru_ref[:, cs] = hs[tc - 1:tc]
        olru_ref[:, cs] = _gelu(ybr_ref[:, cs]) * hs

    h1, t, route, counts = _mix_tail(omla_ref[...], olru_ref[...], x_ref[...], cnt_ref[...],
                                     gmla_ref, glru_ref, wo_ref, gffn_ref, wr_ref, br_ref)
    h1_ref[...] = h1
    _store_token_rows(t_ref, t, 0, tc, 1)
    route_ref[...] = route
    cnt_ref[...] = counts


def _mix_weight_specs():
    return [_const_spec((CONV_W, LRU_WIDTH)), _const_spec((1, LRU_WIDTH)),
            _const_spec((LRU_BLOCKS, LRU_BLOCK, LRU_BLOCK)), _const_spec((1, LRU_WIDTH)),
            _const_spec((LRU_BLOCKS, LRU_BLOCK, LRU_BLOCK)), _const_spec((1, LRU_WIDTH)),
            _const_spec((1, LRU_WIDTH)), _const_spec((1, MLA_WIDTH)), _const_spec((1, LRU_WIDTH)),
            _const_spec((D_MODEL, D_MODEL)), _const_spec((1, D_MODEL)),
            _const_spec((D_MODEL, ROUTE_LANES)), _const_spec((1, ROUTE_LANES))]


def _mix_weights(w):
    return (w["w_conv"], w["b_conv"], w["w_rg"], w["b_rg"], w["w_ig"], w["b_ig"], w["lru_lambda"],
            w["g_out_mla"], w["g_out_lru"], w["w_o"], w["g_ffn"], w["w_route"], w["b_route"])


def _mix_prompt(xbr, ybr, omla, x2d, w, batch, seq, tc, n_all):
    nt = seq // tc
    n = batch * seq
    row = lambda width: pl.BlockSpec((tc, width), lambda b, i: (b * nt + i, 0))
    n_w = len(_mix_weight_specs())

    def entry(*refs):
        _mix_prompt_kernel(*refs[:4 + n_w], *refs[5 + n_w:], tc=tc)

    return pl.pallas_call(
        entry,
        out_shape=(jax.ShapeDtypeStruct((n, D_MODEL), F32), jax.ShapeDtypeStruct((n_all * ROW_CHUNKS, LANES), F32),
                   jax.ShapeDtypeStruct((n, ROUTE_LANES), F32), jax.ShapeDtypeStruct((batch, 1, LRU_WIDTH), F32),
                   jax.ShapeDtypeStruct((1, ROUTE_LANES), F32)),
        grid=(batch, nt),
        in_specs=([row(LRU_WIDTH), row(LRU_WIDTH), row(MLA_WIDTH), row(D_MODEL)] + _mix_weight_specs()
                  + [pl.BlockSpec(memory_space=pl.ANY)]),
        out_specs=(row(D_MODEL), pl.BlockSpec((tc * ROW_CHUNKS, LANES), lambda b, i: (b * nt + i, 0)), row(ROUTE_LANES),
                   pl.BlockSpec((None, 1, LRU_WIDTH), lambda b, i: (b, 0, 0)),
                   pl.BlockSpec((1, ROUTE_LANES), lambda b, i: (0, 0))),
        scratch_shapes=[pltpu.VMEM((1, LRU_WIDTH), F32), pltpu.VMEM((8, LRU_WIDTH), F32),
                        pltpu.VMEM((tc, LRU_WIDTH), F32)],
        input_output_aliases={4 + n_w: 1},
        compiler_params=_cparams(dimension_semantics=("arbitrary", "arbitrary")),
        name="mix_prompt",
    )(xbr, ybr, omla, x2d, *_mix_weights(w), jnp.zeros((n_all * ROW_CHUNKS, LANES), F32))


def _mix_sample_kernel(xbr_ref, ybr_ref, omla_ref, x_ref, sconv_ref, slru_ref, cnt_in_ref, wconv_ref, bconv_ref, wrg_ref,
                       brg_ref, wig_ref, big_ref, lam_ref, gmla_ref, glru_ref, wo_ref, gffn_ref, wr_ref, br_ref,
                       h1_ref, t_ref, route_ref, lru_ref, cnt_ref, *, t_new):
    w_ = LRU_WIDTH
    xs = [sconv_ref[:, k * w_:(k + 1) * w_] for k in range(CONV_W - 1)] + [xbr_ref[:, t * w_:(t + 1) * w_] for t in range(t_new)]
    xc = []
    for t in range(t_new):
        acc = bconv_ref[...] + xs[t] * wconv_ref[0:1, :]
        for k in range(1, CONV_W):
            acc = acc + xs[t + k] * wconv_ref[k:k + 1, :]
        xc.append(acc)
    xc = jnp.concatenate(xc, axis=0)
    a, u = _gates(xc, wrg_ref, brg_ref, wig_ref, big_ref, lam_ref)
    nb = xbr_ref.shape[0]
    h = slru_ref[...]
    hs = []
    for t in range(t_new):
        h = a[t * nb:(t + 1) * nb] * h + u[t * nb:(t + 1) * nb]
        hs.append(h)
    lru_ref[...] = h
    hs = jnp.concatenate(hs, axis=0)
    stack = lambda ref, width: jnp.concatenate([ref[:, t * width:(t + 1) * width] for t in range(t_new)], axis=0)
    o_lru = _gelu(stack(ybr_ref, w_)) * hs
    h1, tt, route, counts = _mix_tail(stack(omla_ref, MLA_WIDTH), o_lru, stack(x_ref, D_MODEL), cnt_in_ref[...],
                                      gmla_ref, glru_ref, wo_ref, gffn_ref, wr_ref, br_ref)
    cnt_ref[...] = counts
    for t in range(t_new):
        h1_ref[:, t * D_MODEL:(t + 1) * D_MODEL] = h1[t * nb:(t + 1) * nb]
        _store_token_rows(t_ref, tt[t * nb:(t + 1) * nb], t, nb, t_new)
        route_ref[:, t * ROUTE_LANES:(t + 1) * ROUTE_LANES] = route[t * nb:(t + 1) * nb]


def _mix_sample(t_all, xbr, ybr, omla, x2d, sconv, slru, counts, w, t_new):
    nb = slru.shape[0]
    n_s = nb * t_new
    assert t_all.shape[0] % (n_s * ROW_CHUNKS) == 0
    t_block = t_all.shape[0] // (n_s * ROW_CHUNKS) - 1
    full = lambda a: pl.BlockSpec(a.shape, lambda i: (0, 0))
    ins = (xbr, ybr, omla, x2d, sconv, slru, counts)
    n_w = len(_mix_weight_specs())
    return pl.pallas_call(
        _mix_sample_entry(t_new),
        out_shape=(jax.ShapeDtypeStruct((nb, t_new * D_MODEL), F32),
                   jax.ShapeDtypeStruct(t_all.shape, F32),
                   jax.ShapeDtypeStruct((nb, t_new * ROUTE_LANES), F32), jax.ShapeDtypeStruct((nb, LRU_WIDTH), F32),
                   jax.ShapeDtypeStruct((1, ROUTE_LANES), F32)),
        grid=(1,),
        in_specs=[full(a) for a in ins] + _mix_weight_specs() + [pl.BlockSpec(memory_space=pl.ANY)],
        out_specs=(pl.BlockSpec((nb, t_new * D_MODEL), lambda i: (0, 0)),
                   pl.BlockSpec((n_s * ROW_CHUNKS, LANES), lambda i: (t_block, 0)),
                   pl.BlockSpec((nb, t_new * ROUTE_LANES), lambda i: (0, 0)), pl.BlockSpec((nb, LRU_WIDTH), lambda i: (0, 0)),
                   pl.BlockSpec((1, ROUTE_LANES), lambda i: (0, 0))),
        input_output_aliases={len(ins) + n_w: 1},
        compiler_params=_cparams(dimension_semantics=("arbitrary",)),
        name="mix_sample",
    )(*ins, *_mix_weights(w), t_all)


def _mix_sample_entry(t_new):
    n_in = 7 + len(_mix_weight_specs())

    def entry(*refs):
        _mix_sample_kernel(*refs[:n_in], *refs[n_in + 1:], t_new=t_new)

    return entry


MOE_ROW_GROUP = 32


def _moe_kernel(te_ref, nu_ref, nv_ref, slot_ref, padlo_ref, padhi_ref, t_hbm, wg_ref, wu_ref, wd_ref, y_hbm,
                xa, xb, oa, ob, pair_ref, gsem, ssem, *, tm):
    del te_ref
    j = pl.program_id(0)
    n_used = nu_ref[0]
    xbufs, obufs = (xa, xb), (oa, ob)
    rc = ROW_CHUNKS
    spare = y_hbm.shape[0] - 2 * tm * rc
    n_pairs = slot_ref.shape[0]
    n_tok = n_pairs // 2

    def build_tables():
        pad_mask = 2 * tm - 1

        def spare_fill(r, carry):
            pair_ref[r] = (n_pairs + tm + r) * rc
            return carry

        lax.fori_loop(0, tm, spare_fill, 0)

        def expert_pads(e, carry):
            def pad(s, c):
                pair_ref[s + tm] = (n_pairs + jnp.bitwise_and(s, pad_mask)) * rc
                return c

            return lax.fori_loop(padlo_ref[e], padhi_ref[e], pad, carry)

        lax.fori_loop(0, N_EXPERTS, expert_pads, 0)

        def pairs(i, carry):
            for u in range(MOE_ROW_GROUP):
                p = i * MOE_ROW_GROUP + u
                pair_ref[slot_ref[p] + tm] = p * rc
            return carry

        lax.fori_loop(0, n_pairs // MOE_ROW_GROUP, pairs, 0)

    def rows(jj):
        return pl.multiple_of(nv_ref[jj + 2], MOE_ROW_GROUP)

    def row_groups(n, issue):
        for g in range(tm // MOE_ROW_GROUP):
            @pl.when(n > g * MOE_ROW_GROUP)
            def _(g=g):
                for u in range(MOE_ROW_GROUP):
                    issue(g * MOE_ROW_GROUP + u)

    def start_gather(jj, k):
        def issue(r):
            pr = pair_ref[(jj + 1) * tm + r]
            src = jnp.where(pr >= n_pairs * rc, pr - n_pairs * rc, jnp.where(pr >= n_tok * rc, pr - n_tok * rc, pr))
            pltpu.make_async_copy(t_hbm.at[pl.ds(pl.multiple_of(src, rc), rc)], xbufs[k].at[pl.ds(r * rc, rc)],
                                  gsem.at[k]).start()

        row_groups(rows(jj), issue)

    def wait_gather(jj, k):
        n = pl.multiple_of(rows(jj) * rc, 8)
        pltpu.make_async_copy(t_hbm.at[pl.ds(0, n)], xbufs[k].at[pl.ds(0, n)], gsem.at[k]).wait()

    def start_scatter(jj, k):
        def issue(r):
            dst = pl.multiple_of(pair_ref[(jj + 1) * tm + r], rc)
            pltpu.make_async_copy(obufs[k].at[pl.ds(r * rc, rc)], y_hbm.at[pl.ds(dst, rc)], ssem.at[k]).start()

        row_groups(rows(jj), issue)

    def wait_scatter(jj, k):
        n = pl.multiple_of(rows(jj) * rc, 8)
        pltpu.make_async_copy(obufs[k].at[pl.ds(0, n)], y_hbm.at[pl.ds(0, n)], ssem.at[k]).wait()

    @pl.when(j == 0)
    def _():
        for buf in (xa, xb, oa, ob):
            buf[...] = jnp.zeros(buf.shape, F32)
        pltpu.make_async_copy(oa, y_hbm.at[pl.ds(spare, tm * rc)], ssem.at[0]).start()
        build_tables()
        start_gather(0, 0)

    def tile(k):
        wait_gather(j, k)
        wait_scatter(j - 2, k)
        start_gather(jnp.minimum(j + 1, n_used - 1), 1 - k)
        start_scatter(j - 1, 1 - k)
        x = _load_token_rows(xbufs[k], tm).astype(BF16)
        hg = _dot(x, wg_ref[...].astype(BF16))
        hu = _dot(x, wu_ref[...].astype(BF16))
        act = (hg * jax.nn.sigmoid(hg) * hu).astype(BF16)
        _store_token_rows(obufs[k], _dot(act, wd_ref[...].astype(BF16)), 0, tm, 1)

    def drain(k):
        start_scatter(j, k)
        wait_gather(j, 1 - k)
        wait_scatter(j - 1, 1 - k)
        wait_scatter(j, k)

    for k in range(2):
        pl.when(jnp.logical_and(j < n_used, j % 2 == k))(functools.partial(tile, k))
        pl.when(jnp.logical_and(j == n_used - 1, j % 2 == k))(functools.partial(drain, k))


def _moe(t_all, tile_expert, n_used, tile_rows, slot, pad_lo, pad_hi, w_gate, w_up, w_down, tm):
    n_tiles = tile_expert.shape[0]
    n_tok = t_all.shape[0] // ROW_CHUNKS
    wspec = lambda a, b: pl.BlockSpec((None, a, b), lambda j, te, nu, nv, sl, lo, hi: (te[j], 0, 0))
    grid_spec = pltpu.PrefetchScalarGridSpec(
        num_scalar_prefetch=6, grid=(n_tiles,),
        in_specs=[pl.BlockSpec(memory_space=pl.ANY), wspec(D_MODEL, D_EXPERT), wspec(D_MODEL, D_EXPERT),
                  wspec(D_EXPERT, D_MODEL)],
        out_specs=pl.BlockSpec(memory_space=pl.ANY),
        scratch_shapes=([pltpu.VMEM((tm * ROW_CHUNKS, LANES), F32)] * 4
                        + [pltpu.SMEM(((n_tiles + 1) * tm,), jnp.int32)]
                        + [pltpu.SemaphoreType.DMA((2,)), pltpu.SemaphoreType.DMA((2,))]))
    return pl.pallas_call(
        functools.partial(_moe_kernel, tm=tm),
        out_shape=jax.ShapeDtypeStruct(((2 * n_tok + 2 * tm) * ROW_CHUNKS, LANES), F32),
        grid_spec=grid_spec,
        compiler_params=_cparams(dimension_semantics=("arbitrary",)),
        name="moe",
    )(tile_expert, n_used, tile_rows, slot, pad_lo, pad_hi, t_all, w_gate, w_up, w_down)


def _moe_plan(route_all, counts, tm):
    n = route_all.shape[0]
    n_pairs = 2 * n
    eflat = jnp.concatenate([route_all[:, 0], route_all[:, 1]]).astype(jnp.int32)
    rank = jnp.concatenate([route_all[:, 4], route_all[:, 5]]).astype(jnp.int32)
    counts = counts[0, N_GROUPS:N_GROUPS + N_EXPERTS].astype(jnp.int32)
    tiles_e = (counts + tm - 1) // tm
    tile_end = jnp.cumsum(tiles_e)
    tile_start = tile_end - tiles_e
    n_used = tile_end[-1]
    n_tiles = n_pairs // tm + N_EXPERTS
    slot = tile_start[eflat] * tm + rank
    tid = jnp.arange(n_tiles, dtype=jnp.int32)
    te = jnp.minimum(jnp.sum((tile_end[None, :] <= tid[:, None]).astype(jnp.int32), axis=1), N_EXPERTS - 1)
    te = jnp.where(tid < n_used, te, te[jnp.maximum(n_used - 1, 0)])
    rows = jnp.where(tid < n_used, jnp.clip(counts[te] - (tid - tile_start[te]) * tm, 0, tm), 0)
    rows = (rows + MOE_ROW_GROUP - 1) // MOE_ROW_GROUP * MOE_ROW_GROUP
    tile_rows = jnp.concatenate([jnp.full((2,), tm, jnp.int32), rows.astype(jnp.int32)])
    pad_lo = (tile_start * tm + counts).astype(jnp.int32)
    pad_hi = (tile_end * tm).astype(jnp.int32)
    return te, n_used.reshape(1).astype(jnp.int32), tile_rows, slot.astype(jnp.int32), pad_lo, pad_hi


def _final_kernel(h1_ref, y0_ref, y1_ref, route_ref, p_ref, gple_ref, wpg_ref, bpg_ref, wpp_ref, gfin_ref, y_ref):
    route = route_ref[...]
    tm = h1_ref.shape[0]
    h2 = h1_ref[...] + (route[:, 2:3] * _load_token_rows(y0_ref, tm) + route[:, 3:4] * _load_token_rows(y1_ref, tm))
    gate = jax.nn.sigmoid(_dot(_rms(h2, gple_ref[...]).astype(BF16), wpg_ref[...]) + bpg_ref[...])
    h3 = h2 + gate * _dot(p_ref[...].astype(BF16), wpp_ref[...])
    y_ref[...] = _rms(h3, gfin_ref[...])


def _final(h1, y_pairs, route, p2d, w, blk0, n_all, tm):
    n = h1.shape[0]
    nb_all = n_all // tm
    row = lambda width: pl.BlockSpec((tm, width), lambda i: (i, 0))
    return pl.pallas_call(
        _final_kernel,
        out_shape=jax.ShapeDtypeStruct((n, D_MODEL), F32),
        grid=(n // tm,),
        in_specs=[row(D_MODEL),
                  pl.BlockSpec((tm * ROW_CHUNKS, LANES), lambda i: (blk0 + i, 0)),
                  pl.BlockSpec((tm * ROW_CHUNKS, LANES), lambda i: (nb_all + blk0 + i, 0)),
                  row(ROUTE_LANES), row(PLE_DIM), _const_spec((1, D_MODEL)), _const_spec((D_MODEL, D_MODEL)),
                  _const_spec((1, D_MODEL)), _const_spec((PLE_DIM, D_MODEL)), _const_spec((1, D_MODEL))],
        out_specs=row(D_MODEL),
        compiler_params=_cparams(dimension_semantics=("arbitrary",)),
        name="final",
    )(h1, y_pairs, y_pairs, route, p2d, w["g_ple"], w["w_ple_gate"], w["b_ple_gate"], w["w_ple_proj"], w["g_final"])


def _rope_tables(pos):
    half = QK_ROPE // 2
    inv = ROPE_THETA ** (-jnp.arange(half, dtype=F32) / half)
    ang = pos.astype(F32)[:, None] * inv[None, :]
    pad = jnp.zeros((pos.shape[0], ROPE_PAD - QK_ROPE), F32)
    cos = jnp.concatenate([jnp.cos(ang), jnp.cos(ang), pad], axis=1)
    sin = jnp.concatenate([jnp.sin(ang), jnp.sin(ang), pad], axis=1)
    return cos, sin


def _rot_cols(wr):
    half = QK_ROPE // 2
    return jnp.concatenate([-wr[..., half:], wr[..., :half]], axis=-1)


def _pad_cols(wr):
    return jnp.concatenate([wr, jnp.zeros(wr.shape[:-1] + (ROPE_PAD - QK_ROPE,), wr.dtype)], axis=-1)


def _prep_weights(g_mix, w_in, g_q, w_uq, g_kv, w_ukv, w_conv, b_conv, w_rg, b_rg, w_ig, b_ig, lru_lambda,
                  g_out_mla, g_out_lru, w_o, g_ffn, w_group, b_group, w_router, b_router, g_ple, w_ple_gate,
                  b_ple_gate, w_ple_proj, g_final):
    row = lambda v: v.reshape(1, -1).astype(F32)
    s0, s1, s2, s3 = Q_LORA, Q_LORA + KV_LORA, Q_LORA + KV_LORA + QK_ROPE, Q_LORA + KV_LORA + QK_ROPE + LRU_WIDTH
    w_in, w_uq, w_ukv = w_in.astype(BF16), w_uq.astype(BF16), w_ukv.astype(BF16)
    kr = w_in[:, s1:s2]
    w_in_ext = jnp.concatenate([w_in[:, :s1], _pad_cols(kr), _pad_cols(_rot_cols(kr)), w_in[:, s2:]], axis=1)
    uq = w_uq.reshape(Q_LORA, MLA_HEADS, QK_NOPE + QK_ROPE)
    uq_r = uq[..., QK_NOPE:]
    w_uq_ext = jnp.concatenate([uq[..., :QK_NOPE].reshape(Q_LORA, -1), _pad_cols(uq_r).reshape(Q_LORA, -1),
                                _pad_cols(_rot_cols(uq_r)).reshape(Q_LORA, -1)], axis=1)
    w_uk = jnp.transpose(w_ukv[..., :QK_NOPE], (1, 2, 0))
    w_uv = jnp.transpose(w_ukv[..., QK_NOPE:], (1, 0, 2))
    w_route = jnp.concatenate([w_group, w_router, jnp.zeros((D_MODEL, ROUTE_LANES - N_GROUPS - N_EXPERTS), F32)], axis=1)
    b_route = jnp.concatenate([b_group, b_router, jnp.zeros((ROUTE_LANES - N_GROUPS - N_EXPERTS,), F32)])
    return dict(
        g_mix=row(g_mix), w_in=w_in_ext.astype(BF16), g_q=row(g_q), w_uq=w_uq_ext.astype(BF16), g_kv=row(g_kv),
        w_uk=w_uk.astype(BF16), w_uv=w_uv.astype(BF16), w_conv=w_conv.astype(F32), b_conv=row(b_conv),
        w_rg=w_rg.astype(BF16), b_rg=row(b_rg), w_ig=w_ig.astype(BF16), b_ig=row(b_ig), lru_lambda=row(lru_lambda),
        g_out_mla=row(g_out_mla), g_out_lru=row(g_out_lru), w_o=w_o.astype(BF16), g_ffn=row(g_ffn),
        w_route=w_route.astype(BF16), b_route=row(b_route), g_ple=row(g_ple), w_ple_gate=w_ple_gate.astype(BF16),
        b_ple_gate=row(b_ple_gate), w_ple_proj=w_ple_proj.astype(BF16), g_final=row(g_final))


TM_PROJ = 512
TM_FINAL = 256
TQ_ATTN = 256
ATTN_HEAD_GROUP = 4
TC_MIX = 256
TM_MOE = 256
PAGES_PER_STEP = 32
SAMPLE_BUFFERS = 4
SAMPLE_SUB_BLOCKS = 4


def kernel(x_prompt, x_sample, p_prompt, p_sample, cache_latent, cache_krope, state_lru, state_conv, page_table, g_mix, w_in, g_q, w_uq, g_kv, w_ukv, w_conv, b_conv, w_rg, b_rg, w_ig, b_ig, lru_lambda, g_out_mla, g_out_lru, w_o, g_ffn, w_group, b_group, w_router, b_router, w_gate, w_up, w_down, g_ple, w_ple_gate, b_ple_gate, w_ple_proj, g_final):
    assert w_in.shape[0] == 1, "single trunk layer"
    batch, seq, _ = x_prompt.shape
    db, t_new, _ = x_sample.shape
    page = cache_latent.shape[2]
    past_len = page_table.shape[1] * page
    n_p, n_s = batch * seq, db * t_new
    n_all = n_p + n_s

    w = _prep_weights(g_mix[0], w_in[0], g_q[0], w_uq[0], g_kv[0], w_ukv[0], w_conv[0], b_conv[0], w_rg[0], b_rg[0],
                      w_ig[0], b_ig[0], lru_lambda[0], g_out_mla[0], g_out_lru[0], w_o[0], g_ffn[0], w_group[0],
                      b_group[0], w_router[0], b_router[0], g_ple[0], w_ple_gate[0], b_ple_gate[0], w_ple_proj[0],
                      g_final)

    xp = x_prompt.reshape(n_p, D_MODEL)
    cos_p, sin_p = _rope_tables(jnp.arange(seq))
    qcat_p, kcat_p, lat_p, krope_p, xbr_p, ybr_p = _proj(xp, cos_p, sin_p, seq // TM_PROJ, w, TM_PROJ)
    omla_p = _prompt_attn(qcat_p, kcat_p, w["w_uv"], batch, seq, TQ_ATTN)
    h1_p, t_p, route_p, lru_p, counts_p = _mix_prompt(xbr_p, ybr_p, omla_p, xp, w, batch, seq, TC_MIX, n_all)

    xs = x_sample.reshape(n_s, D_MODEL)
    tm_s = min(TM_PROJ, n_s)
    cos_s, sin_s = _rope_tables(past_len + (jnp.arange(tm_s) % t_new))
    qcat_s, kcat_s, lat_s, krope_s, xbr_s, ybr_s = _proj(xs, cos_s, sin_s, 1, w, tm_s)
    q_s = qcat_s.reshape(db, t_new, MLA_HEADS, QK_CAT).transpose(0, 2, 1, 3).reshape(db, MLA_HEADS * t_new, QK_CAT)
    knew = jnp.pad(kcat_s.reshape(db, t_new, QK_CAT), ((0, 0), (0, 8 - t_new), (0, 0)))
    cache_ropet = jnp.swapaxes(cache_krope[0], 1, 2)
    omla_s = _sample_attn(page_table, q_s, knew, cache_latent[0], cache_ropet, w["w_uv"], PAGES_PER_STEP)
    h1_s, t_all, route_s, lru_s, counts = _mix_sample(
        t_p, xbr_s.reshape(db, t_new * LRU_WIDTH), ybr_s.reshape(db, t_new * LRU_WIDTH),
        omla_s.reshape(db, t_new * MLA_WIDTH), x_sample.reshape(db, t_new * D_MODEL),
        state_conv[0].reshape(db, (CONV_W - 1) * LRU_WIDTH), state_lru[0], counts_p, w, t_new)
    h1_s = h1_s.reshape(n_s, D_MODEL)
    route_s = route_s.reshape(n_s, ROUTE_LANES)

    route_all = jnp.concatenate([route_p, route_s], axis=0)
    te, n_used, tile_rows, slot, pad_lo, pad_hi = _moe_plan(route_all, counts, TM_MOE)
    y_pairs = _moe(t_all, te, n_used, tile_rows, slot, pad_lo, pad_hi, w_gate[0], w_up[0], w_down[0], TM_MOE)

    y_p = _final(h1_p, y_pairs, route_p, p_prompt[0].reshape(n_p, PLE_DIM), w, 0, n_all, TM_FINAL)
    y_s = _final(h1_s, y_pairs, route_s, p_sample[0].reshape(n_s, PLE_DIM), w, n_p // TM_FINAL, n_all, TM_FINAL)

    new_conv_p = xbr_p.reshape(batch, seq, LRU_WIDTH)[:, seq - (CONV_W - 1):]
    hist = jnp.concatenate([state_conv[0], xbr_s.reshape(db, t_new, LRU_WIDTH)], axis=1)
    new_conv_s = hist[:, hist.shape[1] - (CONV_W - 1):]
    return (y_p.reshape(batch, seq, D_MODEL), y_s.reshape(db, t_new, D_MODEL),
            lat_p.reshape(1, batch, seq, KV_LORA), krope_p.reshape(1, batch, seq, QK_ROPE),
            lru_p.reshape(1, batch, LRU_WIDTH), new_conv_p[None],
            lat_s.reshape(1, db, t_new, KV_LORA), krope_s.reshape(1, db, t_new, QK_ROPE),
            lru_s[None], new_conv_s[None])
```

```python
import functools

import jax
import jax.numpy as jnp
from jax import lax
from jax.experimental import pallas as pl
from jax.experimental.pallas import tpu as pltpu

F32 = jnp.float32
BF16 = jnp.bfloat16

D_MODEL = 2048
MLA_HEADS = 8
V_HEAD = 128
MLA_WIDTH = MLA_HEADS * V_HEAD
LRU_WIDTH = D_MODEL - MLA_WIDTH
QK_NOPE = 128
QK_ROPE = 64
Q_LORA = 512
KV_LORA = 256
ROPE_THETA = 10000.0
SM_SCALE = (QK_NOPE + QK_ROPE) ** -0.5
NEG_INF = -1e30
LRU_BLOCKS = 8
LRU_BLOCK = LRU_WIDTH // LRU_BLOCKS
CONV_W = 4
LRU_C = 8.0
N_GROUPS = 4
EXPERTS_PER_GROUP = 8
N_EXPERTS = N_GROUPS * EXPERTS_PER_GROUP
D_EXPERT = 512
PLE_DIM = 256
EPS = 1e-6

LANES = 128
ROPE_PAD = LANES
QK_CAT = KV_LORA + ROPE_PAD
C_Q0, C_KV0, C_KRA0, C_KRB0, C_X0, C_Y0, IN_EXT = 0, 512, 768, 896, 1024, 2048, 3072
ROUTE_LANES = LANES
VMEM_LIMIT = 56 * 1024 * 1024


def _cparams(**kw):
    return pltpu.CompilerParams(vmem_limit_bytes=VMEM_LIMIT, **kw)


def _rms(x, g):
    return x * lax.rsqrt(jnp.mean(x * x, axis=-1, keepdims=True) + EPS) * g


def _dot(a, b):
    return jnp.dot(a, b, preferred_element_type=F32)


def _dot_nt(a, b):
    return lax.dot_general(a, b, (((1,), (1,)), ((), ())), preferred_element_type=F32)


def _const_spec(shape):
    nd = len(shape)
    return pl.BlockSpec(shape, lambda *_: (0,) * nd, pipeline_mode=pl.Buffered(1))


def _proj_kernel(x_ref, cos_ref, sin_ref, gmix_ref, win_ref, gq_ref, wuq_ref, gkv_ref, wuk_ref,
                 qcat_ref, kcat_ref, lat_ref, krope_ref, xbr_ref, ybr_ref):
    u = _rms(x_ref[...], gmix_ref[...]).astype(BF16)
    z = _dot(u, win_ref[...])
    xbr_ref[...] = z[:, C_X0:C_Y0]
    ybr_ref[...] = z[:, C_Y0:IN_EXT]
    cos = cos_ref[...]
    sin = sin_ref[...]
    lat = _rms(z[:, C_KV0:C_KRA0], gkv_ref[...])
    kr = z[:, C_KRA0:C_KRB0] * cos + z[:, C_KRB0:C_X0] * sin
    lat_ref[...] = lat
    krope_ref[...] = kr[:, :QK_ROPE]
    kcat_ref[:, 0:KV_LORA] = lat.astype(BF16)
    kcat_ref[:, KV_LORA:QK_CAT] = kr.astype(BF16)
    qn = _rms(z[:, C_Q0:C_KV0], gq_ref[...]).astype(BF16)
    q = _dot(qn, wuq_ref[...])
    ra0 = MLA_HEADS * QK_NOPE
    rb0 = ra0 + MLA_HEADS * ROPE_PAD
    for h in range(MLA_HEADS):
        ql = _dot(q[:, h * QK_NOPE:(h + 1) * QK_NOPE].astype(BF16), wuk_ref[h])
        qr = (q[:, ra0 + h * ROPE_PAD:ra0 + (h + 1) * ROPE_PAD] * cos
              + q[:, rb0 + h * ROPE_PAD:rb0 + (h + 1) * ROPE_PAD] * sin)
        qcat_ref[:, h * QK_CAT:h * QK_CAT + KV_LORA] = ql.astype(BF16)
        qcat_ref[:, h * QK_CAT + KV_LORA:(h + 1) * QK_CAT] = qr.astype(BF16)


def _proj(x2d, cos, sin, pos_blocks, w, tm):
    n = x2d.shape[0]
    row = lambda width: pl.BlockSpec((tm, width), lambda i: (i, 0))
    pos = pl.BlockSpec((tm, ROPE_PAD), lambda i: (i % pos_blocks, 0))
    out_shape = (
        jax.ShapeDtypeStruct((n, MLA_HEADS * QK_CAT), BF16),
        jax.ShapeDtypeStruct((n, QK_CAT), BF16),
        jax.ShapeDtypeStruct((n, KV_LORA), F32),
        jax.ShapeDtypeStruct((n, QK_ROPE), F32),
        jax.ShapeDtypeStruct((n, LRU_WIDTH), F32),
        jax.ShapeDtypeStruct((n, LRU_WIDTH), F32),
    )
    return pl.pallas_call(
        _proj_kernel,
        out_shape=out_shape,
        grid=(n // tm,),
        in_specs=[row(D_MODEL), pos, pos, _const_spec((1, D_MODEL)), _const_spec((D_MODEL, IN_EXT)),
                  _const_spec((1, Q_LORA)), _const_spec((Q_LORA, 3 * MLA_HEADS * LANES)),
                  _const_spec((1, KV_LORA)), _const_spec((MLA_HEADS, QK_NOPE, KV_LORA))],
        out_specs=(row(MLA_HEADS * QK_CAT), row(QK_CAT), row(KV_LORA), row(QK_ROPE), row(LRU_WIDTH), row(LRU_WIDTH)),
        compiler_params=_cparams(dimension_semantics=("arbitrary",)),
        name="proj",
    )(x2d, cos, sin, w["g_mix"], w["w_in"], w["g_q"], w["w_uq"], w["g_kv"], w["w_uk"])


def _prompt_attn_kernel(q_ref, k_ref, wuv_ref, o_ref, *scratch, tq, group):
    m_refs, l_refs, acc_refs = scratch[:MLA_HEADS], scratch[MLA_HEADS:2 * MLA_HEADS], scratch[2 * MLA_HEADS:]
    qi = pl.program_id(1)
    for h in range(MLA_HEADS):
        m_refs[h][...] = jnp.full(m_refs[h].shape, NEG_INF, F32)
        l_refs[h][...] = jnp.zeros(l_refs[h].shape, F32)
        acc_refs[h][...] = jnp.zeros(acc_refs[h].shape, F32)

    def step(kb, masked):
        k = k_ref[pl.ds(pl.multiple_of(kb * tq, tq), tq), :]
        v = k[:, :KV_LORA]
        if masked:
            row = lax.broadcasted_iota(jnp.int32, (tq, tq), 0)
            col = lax.broadcasted_iota(jnp.int32, (tq, tq), 1)
            keep = col <= row
        for h0 in range(0, MLA_HEADS, group):
            heads = range(h0, h0 + group)
            scores = [_dot_nt(q_ref[:, h * QK_CAT:(h + 1) * QK_CAT], k) * SM_SCALE for h in heads]
            probs = []
            for h, s in zip(heads, scores):
                if masked:
                    s = jnp.where(keep, s, NEG_INF)
                m_prev = m_refs[h][...]
                m_new = jnp.maximum(m_prev, jnp.max(s, axis=1, keepdims=True))
                alpha = jnp.exp(m_prev - m_new)
                p = jnp.exp(s - jnp.concatenate([m_new] * (tq // LANES), axis=1))
                l_refs[h][...] = alpha * l_refs[h][...] + jnp.sum(p, axis=1, keepdims=True)
                m_refs[h][...] = m_new
                probs.append((alpha, p.astype(BF16)))
            for h, (alpha, p) in zip(heads, probs):
                acc_refs[h][...] = jnp.concatenate([alpha] * (KV_LORA // LANES), axis=1) * acc_refs[h][...] + _dot(p, v)

    def body(kb, carry):
        step(kb, False)
        return carry

    lax.fori_loop(0, qi, body, 0)
    step(qi, True)
    for h in range(MLA_HEADS):
        o = acc_refs[h][...] / jnp.concatenate([l_refs[h][...]] * (KV_LORA // LANES), axis=1)
        o_ref[:, h * V_HEAD:(h + 1) * V_HEAD] = _dot(o.astype(BF16), wuv_ref[h])


def _prompt_attn(qcat, kcat, wuv, batch, seq, tq):
    nq = seq // tq
    return pl.pallas_call(
        functools.partial(_prompt_attn_kernel, tq=tq, group=ATTN_HEAD_GROUP),
        out_shape=jax.ShapeDtypeStruct((batch * seq, MLA_WIDTH), F32),
        grid=(batch, nq),
        in_specs=[pl.BlockSpec((tq, MLA_HEADS * QK_CAT), lambda b, i: (b * nq + i, 0)),
                  pl.BlockSpec((seq, QK_CAT), lambda b, i: (b, 0)),
                  _const_spec((MLA_HEADS, KV_LORA, V_HEAD))],
        out_specs=pl.BlockSpec((tq, MLA_WIDTH), lambda b, i: (b * nq + i, 0)),
        scratch_shapes=([pltpu.VMEM((tq, LANES), F32)] * (2 * MLA_HEADS) + [pltpu.VMEM((tq, KV_LORA), F32)] * MLA_HEADS),
        compiler_params=_cparams(dimension_semantics=("arbitrary", "arbitrary")),
        name="prompt_attn",
    )(qcat, kcat, wuv)


def _sample_attn_kernel(pt_ref, q_ref, knew_ref, lat_hbm, ropet_hbm, wuv_ref, o_ref, *scratch,
                        pages, page, n_groups, n_buf, t_new, n_sub):
    lat_bufs, rope_bufs = scratch[:n_buf], scratch[n_buf:2 * n_buf]
    sem, m_ref, l_ref, acc_ref = scratch[2 * n_buf:]
    c = pl.program_id(1)
    step = pl.program_id(0) * n_groups + c
    n_steps = pl.num_programs(0) * n_groups
    bufs = tuple(zip(lat_bufs, rope_bufs))

    def start_chunk(chunk, k):
        for g in range(pages):
            pid = pt_ref[chunk * pages + g]
            pltpu.make_async_copy(lat_hbm.at[pid], bufs[k][0].at[g], sem.at[0, k]).start()
            pltpu.make_async_copy(ropet_hbm.at[pid], bufs[k][1].at[g], sem.at[1, k]).start()

    def wait_chunk(k):
        pltpu.make_async_copy(lat_hbm.at[pl.ds(0, pages)], bufs[k][0], sem.at[0, k]).wait()
        pltpu.make_async_copy(ropet_hbm.at[pl.ds(0, pages)], bufs[k][1], sem.at[1, k]).wait()

    q = q_ref[0]
    q_lat = q[:, :KV_LORA]
    q_rope = q[:, KV_LORA:KV_LORA + QK_ROPE]
    ps = pages // n_sub

    def attend(k):
        latbuf, ropebuf = bufs[k]
        lats, scores, parts = [], [], []
        for sb in range(n_sub):
            lat = latbuf[sb * ps:(sb + 1) * ps].reshape(ps * page, KV_LORA).astype(BF16)
            s_rope = jnp.concatenate([_dot(q_rope, ropebuf[sb * ps + g].astype(BF16)) for g in range(ps)], axis=1)
            lats.append(lat)
            scores.append((_dot_nt(q_lat, lat) + s_rope) * SM_SCALE)
        for sb in range(n_sub):
            m_i = jnp.max(scores[sb], axis=1, keepdims=True)
            p = jnp.exp(scores[sb] - m_i)
            parts.append((m_i, jnp.sum(p, axis=1, keepdims=True), p.astype(BF16)))
        parts = [(m_i, l_i, _dot(p, lats[sb])) for sb, (m_i, l_i, p) in enumerate(parts)]
        m_prev = m_ref[...]
        m_new = m_prev
        for m_i, _, _ in parts:
            m_new = jnp.maximum(m_new, m_i)
        alpha = jnp.exp(m_prev - m_new)
        l = alpha * l_ref[...]
        acc = alpha * acc_ref[...]
        for m_i, l_i, o_i in parts:
            w_i = jnp.exp(m_i - m_new)
            l = l + w_i * l_i
            acc = acc + w_i * o_i
        l_ref[...] = l
        acc_ref[...] = acc
        m_ref[...] = m_new

    @pl.when(step == 0)
    def _():
        for k in range(n_buf):
            start_chunk(k, k)

    @pl.when(c == 0)
    def _():
        m_ref[...] = jnp.full(m_ref.shape, NEG_INF, F32)
        l_ref[...] = jnp.zeros(l_ref.shape, F32)
        acc_ref[...] = jnp.zeros(acc_ref.shape, F32)

    last_chunk = n_buf * n_steps - 1
    for k in range(n_buf):
        wait_chunk(k)
        attend(k)
        start_chunk(jnp.minimum(n_buf * (step + 1) + k, last_chunk), k)

    @pl.when(step == n_steps - 1)
    def _():
        for k in range(n_buf):
            wait_chunk(k)

    @pl.when(c == n_groups - 1)
    def _():
        qf = q.astype(F32)
        kn = knew_ref[0].astype(F32)
        tok = lax.broadcasted_iota(jnp.int32, (q.shape[0], 1), 0) & (t_new - 1)
        cols = []
        for j in range(t_new):
            sj = jnp.sum(qf * kn[j:j + 1, :], axis=1, keepdims=True) * SM_SCALE
            cols.append(jnp.where(tok >= j, sj, NEG_INF))
        m_prev = m_ref[...]
        m_new = m_prev
        for sj in cols:
            m_new = jnp.maximum(m_new, sj)
        alpha = jnp.exp(m_prev - m_new)
        l = alpha * l_ref[...]
        acc = alpha * acc_ref[...]
        for j, sj in enumerate(cols):
            pj = jnp.exp(sj - m_new)
            l = l + pj
            acc = acc + pj * kn[j:j + 1, :KV_LORA]
        o = (acc / l).astype(BF16)
        for h in range(MLA_HEADS):
            oh = _dot(o, wuv_ref[h])
            o_ref[0, :, h * V_HEAD:(h + 1) * V_HEAD] = oh[h * t_new:(h + 1) * t_new]


def _sample_attn(page_table, q_s, knew, cache_lat, cache_ropet, wuv, pages):
    db, n_pages = page_table.shape
    page = cache_lat.shape[1]
    n_buf = SAMPLE_BUFFERS
    n_groups = n_pages // (n_buf * pages)
    assert n_groups * n_buf * pages == n_pages
    rows = q_s.shape[1]
    t_new = rows // MLA_HEADS
    in_specs = [pl.BlockSpec((1, rows, QK_CAT), lambda b, c, pt: (b, 0, 0)),
                pl.BlockSpec((1, 8, QK_CAT), lambda b, c, pt: (b, 0, 0)),
                pl.BlockSpec(memory_space=pl.ANY), pl.BlockSpec(memory_space=pl.ANY),
                pl.BlockSpec((MLA_HEADS, KV_LORA, V_HEAD), lambda b, c, pt: (0, 0, 0))]
    grid_spec = pltpu.PrefetchScalarGridSpec(
        num_scalar_prefetch=1, grid=(db, n_groups), in_specs=in_specs,
        out_specs=pl.BlockSpec((1, t_new, MLA_WIDTH), lambda b, c, pt: (b, 0, 0)),
        scratch_shapes=([pltpu.VMEM((pages, page, KV_LORA), F32)] * n_buf
                        + [pltpu.VMEM((pages, QK_ROPE, page), F32)] * n_buf
                        + [pltpu.SemaphoreType.DMA((2, n_buf)), pltpu.VMEM((rows, 1), F32),
                           pltpu.VMEM((rows, 1), F32), pltpu.VMEM((rows, KV_LORA), F32)]))
    return pl.pallas_call(
        functools.partial(_sample_attn_kernel, pages=pages, page=page, n_groups=n_groups, n_buf=n_buf, t_new=t_new,
                          n_sub=SAMPLE_SUB_BLOCKS),
        out_shape=jax.ShapeDtypeStruct((db, t_new, MLA_WIDTH), F32),
        grid_spec=grid_spec,
        compiler_params=_cparams(dimension_semantics=("arbitrary", "arbitrary")),
        name="sample_attn",
    )(page_table.reshape(-1), q_s, knew, cache_lat, cache_ropet, wuv)


def _gate_block(xc, n, wrg_ref, brg_ref, wig_ref, big_ref, lam_ref):
    cs = slice(n * LRU_BLOCK, (n + 1) * LRU_BLOCK)
    xb = xc.astype(BF16)
    r = jax.nn.sigmoid(_dot(xb, wrg_ref[n]) + brg_ref[:, cs])
    i = jax.nn.sigmoid(_dot(xb, wig_ref[n]) + big_ref[:, cs])
    neg_lam = -lam_ref[:, cs]
    softplus = jnp.maximum(neg_lam, 0.0) + jnp.log(1.0 + jnp.exp(-jnp.abs(neg_lam)))
    log_a = -LRU_C * r * softplus
    a = jnp.exp(log_a)
    gap = 1.0 - a * a
    root = jnp.where(gap > 0.0, gap * lax.rsqrt(gap), 0.0)
    u = root * i * xc
    return a, u


def _gates(xc, wrg_ref, brg_ref, wig_ref, big_ref, lam_ref):
    blocks = [_gate_block(xc[:, n * LRU_BLOCK:(n + 1) * LRU_BLOCK], n, wrg_ref, brg_ref, wig_ref, big_ref, lam_ref)
              for n in range(LRU_BLOCKS)]
    return (jnp.concatenate([a for a, _ in blocks], axis=1), jnp.concatenate([u for _, u in blocks], axis=1))


def _gelu(y):
    return 0.5 * y * (1.0 + jnp.tanh(0.7978845608028654 * (y + 0.044715 * (y * y * y))))


def _route(logits, counts):
    lane = lax.broadcasted_iota(jnp.int32, logits.shape, 1)
    far = jnp.int32(4 * ROUTE_LANES)
    gmask = lane < N_GROUPS
    gl = jnp.where(gmask, logits, NEG_INF)
    gmax = jnp.max(gl, axis=1, keepdims=True)
    gidx = jnp.min(jnp.where(gl == gmax, lane, far), axis=1, keepdims=True)
    g_w = 1.0 / jnp.sum(jnp.where(gmask, jnp.exp(gl - gmax), 0.0), axis=1, keepdims=True)
    lo = N_GROUPS + gidx * EXPERTS_PER_GROUP
    emask = jnp.logical_and(lane >= lo, lane < lo + EXPERTS_PER_GROUP)
    el = jnp.where(emask, logits, NEG_INF)
    emax = jnp.max(el, axis=1, keepdims=True)
    ex = jnp.where(emask, jnp.exp(el - emax), 0.0)
    prob = jnp.where(emask, ex / jnp.sum(ex, axis=1, keepdims=True), -1.0)
    p1 = jnp.max(prob, axis=1, keepdims=True)
    i1 = jnp.min(jnp.where(prob == p1, lane, far), axis=1, keepdims=True)
    rest = jnp.where(lane == i1, -1.0, prob)
    p2 = jnp.max(rest, axis=1, keepdims=True)
    i2 = jnp.min(jnp.where(rest == p2, lane, far), axis=1, keepdims=True)
    den = p1 + p2
    w1 = g_w * p1 / den
    w2 = g_w * p2 / den
    e1 = (i1 - N_GROUPS).astype(F32)
    e2 = (i2 - N_GROUPS).astype(F32)
    n = logits.shape[0]
    hit1 = lane == i1
    hit2 = lane == i2
    chosen = jnp.where(hit1, 1.0, jnp.where(hit2, 1.0, 0.0))
    earlier = jnp.where(lax.broadcasted_iota(jnp.int32, (n, n), 1) < lax.broadcasted_iota(jnp.int32, (n, n), 0), 1.0, 0.0)
    before = _dot(earlier.astype(BF16), chosen.astype(BF16)) + counts
    r1 = jnp.sum(jnp.where(hit1, before, 0.0), axis=1, keepdims=True)
    r2 = jnp.sum(jnp.where(hit2, before, 0.0), axis=1, keepdims=True)
    out = jnp.where(lane == 0, e1, jnp.where(lane == 1, e2, jnp.where(lane == 2, w1, jnp.where(lane == 3, w2,
          jnp.where(lane == 4, r1, jnp.where(lane == 5, r2, 0.0))))))
    return out, counts + jnp.sum(chosen, axis=0, keepdims=True)


ROW_CHUNKS = D_MODEL // LANES


def _store_token_rows(ref, val, first, count, step):
    for c in range(ROW_CHUNKS):
        ref[pl.ds(first * ROW_CHUNKS + c, count, stride=step * ROW_CHUNKS), :] = val[:, c * LANES:(c + 1) * LANES]


def _load_token_rows(ref, count):
    return jnp.concatenate([ref[pl.ds(c, count, stride=ROW_CHUNKS), :] for c in range(ROW_CHUNKS)], axis=1)


def _mix_tail(o_mla, o_lru, x, counts, gmla_ref, glru_ref, wo_ref, gffn_ref, wr_ref, br_ref):
    mixed = jnp.concatenate([_rms(o_mla, gmla_ref[...]), _rms(o_lru, glru_ref[...])], axis=1).astype(BF16)
    h1 = x + _dot(mixed, wo_ref[...])
    t = _rms(h1, gffn_ref[...])
    logits = _dot(t.astype(BF16), wr_ref[...]) + br_ref[...]
    route, counts = _route(logits, counts)
    return h1, t, route, counts


def _mix_prompt_kernel(xbr_ref, ybr_ref, omla_ref, x_ref, wconv_ref, bconv_ref, wrg_ref, brg_ref, wig_ref, big_ref,
                       lam_ref, gmla_ref, glru_ref, wo_ref, gffn_ref, wr_ref, br_ref,
                       h1_ref, t_ref, route_ref, lru_ref, cnt_ref, hc_ref, tail_ref, olru_ref, *, tc):
    @pl.when(pl.program_id(1) == 0)
    def _():
        hc_ref[...] = jnp.zeros(hc_ref.shape, F32)
        tail_ref[...] = jnp.zeros(tail_ref.shape, F32)

    @pl.when(jnp.logical_and(pl.program_id(0) == 0, pl.program_id(1) == 0))
    def _():
        cnt_ref[...] = jnp.zeros(cnt_ref.shape, F32)

    sub8 = lax.broadcasted_iota(jnp.int32, (tc // 8, 8, LRU_BLOCK), 1)
    row = lax.broadcasted_iota(jnp.int32, (tc, LRU_BLOCK), 0)
    for n in range(LRU_BLOCKS):
        cs = slice(n * LRU_BLOCK, (n + 1) * LRU_BLOCK)
        xb = xbr_ref[:, cs]
        tail = tail_ref[:, cs]
        xc = bconv_ref[:, cs] + xb * wconv_ref[CONV_W - 1:CONV_W, cs]
        x3 = jnp.concatenate([tail, xb], axis=0).reshape(tc // 8 + 1, 8, LRU_BLOCK)
        for j in range(1, CONV_W):
            rot = pltpu.roll(x3, j, axis=1)
            sh = jnp.where(sub8 < j, rot[:-1], rot[1:]).reshape(tc, LRU_BLOCK)
            xc = xc + sh * wconv_ref[CONV_W - 1 - j:CONV_W - j, cs]
        tail_ref[:, cs] = xb[tc - 8:tc]

        a, u = _gate_block(xc, n, wrg_ref, brg_ref, wig_ref, big_ref, lam_ref)
        d = 1
        while d < tc:
            keep = row >= d
            u = jnp.where(keep, a * pltpu.roll(u, d, axis=0) + u, u)
            a = jnp.where(keep, a * pltpu.roll(a, d, axis=0), a)
            d *= 2
        hs = a * hc_ref[:, cs] + u
        hc_ref[:, cs] = hs[tc - 1:tc]
        lru_ref[:, cs] = hs[tc - 1:tc]
        olru_ref[:, cs] = _gelu(ybr_ref[:, cs]) * hs

    h1, t, route, counts = _mix_tail(omla_ref[...], olru_ref[...], x_ref[...], cnt_ref[...],
                                     gmla_ref, glru_ref, wo_ref, gffn_ref, wr_ref, br_ref)
    h1_ref[...] = h1
    _store_token_rows(t_ref, t, 0, tc, 1)
    route_ref[...] = route
    cnt_ref[...] = counts


def _mix_weight_specs():
    return [_const_spec((CONV_W, LRU_WIDTH)), _const_spec((1, LRU_WIDTH)),
            _const_spec((LRU_BLOCKS, LRU_BLOCK, LRU_BLOCK)), _const_spec((1, LRU_WIDTH)),
            _const_spec((LRU_BLOCKS, LRU_BLOCK, LRU_BLOCK)), _const_spec((1, LRU_WIDTH)),
            _const_spec((1, LRU_WIDTH)), _const_spec((1, MLA_WIDTH)), _const_spec((1, LRU_WIDTH)),
            _const_spec((D_MODEL, D_MODEL)), _const_spec((1, D_MODEL)),
            _const_spec((D_MODEL, ROUTE_LANES)), _const_spec((1, ROUTE_LANES))]


def _mix_weights(w):
    return (w["w_conv"], w["b_conv"], w["w_rg"], w["b_rg"], w["w_ig"], w["b_ig"], w["lru_lambda"],
            w["g_out_mla"], w["g_out_lru"], w["w_o"], w["g_ffn"], w["w_route"], w["b_route"])


def _mix_prompt(xbr, ybr, omla, x2d, w, batch, seq, tc, n_all):
    nt = seq // tc
    n = batch * seq
    row = lambda width: pl.BlockSpec((tc, width), lambda b, i: (b * nt + i, 0))
    n_w = len(_mix_weight_specs())

    def entry(*refs):
        _mix_prompt_kernel(*refs[:4 + n_w], *refs[5 + n_w:], tc=tc)

    return pl.pallas_call(
        entry,
        out_shape=(jax.ShapeDtypeStruct((n, D_MODEL), F32), jax.ShapeDtypeStruct((n_all * ROW_CHUNKS, LANES), F32),
                   jax.ShapeDtypeStruct((n, ROUTE_LANES), F32), jax.ShapeDtypeStruct((batch, 1, LRU_WIDTH), F32),
                   jax.ShapeDtypeStruct((1, ROUTE_LANES), F32)),
        grid=(batch, nt),
        in_specs=([row(LRU_WIDTH), row(LRU_WIDTH), row(MLA_WIDTH), row(D_MODEL)] + _mix_weight_specs()
                  + [pl.BlockSpec(memory_space=pl.ANY)]),
        out_specs=(row(D_MODEL), pl.BlockSpec((tc * ROW_CHUNKS, LANES), lambda b, i: (b * nt + i, 0)), row(ROUTE_LANES),
                   pl.BlockSpec((None, 1, LRU_WIDTH), lambda b, i: (b, 0, 0)),
                   pl.BlockSpec((1, ROUTE_LANES), lambda b, i: (0, 0))),
        scratch_shapes=[pltpu.VMEM((1, LRU_WIDTH), F32), pltpu.VMEM((8, LRU_WIDTH), F32),
                        pltpu.VMEM((tc, LRU_WIDTH), F32)],
        input_output_aliases={4 + n_w: 1},
        compiler_params=_cparams(dimension_semantics=("arbitrary", "arbitrary")),
        name="mix_prompt",
    )(xbr, ybr, omla, x2d, *_mix_weights(w), jnp.zeros((n_all * ROW_CHUNKS, LANES), F32))


def _mix_sample_kernel(xbr_ref, ybr_ref, omla_ref, x_ref, sconv_ref, slru_ref, cnt_in_ref, wconv_ref, bconv_ref, wrg_ref,
                       brg_ref, wig_ref, big_ref, lam_ref, gmla_ref, glru_ref, wo_ref, gffn_ref, wr_ref, br_ref,
                       h1_ref, t_ref, route_ref, lru_ref, cnt_ref, *, t_new):
    w_ = LRU_WIDTH
    xs = [sconv_ref[:, k * w_:(k + 1) * w_] for k in range(CONV_W - 1)] + [xbr_ref[:, t * w_:(t + 1) * w_] for t in range(t_new)]
    xc = []
    for t in range(t_new):
        acc = bconv_ref[...] + xs[t] * wconv_ref[0:1, :]
        for k in range(1, CONV_W):
            acc = acc + xs[t + k] * wconv_ref[k:k + 1, :]
        xc.append(acc)
    xc = jnp.concatenate(xc, axis=0)
    a, u = _gates(xc, wrg_ref, brg_ref, wig_ref, big_ref, lam_ref)
    nb = xbr_ref.shape[0]
    h = slru_ref[...]
    hs = []
    for t in range(t_new):
        h = a[t * nb:(t + 1) * nb] * h + u[t * nb:(t + 1) * nb]
        hs.append(h)
    lru_ref[...] = h
    hs = jnp.concatenate(hs, axis=0)
    stack = lambda ref, width: jnp.concatenate([ref[:, t * width:(t + 1) * width] for t in range(t_new)], axis=0)
    o_lru = _gelu(stack(ybr_ref, w_)) * hs
    h1, tt, route, counts = _mix_tail(stack(omla_ref, MLA_WIDTH), o_lru, stack(x_ref, D_MODEL), cnt_in_ref[...],
                                      gmla_ref, glru_ref, wo_ref, gffn_ref, wr_ref, br_ref)
    cnt_ref[...] = counts
    for t in range(t_new):
        h1_ref[:, t * D_MODEL:(t + 1) * D_MODEL] = h1[t * nb:(t + 1) * nb]
        _store_token_rows(t_ref, tt[t * nb:(t + 1) * nb], t, nb, t_new)
        route_ref[:, t * ROUTE_LANES:(t + 1) * ROUTE_LANES] = route[t * nb:(t + 1) * nb]


def _mix_sample(t_all, xbr, ybr, omla, x2d, sconv, slru, counts, w, t_new):
    nb = slru.shape[0]
    n_s = nb * t_new
    assert t_all.shape[0] % (n_s * ROW_CHUNKS) == 0
    t_block = t_all.shape[0] // (n_s * ROW_CHUNKS) - 1
    full = lambda a: pl.BlockSpec(a.shape, lambda i: (0, 0))
    ins = (xbr, ybr, omla, x2d, sconv, slru, counts)
    n_w = len(_mix_weight_specs())
    return pl.pallas_call(
        _mix_sample_entry(t_new),
        out_shape=(jax.ShapeDtypeStruct((nb, t_new * D_MODEL), F32),
                   jax.ShapeDtypeStruct(t_all.shape, F32),
                   jax.ShapeDtypeStruct((nb, t_new * ROUTE_LANES), F32), jax.ShapeDtypeStruct((nb, LRU_WIDTH), F32),
                   jax.ShapeDtypeStruct((1, ROUTE_LANES), F32)),
        grid=(1,),
        in_specs=[full(a) for a in ins] + _mix_weight_specs() + [pl.BlockSpec(memory_space=pl.ANY)],
        out_specs=(pl.BlockSpec((nb, t_new * D_MODEL), lambda i: (0, 0)),
                   pl.BlockSpec((n_s * ROW_CHUNKS, LANES), lambda i: (t_block, 0)),
                   pl.BlockSpec((nb, t_new * ROUTE_LANES), lambda i: (0, 0)), pl.BlockSpec((nb, LRU_WIDTH), lambda i: (0, 0)),
                   pl.BlockSpec((1, ROUTE_LANES), lambda i: (0, 0))),
        input_output_aliases={len(ins) + n_w: 1},
        compiler_params=_cparams(dimension_semantics=("arbitrary",)),
        name="mix_sample",
    )(*ins, *_mix_weights(w), t_all)


def _mix_sample_entry(t_new):
    n_in = 7 + len(_mix_weight_specs())

    def entry(*refs):
        _mix_sample_kernel(*refs[:n_in], *refs[n_in + 1:], t_new=t_new)

    return entry


MOE_ROW_GROUP = 32


def _moe_kernel(te_ref, nu_ref, nv_ref, slot_ref, padlo_ref, padhi_ref, t_hbm, wg_ref, wu_ref, wd_ref, y_hbm,
                xa, xb, oa, ob, pair_ref, gsem, ssem, *, tm):
    del te_ref
    j = pl.program_id(0)
    n_used = nu_ref[0]
    xbufs, obufs = (xa, xb), (oa, ob)
    rc = ROW_CHUNKS
    spare = y_hbm.shape[0] - 2 * tm * rc
    n_pairs = slot_ref.shape[0]
    n_tok = n_pairs // 2

    def build_tables():
        pad_mask = 2 * tm - 1

        def spare_fill(r, carry):
            pair_ref[r] = (n_pairs + tm + r) * rc
            return carry

        lax.fori_loop(0, tm, spare_fill, 0)

        def expert_pads(e, carry):
            def pad(s, c):
                pair_ref[s + tm] = (n_pairs + jnp.bitwise_and(s, pad_mask)) * rc
                return c

            return lax.fori_loop(padlo_ref[e], padhi_ref[e], pad, carry)

        lax.fori_loop(0, N_EXPERTS, expert_pads, 0)

        def pairs(i, carry):
            for u in range(MOE_ROW_GROUP):
                p = i * MOE_ROW_GROUP + u
                pair_ref[slot_ref[p] + tm] = p * rc
            return carry

        lax.fori_loop(0, n_pairs // MOE_ROW_GROUP, pairs, 0)

    def rows(jj):
        return pl.multiple_of(nv_ref[jj + 2], MOE_ROW_GROUP)

    def row_groups(n, issue):
        for g in range(tm // MOE_ROW_GROUP):
            @pl.when(n > g * MOE_ROW_GROUP)
            def _(g=g):
                for u in range(MOE_ROW_GROUP):
                    issue(g * MOE_ROW_GROUP + u)

    def start_gather(jj, k):
        def issue(r):
            pr = pair_ref[(jj + 1) * tm + r]
            src = jnp.where(pr >= n_pairs * rc, pr - n_pairs * rc, jnp.where(pr >= n_tok * rc, pr - n_tok * rc, pr))
            pltpu.make_async_copy(t_hbm.at[pl.ds(pl.multiple_of(src, rc), rc)], xbufs[k].at[pl.ds(r * rc, rc)],
                                  gsem.at[k]).start()

        row_groups(rows(jj), issue)

    def wait_gather(jj, k):
        n = pl.multiple_of(rows(jj) * rc, 8)
        pltpu.make_async_copy(t_hbm.at[pl.ds(0, n)], xbufs[k].at[pl.ds(0, n)], gsem.at[k]).wait()

    def start_scatter(jj, k):
        def issue(r):
            dst = pl.multiple_of(pair_ref[(jj + 1) * tm + r], rc)
            pltpu.make_async_copy(obufs[k].at[pl.ds(r * rc, rc)], y_hbm.at[pl.ds(dst, rc)], ssem.at[k]).start(priority=1)

        row_groups(rows(jj), issue)

    def wait_scatter(jj, k):
        n = pl.multiple_of(rows(jj) * rc, 8)
        pltpu.make_async_copy(obufs[k].at[pl.ds(0, n)], y_hbm.at[pl.ds(0, n)], ssem.at[k]).wait()

    @pl.when(j == 0)
    def _():
        for buf in (xa, xb, oa, ob):
            buf[...] = jnp.zeros(buf.shape, F32)
        pltpu.make_async_copy(oa, y_hbm.at[pl.ds(spare, tm * rc)], ssem.at[0]).start()
        build_tables()
        start_gather(0, 0)

    def tile(k):
        wait_gather(j, k)
        wait_scatter(j - 2, k)
        start_gather(jnp.minimum(j + 1, n_used - 1), 1 - k)
        start_scatter(j - 1, 1 - k)
        x = _load_token_rows(xbufs[k], tm).astype(BF16)
        hg = _dot(x, wg_ref[...].astype(BF16))
        hu = _dot(x, wu_ref[...].astype(BF16))
        act = (hg * jax.nn.sigmoid(hg) * hu).astype(BF16)
        _store_token_rows(obufs[k], _dot(act, wd_ref[...].astype(BF16)), 0, tm, 1)

    def drain(k):
        start_scatter(j, k)
        wait_gather(j, 1 - k)
        wait_scatter(j - 1, 1 - k)
        wait_scatter(j, k)

    for k in range(2):
        pl.when(jnp.logical_and(j < n_used, j % 2 == k))(functools.partial(tile, k))
        pl.when(jnp.logical_and(j == n_used - 1, j % 2 == k))(functools.partial(drain, k))


def _moe(t_all, tile_expert, n_used, tile_rows, slot, pad_lo, pad_hi, w_gate, w_up, w_down, tm):
    n_tiles = tile_expert.shape[0]
    n_tok = t_all.shape[0] // ROW_CHUNKS
    wspec = lambda a, b: pl.BlockSpec((None, a, b), lambda j, te, nu, nv, sl, lo, hi: (te[j], 0, 0))
    grid_spec = pltpu.PrefetchScalarGridSpec(
        num_scalar_prefetch=6, grid=(n_tiles,),
        in_specs=[pl.BlockSpec(memory_space=pl.ANY), wspec(D_MODEL, D_EXPERT), wspec(D_MODEL, D_EXPERT),
                  wspec(D_EXPERT, D_MODEL)],
        out_specs=pl.BlockSpec(memory_space=pl.ANY),
        scratch_shapes=([pltpu.VMEM((tm * ROW_CHUNKS, LANES), F32)] * 4
                        + [pltpu.SMEM(((n_tiles + 1) * tm,), jnp.int32)]
                        + [pltpu.SemaphoreType.DMA((2,)), pltpu.SemaphoreType.DMA((2,))]))
    return pl.pallas_call(
        functools.partial(_moe_kernel, tm=tm),
        out_shape=jax.ShapeDtypeStruct(((2 * n_tok + 2 * tm) * ROW_CHUNKS, LANES), F32),
        grid_spec=grid_spec,
        compiler_params=_cparams(dimension_semantics=("arbitrary",)),
        name="moe",
    )(tile_expert, n_used, tile_rows, slot, pad_lo, pad_hi, t_all, w_gate, w_up, w_down)


def _moe_plan(route_all, counts, tm):
    n = route_all.shape[0]
    n_pairs = 2 * n
    eflat = jnp.concatenate([route_all[:, 0], route_all[:, 1]]).astype(jnp.int32)
    rank = jnp.concatenate([route_all[:, 4], route_all[:, 5]]).astype(jnp.int32)
    counts = counts[0, N_GROUPS:N_GROUPS + N_EXPERTS].astype(jnp.int32)
    tiles_e = (counts + tm - 1) // tm
    tile_end = jnp.cumsum(tiles_e)
    tile_start = tile_end - tiles_e
    n_used = tile_end[-1]
    n_tiles = n_pairs // tm + N_EXPERTS
    slot = tile_start[eflat] * tm + rank
    tid = jnp.arange(n_tiles, dtype=jnp.int32)
    te = jnp.minimum(jnp.sum((tile_end[None, :] <= tid[:, None]).astype(jnp.int32), axis=1), N_EXPERTS - 1)
    te = jnp.where(tid < n_used, te, te[jnp.maximum(n_used - 1, 0)])
    rows = jnp.where(tid < n_used, jnp.clip(counts[te] - (tid - tile_start[te]) * tm, 0, tm), 0)
    rows = (rows + MOE_ROW_GROUP - 1) // MOE_ROW_GROUP * MOE_ROW_GROUP
    tile_rows = jnp.concatenate([jnp.full((2,), tm, jnp.int32), rows.astype(jnp.int32)])
    pad_lo = (tile_start * tm + counts).astype(jnp.int32)
    pad_hi = (tile_end * tm).astype(jnp.int32)
    return te, n_used.reshape(1).astype(jnp.int32), tile_rows, slot.astype(jnp.int32), pad_lo, pad_hi


def _final_kernel(h1_ref, y0_ref, y1_ref, route_ref, p_ref, gple_ref, wpg_ref, bpg_ref, wpp_ref, gfin_ref, y_ref):
    route = route_ref[...]
    tm = h1_ref.shape[0]
    h2 = h1_ref[...] + (route[:, 2:3] * _load_token_rows(y0_ref, tm) + route[:, 3:4] * _load_token_rows(y1_ref, tm))
    gate = jax.nn.sigmoid(_dot(_rms(h2, gple_ref[...]).astype(BF16), wpg_ref[...]) + bpg_ref[...])
    h3 = h2 + gate * _dot(p_ref[...].astype(BF16), wpp_ref[...])
    y_ref[...] = _rms(h3, gfin_ref[...])


def _final(h1, y_pairs, route, p2d, w, blk0, n_all, tm):
    n = h1.shape[0]
    nb_all = n_all // tm
    row = lambda width: pl.BlockSpec((tm, width), lambda i: (i, 0))
    return pl.pallas_call(
        _final_kernel,
        out_shape=jax.ShapeDtypeStruct((n, D_MODEL), F32),
        grid=(n // tm,),
        in_specs=[row(D_MODEL),
                  pl.BlockSpec((tm * ROW_CHUNKS, LANES), lambda i: (blk0 + i, 0)),
                  pl.BlockSpec((tm * ROW_CHUNKS, LANES), lambda i: (nb_all + blk0 + i, 0)),
                  row(ROUTE_LANES), row(PLE_DIM), _const_spec((1, D_MODEL)), _const_spec((D_MODEL, D_MODEL)),
                  _const_spec((1, D_MODEL)), _const_spec((PLE_DIM, D_MODEL)), _const_spec((1, D_MODEL))],
        out_specs=row(D_MODEL),
        compiler_params=_cparams(dimension_semantics=("arbitrary",)),
        name="final",
    )(h1, y_pairs, y_pairs, route, p2d, w["g_ple"], w["w_ple_gate"], w["b_ple_gate"], w["w_ple_proj"], w["g_final"])


def _rope_tables(pos):
    half = QK_ROPE // 2
    inv = ROPE_THETA ** (-jnp.arange(half, dtype=F32) / half)
    ang = pos.astype(F32)[:, None] * inv[None, :]
    pad = jnp.zeros((pos.shape[0], ROPE_PAD - QK_ROPE), F32)
    cos = jnp.concatenate([jnp.cos(ang), jnp.cos(ang), pad], axis=1)
    sin = jnp.concatenate([jnp.sin(ang), jnp.sin(ang), pad], axis=1)
    return cos, sin


def _rot_cols(wr):
    half = QK_ROPE // 2
    return jnp.concatenate([-wr[..., half:], wr[..., :half]], axis=-1)


def _pad_cols(wr):
    return jnp.concatenate([wr, jnp.zeros(wr.shape[:-1] + (ROPE_PAD - QK_ROPE,), wr.dtype)], axis=-1)


def _prep_weights(g_mix, w_in, g_q, w_uq, g_kv, w_ukv, w_conv, b_conv, w_rg, b_rg, w_ig, b_ig, lru_lambda,
                  g_out_mla, g_out_lru, w_o, g_ffn, w_group, b_group, w_router, b_router, g_ple, w_ple_gate,
                  b_ple_gate, w_ple_proj, g_final):
    row = lambda v: v.reshape(1, -1).astype(F32)
    s0, s1, s2, s3 = Q_LORA, Q_LORA + KV_LORA, Q_LORA + KV_LORA + QK_ROPE, Q_LORA + KV_LORA + QK_ROPE + LRU_WIDTH
    w_in, w_uq, w_ukv = w_in.astype(BF16), w_uq.astype(BF16), w_ukv.astype(BF16)
    kr = w_in[:, s1:s2]
    w_in_ext = jnp.concatenate([w_in[:, :s1], _pad_cols(kr), _pad_cols(_rot_cols(kr)), w_in[:, s2:]], axis=1)
    uq = w_uq.reshape(Q_LORA, MLA_HEADS, QK_NOPE + QK_ROPE)
    uq_r = uq[..., QK_NOPE:]
    w_uq_ext = jnp.concatenate([uq[..., :QK_NOPE].reshape(Q_LORA, -1), _pad_cols(uq_r).reshape(Q_LORA, -1),
                                _pad_cols(_rot_cols(uq_r)).reshape(Q_LORA, -1)], axis=1)
    w_uk = jnp.transpose(w_ukv[..., :QK_NOPE], (1, 2, 0))
    w_uv = jnp.transpose(w_ukv[..., QK_NOPE:], (1, 0, 2))
    w_route = jnp.concatenate([w_group, w_router, jnp.zeros((D_MODEL, ROUTE_LANES - N_GROUPS - N_EXPERTS), F32)], axis=1)
    b_route = jnp.concatenate([b_group, b_router, jnp.zeros((ROUTE_LANES - N_GROUPS - N_EXPERTS,), F32)])
    return dict(
        g_mix=row(g_mix), w_in=w_in_ext.astype(BF16), g_q=row(g_q), w_uq=w_uq_ext.astype(BF16), g_kv=row(g_kv),
        w_uk=w_uk.astype(BF16), w_uv=w_uv.astype(BF16), w_conv=w_conv.astype(F32), b_conv=row(b_conv),
        w_rg=w_rg.astype(BF16), b_rg=row(b_rg), w_ig=w_ig.astype(BF16), b_ig=row(b_ig), lru_lambda=row(lru_lambda),
        g_out_mla=row(g_out_mla), g_out_lru=row(g_out_lru), w_o=w_o.astype(BF16), g_ffn=row(g_ffn),
        w_route=w_route.astype(BF16), b_route=row(b_route), g_ple=row(g_ple), w_ple_gate=w_ple_gate.astype(BF16),
        b_ple_gate=row(b_ple_gate), w_ple_proj=w_ple_proj.astype(BF16), g_final=row(g_final))


TM_PROJ = 512
TM_FINAL = 256
TQ_ATTN = 256
ATTN_HEAD_GROUP = 4
TC_MIX = 256
TM_MOE = 256
PAGES_PER_STEP = 32
SAMPLE_BUFFERS = 4
SAMPLE_SUB_BLOCKS = 4


def kernel(x_prompt, x_sample, p_prompt, p_sample, cache_latent, cache_krope, state_lru, state_conv, page_table, g_mix, w_in, g_q, w_uq, g_kv, w_ukv, w_conv, b_conv, w_rg, b_rg, w_ig, b_ig, lru_lambda, g_out_mla, g_out_lru, w_o, g_ffn, w_group, b_group, w_router, b_router, w_gate, w_up, w_down, g_ple, w_ple_gate, b_ple_gate, w_ple_proj, g_final):
    assert w_in.shape[0] == 1, "single trunk layer"
    batch, seq, _ = x_prompt.shape
    db, t_new, _ = x_sample.shape
    page = cache_latent.shape[2]
    past_len = page_table.shape[1] * page
    n_p, n_s = batch * seq, db * t_new
    n_all = n_p + n_s

    w = _prep_weights(g_mix[0], w_in[0], g_q[0], w_uq[0], g_kv[0], w_ukv[0], w_conv[0], b_conv[0], w_rg[0], b_rg[0],
                      w_ig[0], b_ig[0], lru_lambda[0], g_out_mla[0], g_out_lru[0], w_o[0], g_ffn[0], w_group[0],
                      b_group[0], w_router[0], b_router[0], g_ple[0], w_ple_gate[0], b_ple_gate[0], w_ple_proj[0],
                      g_final)

    xp = x_prompt.reshape(n_p, D_MODEL)
    cos_p, sin_p = _rope_tables(jnp.arange(seq))
    qcat_p, kcat_p, lat_p, krope_p, xbr_p, ybr_p = _proj(xp, cos_p, sin_p, seq // TM_PROJ, w, TM_PROJ)
    omla_p = _prompt_attn(qcat_p, kcat_p, w["w_uv"], batch, seq, TQ_ATTN)
    h1_p, t_p, route_p, lru_p, counts_p = _mix_prompt(xbr_p, ybr_p, omla_p, xp, w, batch, seq, TC_MIX, n_all)

    xs = x_sample.reshape(n_s, D_MODEL)
    tm_s = min(TM_PROJ, n_s)
    cos_s, sin_s = _rope_tables(past_len + (jnp.arange(tm_s) % t_new))
    qcat_s, kcat_s, lat_s, krope_s, xbr_s, ybr_s = _proj(xs, cos_s, sin_s, 1, w, tm_s)
    q_s = qcat_s.reshape(db, t_new, MLA_HEADS, QK_CAT).transpose(0, 2, 1, 3).reshape(db, MLA_HEADS * t_new, QK_CAT)
    knew = jnp.pad(kcat_s.reshape(db, t_new, QK_CAT), ((0, 0), (0, 8 - t_new), (0, 0)))
    cache_ropet = jnp.swapaxes(cache_krope[0], 1, 2)
    omla_s = _sample_attn(page_table, q_s, knew, cache_latent[0], cache_ropet, w["w_uv"], PAGES_PER_STEP)
    h1_s, t_all, route_s, lru_s, counts = _mix_sample(
        t_p, xbr_s.reshape(db, t_new * LRU_WIDTH), ybr_s.reshape(db, t_new * LRU_WIDTH),
        omla_s.reshape(db, t_new * MLA_WIDTH), x_sample.reshape(db, t_new * D_MODEL),
        state_conv[0].reshape(db, (CONV_W - 1) * LRU_WIDTH), state_lru[0], counts_p, w, t_new)
    h1_s = h1_s.reshape(n_s, D_MODEL)
    route_s = route_s.reshape(n_s, ROUTE_LANES)

    route_all = jnp.concatenate([route_p, route_s], axis=0)
    te, n_used, tile_rows, slot, pad_lo, pad_hi = _moe_plan(route_all, counts, TM_MOE)
    y_pairs = _moe(t_all, te, n_used, tile_rows, slot, pad_lo, pad_hi, w_gate[0], w_up[0], w_down[0], TM_MOE)

    y_p = _final(h1_p, y_pairs, route_p, p_prompt[0].reshape(n_p, PLE_DIM), w, 0, n_all, TM_FINAL)
    y_s = _final(h1_s, y_pairs, route_s, p_sample[0].reshape(n_s, PLE_DIM), w, n_p // TM_FINAL, n_all, TM_FINAL)

    new_conv_p = xbr_p.reshape(batch, seq, LRU_WIDTH)[:, seq - (CONV_W - 1):]
    hist = jnp.concatenate([state_conv[0], xbr_s.reshape(db, t_new, LRU_WIDTH)], axis=1)
    new_conv_s = hist[:, hist.shape[1] - (CONV_W - 1):]
    return (y_p.reshape(batch, seq, D_MODEL), y_s.reshape(db, t_new, D_MODEL),
            lat_p.reshape(1, batch, seq, KV_LORA), krope_p.reshape(1, batch, seq, QK_ROPE),
            lru_p.reshape(1, batch, LRU_WIDTH), new_conv_p[None],
            lat_s.reshape(1, db, t_new, KV_LORA), krope_s.reshape(1, db, t_new, QK_ROPE),
            lru_s[None], new_conv_s[None])
```
